```python
import math
import jax, jax.numpy as jnp
from jax import lax
import numpy as np

D_MODEL = 4096
BATCH = 4
SEQ = 4096
DEPTH = 1

CHUNK = 64

FOX_HEADS = 16
FOX_HEAD_DIM = 128
FOX_WIDTH = FOX_HEADS * FOX_HEAD_DIM
Q_BLOCK = 128

S5_GROUP = 16
S5_GROUPS = 64
S5_WIDTH = S5_GROUP * S5_GROUPS
S5_STATE = 64

D_FF = ((8 * D_MODEL + 3 * 256 - 1) // (3 * 256)) * 256

N_BRANCHES = 2
RMS_EPS = 1e-6
MASK_VALUE = -1e30

COL_Q = FOX_WIDTH
COL_K = COL_Q + FOX_WIDTH
COL_V = COL_K + FOX_WIDTH
COL_F = COL_V + FOX_HEADS
COL_S5 = COL_F + S5_WIDTH
COL_GATE_FOX = COL_S5 + D_MODEL
IN_COLS = COL_GATE_FOX + D_MODEL

kernel_name = "fox_s5_gated_hybrid_block"


def _rmsnorm(x, g):
    x32 = x.astype(jnp.float32)
    y = x32 * lax.rsqrt(jnp.mean(x32 * x32, axis=-1, keepdims=True) + RMS_EPS)
    return (y * g.astype(jnp.float32)).astype(x.dtype)


def _forgetting_attention(q, k, v, f_logit, q_norm, k_norm):
    B, S, _ = q.shape
    n_blk = S // Q_BLOCK
    q = _rmsnorm(q.reshape(B, S, FOX_HEADS, FOX_HEAD_DIM), q_norm) * (1.0 / math.sqrt(FOX_HEAD_DIM))
    k = _rmsnorm(k.reshape(B, S, FOX_HEADS, FOX_HEAD_DIM), k_norm)
    v = v.reshape(B, S, FOX_HEADS, FOX_HEAD_DIM)
    qh = q.transpose(0, 2, 1, 3)
    kh = k.transpose(0, 2, 1, 3)
    vh = v.transpose(0, 2, 1, 3)
    log_f = jax.nn.log_sigmoid(f_logit.astype(jnp.float32))
    F = jnp.cumsum(log_f, axis=1).transpose(0, 2, 1)
    q_blocks = qh.reshape(B, FOX_HEADS, n_blk, Q_BLOCK, FOX_HEAD_DIM).transpose(2, 0, 1, 3, 4)
    f_blocks = F.reshape(B, FOX_HEADS, n_blk, Q_BLOCK).transpose(2, 0, 1, 3)
    k_pos = jnp.arange(S)

    def one_block(args):
        qb, fb, blk = args
        q_pos = blk * Q_BLOCK + jnp.arange(Q_BLOCK)
        logits = jnp.einsum('bhqd,bhkd->bhqk', qb, kh).astype(jnp.float32)
        logits = logits + (fb[..., :, None] - F[:, :, None, :])
        logits = jnp.where(k_pos[None, :] <= q_pos[:, None], logits, MASK_VALUE)
        p = jax.nn.softmax(logits, axis=-1)
        return jnp.einsum('bhqk,bhkd->bhqd', p.astype(vh.dtype), vh)

    out = lax.map(one_block, (q_blocks, f_blocks, jnp.arange(n_blk)))
    return out.transpose(1, 0, 3, 2, 4).reshape(B, S, FOX_WIDTH)


def _cplx_scan_op(e1, e2):
    a1r, a1i, b1r, b1i = e1
    a2r, a2i, b2r, b2i = e2
    ar = a2r * a1r - a2i * a1i
    ai = a2r * a1i + a2i * a1r
    br = a2r * b1r - a2i * b1i + b2r
    bi = a2r * b1i + a2i * b1r + b2i
    return (ar, ai, br, bi)


def _s5(s5_in, lam_re, lam_im, log_step, b_re, b_im, c_re, c_im, d, w_glu, b_glu):
    B, S, _ = s5_in.shape
    u = s5_in.reshape(B, S, S5_GROUPS, S5_GROUP).astype(jnp.float32)
    lr = lam_re.astype(jnp.float32)
    li = lam_im.astype(jnp.float32)
    dt = jnp.exp(log_step.astype(jnp.float32))[:, None]
    mag = jnp.exp(lr * dt)
    lb_re = mag * jnp.cos(li * dt)
    lb_im = mag * jnp.sin(li * dt)
    denom = lr * lr + li * li
    num_re = lb_re - 1.0
    fac_re = (num_re * lr + lb_im * li) / denom
    fac_im = (lb_im * lr - num_re * li) / denom
    br = b_re.astype(jnp.float32)
    bi = b_im.astype(jnp.float32)
    bb_re = fac_re[..., None] * br - fac_im[..., None] * bi
    bb_im = fac_re[..., None] * bi + fac_im[..., None] * br
    bu_re = jnp.einsum('bsgi,gpi->bsgp', u, bb_re)
    bu_im = jnp.einsum('bsgi,gpi->bsgp', u, bb_im)
    a_re = jnp.broadcast_to(lb_re[None, None], (1, S, S5_GROUPS, S5_STATE))
    a_im = jnp.broadcast_to(lb_im[None, None], (1, S, S5_GROUPS, S5_STATE))
    _, _, x_re, x_im = lax.associative_scan(_cplx_scan_op, (a_re, a_im, bu_re, bu_im), axis=1)
    y = (jnp.einsum('bsgp,gip->bsgi', x_re, c_re.astype(jnp.float32))
         - jnp.einsum('bsgp,gip->bsgi', x_im, c_im.astype(jnp.float32))
         + d.astype(jnp.float32) * u)
    y = y.reshape(B, S, S5_WIDTH).astype(s5_in.dtype)
    g = jax.nn.gelu(y)
    return g * jax.nn.sigmoid(g @ w_glu + b_glu)


def _layer(x, g_mix, w_in, b_fgate, b_gates, q_norm, k_norm,
           s5_lambda_re, s5_lambda_im, s5_log_step, s5_b_re, s5_b_im, s5_c_re, s5_c_im, s5_d,
           w_glu, b_glu, w_proj_fox, w_proj_s5, w_out, g_ffn, w_gate_up, w_down):
    u = _rmsnorm(x, g_mix)
    z = u @ w_in
    q, k, v, f_logit, s5_in, gate_logits = jnp.split(
        z, [COL_Q, COL_K, COL_V, COL_F, COL_S5], axis=-1)
    gates = jax.nn.sigmoid(gate_logits.astype(jnp.float32) + b_gates.astype(jnp.float32))
    gate_fox, gate_s5 = jnp.split(gates, [D_MODEL], axis=-1)
    attn = _forgetting_attention(q, k, v, f_logit + b_fgate, q_norm, k_norm)
    ssm = _s5(s5_in, s5_lambda_re, s5_lambda_im, s5_log_step, s5_b_re, s5_b_im,
              s5_c_re, s5_c_im, s5_d, w_glu, b_glu)
    merged = (gate_fox * (attn @ w_proj_fox) + gate_s5 * (ssm @ w_proj_s5)).astype(x.dtype)
    h = x + merged @ w_out
    hn = _rmsnorm(h, g_ffn)
    gate, up = jnp.split(hn @ w_gate_up, [D_FF], axis=-1)
    return h + (jax.nn.silu(gate) * up) @ w_down


def _normal(key, shape, scale):
    return jax.random.normal(key, shape, jnp.float32) * scale


def setup_inputs(seed: int = 0) -> dict:
    key = jax.random.key(seed)
    ks = jax.random.split(key, 24)
    L = DEPTH
    n_idx = jnp.arange(S5_STATE, dtype=jnp.float32)
    lam_re = -0.5 + _normal(ks[8], (L, S5_GROUPS, S5_STATE), 0.01)
    lam_im = math.pi * n_idx[None, None, :] + _normal(ks[9], (L, S5_GROUPS, S5_STATE), 0.01)
    log_step = jax.random.uniform(ks[10], (L, S5_GROUPS), jnp.float32,
                                  math.log(1e-3), math.log(1e-1))
    return {
        "x": _normal(ks[0], (BATCH, SEQ, D_MODEL), 1.0),
        "g_mix": 1.0 + _normal(ks[1], (L, D_MODEL), 0.02),
        "w_in": _normal(ks[2], (L, D_MODEL, IN_COLS), D_MODEL ** -0.5),
        "b_fgate": 3.0 + _normal(ks[3], (L, FOX_HEADS), 0.5),
        "b_gates": _normal(ks[4], (L, N_BRANCHES * D_MODEL), 0.02),
        "q_norm": 1.0 + _normal(ks[5], (L, FOX_HEAD_DIM), 0.02),
        "k_norm": 1.0 + _normal(ks[6], (L, FOX_HEAD_DIM), 0.02),
        "s5_lambda_re": lam_re,
        "s5_lambda_im": lam_im,
        "s5_log_step": log_step,
        "s5_b_re": _normal(ks[11], (L, S5_GROUPS, S5_STATE, S5_GROUP), (2 * S5_GROUP) ** -0.5),
        "s5_b_im": _normal(ks[12], (L, S5_GROUPS, S5_STATE, S5_GROUP), (2 * S5_GROUP) ** -0.5),
        "s5_c_re": _normal(ks[13], (L, S5_GROUPS, S5_GROUP, S5_STATE), S5_STATE ** -0.5),
        "s5_c_im": _normal(ks[14], (L, S5_GROUPS, S5_GROUP, S5_STATE), S5_STATE ** -0.5),
        "s5_d": _normal(ks[15], (L, S5_GROUPS, S5_GROUP), 1.0),
        "w_glu": _normal(ks[16], (L, S5_WIDTH, S5_WIDTH), S5_WIDTH ** -0.5),
        "b_glu": _normal(ks[17], (L, S5_WIDTH), 0.02),
        "w_proj_fox": _normal(ks[18], (L, FOX_WIDTH, D_MODEL), FOX_WIDTH ** -0.5),
        "w_proj_s5": _normal(ks[19], (L, S5_WIDTH, D_MODEL), S5_WIDTH ** -0.5),
        "w_out": _normal(ks[20], (L, D_MODEL, D_MODEL), D_MODEL ** -0.5),
        "g_ffn": 1.0 + _normal(ks[21], (L, D_MODEL), 0.02),
        "w_gate_up": _normal(ks[22], (L, D_MODEL, 2 * D_FF), D_MODEL ** -0.5),
        "w_down": _normal(ks[23], (L, D_FF, D_MODEL), D_FF ** -0.5),
    }


def reference(x, g_mix, w_in, b_fgate, b_gates, q_norm, k_norm,
              s5_lambda_re, s5_lambda_im, s5_log_step, s5_b_re, s5_b_im, s5_c_re, s5_c_im, s5_d,
              w_glu, b_glu, w_proj_fox, w_proj_s5, w_out, g_ffn, w_gate_up, w_down):
    for l in range(DEPTH):
        x = _layer(x, g_mix[l], w_in[l], b_fgate[l], b_gates[l], q_norm[l], k_norm[l],
                   s5_lambda_re[l], s5_lambda_im[l], s5_log_step[l], s5_b_re[l], s5_b_im[l],
                   s5_c_re[l], s5_c_im[l], s5_d[l], w_glu[l], b_glu[l],
                   w_proj_fox[l], w_proj_s5[l], w_out[l], g_ffn[l], w_gate_up[l], w_down[l])
    return x
```

```python
import functools
import math

import jax
import jax.numpy as jnp
from jax import lax
from jax.experimental import pallas as pl
from jax.experimental.pallas import tpu as pltpu

F32 = jnp.float32
BF16 = jnp.bfloat16

FOX_HEADS = 16
FOX_HEAD_DIM = 128
S5_GROUP = 16
S5_GROUPS = 64
S5_STATE = 64
S5_CHUNK = 16
RMS_EPS = 1e-6
MASK_VALUE = -1e30

VMEM_LIMIT_BYTES = 56 * 1024 * 1024


def _params(*semantics):
    return pltpu.CompilerParams(dimension_semantics=semantics,
                                vmem_limit_bytes=VMEM_LIMIT_BYTES)


def _dot(a, b):
    return jnp.dot(a, b, preferred_element_type=F32)


def _rmsnorm_kernel(x_ref, g_ref, o_ref):
    x = x_ref[...]
    ms = jnp.mean(x * x, axis=-1, keepdims=True)
    o_ref[...] = (x * lax.rsqrt(ms + RMS_EPS) * g_ref[...]).astype(o_ref.dtype)


def _rmsnorm(x, g, *, bm=256):
    m, d = x.shape
    return pl.pallas_call(
        _rmsnorm_kernel,
        grid=(m // bm,),
        in_specs=[pl.BlockSpec((bm, d), lambda i: (i, 0)),
                  pl.BlockSpec((1, d), lambda i: (0, 0))],
        out_specs=pl.BlockSpec((bm, d), lambda i: (i, 0)),
        out_shape=jax.ShapeDtypeStruct((m, d), BF16),
        compiler_params=_params("parallel"),
        name="rmsnorm",
    )(x, g.reshape(1, d).astype(F32))


def _mm_kernel(a_ref, w_ref, o_ref):
    o_ref[...] = _dot(a_ref[...], w_ref[...]).astype(o_ref.dtype)


def _matmul(a, w, out_dtype, *, bm, bn, name):
    m, k = a.shape
    n = w.shape[1]
    return pl.pallas_call(
        _mm_kernel,
        grid=(m // bm, n // bn),
        in_specs=[pl.BlockSpec((bm, k), lambda i, j: (i, 0)),
                  pl.BlockSpec((k, bn), lambda i, j: (0, j))],
        out_specs=pl.BlockSpec((bm, bn), lambda i, j: (i, j)),
        out_shape=jax.ShapeDtypeStruct((m, n), out_dtype),
        compiler_params=_params("parallel", "arbitrary"),
        name=name,
    )(a, w)


def _mm_headnorm_kernel(a_ref, w_ref, s_ref, o_ref):
    acc = _dot(a_ref[...], w_ref[...])
    for h in range(acc.shape[1] // FOX_HEAD_DIM):
        cols = slice(h * FOX_HEAD_DIM, (h + 1) * FOX_HEAD_DIM)
        blk = acc[:, cols]
        ms = jnp.mean(blk * blk, axis=-1, keepdims=True)
        o_ref[:, cols] = (blk * lax.rsqrt(ms + RMS_EPS) * s_ref[:, cols]).astype(o_ref.dtype)


def _matmul_headnorm(a, w, scale, *, bm, bn):
    m, k = a.shape
    n = w.shape[1]
    return pl.pallas_call(
        _mm_headnorm_kernel,
        grid=(m // bm, n // bn),
        in_specs=[pl.BlockSpec((bm, k), lambda i, j: (i, 0)),
                  pl.BlockSpec((k, bn), lambda i, j: (0, j)),
                  pl.BlockSpec((1, bn), lambda i, j: (0, j))],
        out_specs=pl.BlockSpec((bm, bn), lambda i, j: (i, j)),
        out_shape=jax.ShapeDtypeStruct((m, n), BF16),
        compiler_params=_params("parallel", "arbitrary"),
        name="qk_proj_headnorm",
    )(a, w, scale)


def _fgate_kernel(u_ref, wt_ref, b_ref, o_ref, carry_ref):
    @pl.when(pl.program_id(1) == 0)
    def _():
        carry_ref[...] = jnp.zeros_like(carry_ref)

    z = lax.dot_general(wt_ref[...], u_ref[...], (((1,), (1,)), ((), ())),
                        preferred_element_type=F32) + b_ref[...]
    x = jnp.minimum(z, 0.0) - jnp.log1p(jnp.exp(-jnp.abs(z)))
    bs = x.shape[1]
    lane = lax.broadcasted_iota(jnp.int32, x.shape, 1)
    shift = 1
    while shift < bs:
        x = x + jnp.where(lane >= shift, pltpu.roll(x, shift, 1), 0.0)
        shift *= 2
    x = x + carry_ref[:, 0:1]
    o_ref[...] = x
    carry_ref[...] = jnp.broadcast_to(x[:, bs - 1:bs], carry_ref.shape)


def _forget_cumsum(u, wt, bias, batch, seq, *, bs=512):
    h, d = wt.shape
    ns = seq // bs
    return pl.pallas_call(
        _fgate_kernel,
        grid=(batch, ns),
        in_specs=[pl.BlockSpec((bs, d), lambda b, j: (b * ns + j, 0)),
                  pl.BlockSpec((h, d), lambda b, j: (0, 0)),
                  pl.BlockSpec((h, 1), lambda b, j: (0, 0))],
        out_specs=pl.BlockSpec((None, h, bs), lambda b, j: (b, 0, j)),
        out_shape=jax.ShapeDtypeStruct((batch, h, seq), F32),
        scratch_shapes=[pltpu.VMEM((h, 128), F32)],
        compiler_params=_params("parallel", "arbitrary"),
        name="forget_cumsum",
    )(u, wt, bias)


def _attn_kernel(q_ref, k_ref, v_ref, f_ref, o_ref, *, blk):
    i = pl.program_id(2)
    q = q_ref[...]

    def block(j, carry, masked):
        m, l, acc = carry
        start = pl.multiple_of(j * blk, blk)
        ks = k_ref[pl.ds(start, blk), :]
        vs = v_ref[pl.ds(start, blk), :]
        s = lax.dot_general(q, ks, (((1,), (1,)), ((), ())), preferred_element_type=F32)
        s = s - f_ref[j]
        if masked:
            row = lax.broadcasted_iota(jnp.int32, s.shape, 0)
            col = lax.broadcasted_iota(jnp.int32, s.shape, 1)
            s = jnp.where(col <= row, s, MASK_VALUE)
        m_new = jnp.maximum(m, jnp.max(s, axis=-1, keepdims=True))
        alpha = jnp.exp(m - m_new)
        p = jnp.exp(s - m_new)
        l = alpha * l + jnp.sum(p, axis=-1, keepdims=True)
        acc = alpha * acc + _dot(p.astype(BF16), vs)
        return m_new, l, acc

    init = (jnp.full((blk, 1), MASK_VALUE, F32), jnp.zeros((blk, 1), F32),
            jnp.zeros((blk, FOX_HEAD_DIM), F32))
    carry = lax.fori_loop(0, i, lambda j, c: block(j, c, False), init)
    _, l, acc = block(i, carry, True)
    o_ref[...] = (acc / l).astype(o_ref.dtype)


def _attention(qk, v, f_rows, *, blk=512):
    batch, seq, _ = v.shape
    nblk = seq // blk
    return pl.pallas_call(
        functools.partial(_attn_kernel, blk=blk),
        grid=(batch, FOX_HEADS, nblk),
        in_specs=[pl.BlockSpec((None, blk, FOX_HEAD_DIM), lambda b, h, i: (b, i, h)),
                  pl.BlockSpec((None, seq, FOX_HEAD_DIM), lambda b, h, i: (b, 0, h + FOX_HEADS)),
                  pl.BlockSpec((None, seq, FOX_HEAD_DIM), lambda b, h, i: (b, 0, h)),
                  pl.BlockSpec((None, None, nblk, 1, blk), lambda b, h, i: (b, h, 0, 0, 0))],
        out_specs=pl.BlockSpec((None, blk, FOX_HEAD_DIM), lambda b, h, i: (b, i, h)),
        out_shape=jax.ShapeDtypeStruct(v.shape, BF16),
        compiler_params=_params("parallel", "parallel", "arbitrary"),
        name="fox_attention",
    )(qk, qk, v, f_rows)


def _s5_kernel(u_ref, lam_ref, lamc_ref, ls_ref, bt_ref, ct_ref, d_ref, y_ref,
               sr_ref, si_ref, xr_ref, xi_ref, *, batch):
    hdot = functools.partial(jnp.dot, precision=lax.Precision.HIGHEST, preferred_element_type=F32)
    t_len, grp, p_dim = S5_CHUNK, S5_GROUP, S5_STATE
    width = t_len * grp
    shift = grp.bit_length() - 1
    dt = jnp.exp(ls_ref[...])

    def lam_pow(kf, lr, li):
        mag = jnp.exp(kf * (lr * dt))
        ang = kf * (li * dt)
        return mag * jnp.cos(ang), mag * jnp.sin(ang)

    lr, li = lam_ref[0:1, :], lam_ref[1:2, :]
    lbr, lbi = lam_pow(1.0, lr, li)
    den = lr * lr + li * li
    nr = lbr - 1.0
    fr = (nr * lr + lbi * li) / den
    fi = (lbi * lr - nr * li) / den

    row = lax.broadcasted_iota(jnp.int32, (width, p_dim), 0)
    row_step = (row >> shift).astype(F32)
    sel_t = (lax.broadcasted_iota(jnp.int32, (width, grp), 0) & (grp - 1)
             == lax.broadcasted_iota(jnp.int32, (width, grp), 1)).astype(F32)
    b_re = hdot(sel_t, bt_ref[0])
    b_im = hdot(sel_t, bt_ref[1])
    bb_re = fr * b_re - fi * b_im
    bb_im = fr * b_im + fi * b_re
    pk_r, pk_i = lam_pow(row_step, lr, li)
    w_re = bb_re * pk_r - bb_im * pk_i
    w_im = bb_re * pk_i + bb_im * pk_r
    pe_r, pe_i = lam_pow((t_len - 1.0) - row_step, lr, li)
    bpow_r = bb_re * pe_r - bb_im * pe_i
    bpow_i = bb_re * pe_i + bb_im * pe_r

    lrc, lic = lamc_ref[:, 0:1], lamc_ref[:, 1:2]
    lane = lax.broadcasted_iota(jnp.int32, (p_dim, width), 1)
    sel = (lax.broadcasted_iota(jnp.int32, (grp, width), 0)
           == lax.broadcasted_iota(jnp.int32, (grp, width), 1) & (grp - 1)).astype(F32)
    c_re = hdot(ct_ref[0], sel)
    c_im = hdot(ct_ref[1], sel)
    pt_r, pt_i = lam_pow((lane >> shift).astype(F32) + 1.0, lrc, lic)
    cpow_r = c_re * pt_r - c_im * pt_i
    cpow_i = -(c_re * pt_i + c_im * pt_r)

    gall = hdot(w_re, c_re) - hdot(w_im, c_im)
    r2 = lax.broadcasted_iota(jnp.int32, (width, width), 0)
    c2 = lax.broadcasted_iota(jnp.int32, (width, width), 1)
    lag = (c2 >> shift) - (r2 >> shift)
    mat = jnp.zeros((width, width), F32)
    for k in range(t_len):
        blk = gall[k * grp:(k + 1) * grp, :]
        mat = jnp.where(lag == k, jnp.concatenate([blk] * t_len, axis=0), mat)
    mat = mat + jnp.where(r2 == c2, hdot(d_ref[...], sel), 0.0)

    u = u_ref[...]
    sr_ref[...] = hdot(u, bpow_r)
    si_ref[...] = hdot(u, bpow_i)

    l16r, l16i = lam_pow(float(t_len), lr, li)
    ar = jnp.broadcast_to(l16r, (batch, p_dim))
    ai = jnp.broadcast_to(l16i, (batch, p_dim))

    def step(c, carry):
        xr, xi = carry
        r0 = pl.multiple_of(c * batch, batch)
        xr_ref[pl.ds(r0, batch), :] = xr
        xi_ref[pl.ds(r0, batch), :] = xi
        sr = sr_ref[pl.ds(r0, batch), :]
        si = si_ref[pl.ds(r0, batch), :]
        return ar * xr - ai * xi + sr, ar * xi + ai * xr + si

    zero = jnp.zeros((batch, p_dim), F32)
    lax.fori_loop(0, u.shape[0] // batch, step, (zero, zero))

    y_ref[...] = hdot(u, mat) + hdot(xr_ref[...], cpow_r) + hdot(xi_ref[...], cpow_i)


def _s5_scan(u, lam, lamc, log_step, bt, ct, d, *, batch):
    groups, rows, width = u.shape
    p = S5_STATE
    return pl.pallas_call(
        functools.partial(_s5_kernel, batch=batch),
        grid=(groups,),
        in_specs=[pl.BlockSpec((None, rows, width), lambda g: (g, 0, 0)),
                  pl.BlockSpec((None, 2, p), lambda g: (g, 0, 0)),
                  pl.BlockSpec((None, p, 2), lambda g: (g, 0, 0)),
                  pl.BlockSpec((None, 1, 1), lambda g: (g, 0, 0)),
                  pl.BlockSpec((None, 2, S5_GROUP, p), lambda g: (g, 0, 0, 0)),
                  pl.BlockSpec((None, 2, p, S5_GROUP), lambda g: (g, 0, 0, 0)),
                  pl.BlockSpec((None, 1, S5_GROUP), lambda g: (g, 0, 0))],
        out_specs=pl.BlockSpec((None, rows, width), lambda g: (g, 0, 0)),
        out_shape=jax.ShapeDtypeStruct(u.shape, F32),
        scratch_shapes=[pltpu.VMEM((rows, p), F32)] * 4,
        compiler_params=_params("parallel"),
        name="s5_chunked",
    )(u, lam, lamc, log_step, bt, ct, d)


def _glu_kernel(y_ref, w_ref, b_ref, o_ref):
    g = jax.nn.gelu(y_ref[...])
    o_ref[...] = (g * jax.nn.sigmoid(_dot(g.astype(BF16), w_ref[...]) + b_ref[...])).astype(o_ref.dtype)


def _glu(y, w, b, *, bm=512):
    m, n = y.shape
    return pl.pallas_call(
        _glu_kernel,
        grid=(m // bm,),
        in_specs=[pl.BlockSpec((bm, n), lambda i: (i, 0)),
                  pl.BlockSpec((n, n), lambda i: (0, 0)),
                  pl.BlockSpec((1, n), lambda i: (0, 0))],
        out_specs=pl.BlockSpec((bm, n), lambda i: (i, 0)),
        out_shape=jax.ShapeDtypeStruct((m, n), BF16),
        compiler_params=_params("parallel"),
        name="s5_glu",
    )(y, w, b)


def _merge_kernel(u_ref, a_ref, s_ref, wgf_ref, wgs_ref, wpf_ref, wps_ref, bgf_ref, bgs_ref, o_ref):
    u = u_ref[...]
    gate_fox = jax.nn.sigmoid(_dot(u, wgf_ref[...]) + bgf_ref[...])
    gate_s5 = jax.nn.sigmoid(_dot(u, wgs_ref[...]) + bgs_ref[...])
    o_ref[...] = (gate_fox * _dot(a_ref[...], wpf_ref[...])
                  + gate_s5 * _dot(s_ref[...], wps_ref[...])).astype(o_ref.dtype)


def _merge(u, attn, ssm, wgf, wgs, wpf, wps, bgf, bgs, *, bm=512, bn=512):
    m, d = u.shape
    n = wgf.shape[1]
    act = lambda a: pl.BlockSpec((bm, a.shape[1]), lambda i, j: (i, 0))
    wgt = lambda w: pl.BlockSpec((w.shape[0], bn), lambda i, j: (0, j))
    return pl.pallas_call(
        _merge_kernel,
        grid=(m // bm, n // bn),
        in_specs=[act(u), act(attn), act(ssm), wgt(wgf), wgt(wgs), wgt(wpf), wgt(wps),
                  pl.BlockSpec((1, bn), lambda i, j: (0, j)),
                  pl.BlockSpec((1, bn), lambda i, j: (0, j))],
        out_specs=pl.BlockSpec((bm, bn), lambda i, j: (i, j)),
        out_shape=jax.ShapeDtypeStruct((m, n), BF16),
        compiler_params=_params("parallel", "arbitrary"),
        name="gated_merge",
    )(u, attn, ssm, wgf, wgs, wpf, wps, bgf, bgs)


def _mm_residual_kernel(a_ref, w_ref, r_ref, o_ref):
    o_ref[...] = r_ref[...] + _dot(a_ref[...], w_ref[...])


def _matmul_residual(a, w, res, *, bm, bn):
    m, k = a.shape
    n = w.shape[1]
    return pl.pallas_call(
        _mm_residual_kernel,
        grid=(m // bm, n // bn),
        in_specs=[pl.BlockSpec((bm, k), lambda i, j: (i, 0)),
                  pl.BlockSpec((k, bn), lambda i, j: (0, j)),
                  pl.BlockSpec((bm, bn), lambda i, j: (i, j))],
        out_specs=pl.BlockSpec((bm, bn), lambda i, j: (i, j)),
        out_shape=jax.ShapeDtypeStruct((m, n), F32),
        compiler_params=_params("parallel", "arbitrary"),
        name="out_proj_residual",
    )(a, w, res)


def _swiglu_kernel(a_ref, wg_ref, wu_ref, o_ref):
    a = a_ref[...]
    gate = _dot(a, wg_ref[...])
    o_ref[...] = (jax.nn.silu(gate) * _dot(a, wu_ref[...])).astype(o_ref.dtype)


def _swiglu_up(a, w_gate_up, d_ff, *, bm, bn):
    m, k = a.shape
    nj = d_ff // bn
    return pl.pallas_call(
        _swiglu_kernel,
        grid=(m // bm, nj),
        in_specs=[pl.BlockSpec((bm, k), lambda i, j: (i, 0)),
                  pl.BlockSpec((k, bn), lambda i, j: (0, j)),
                  pl.BlockSpec((k, bn), lambda i, j: (0, j + nj))],
        out_specs=pl.BlockSpec((bm, bn), lambda i, j: (i, j)),
        out_shape=jax.ShapeDtypeStruct((m, d_ff), BF16),
        compiler_params=_params("parallel", "arbitrary"),
        name="swiglu_up",
    )(a, w_gate_up, w_gate_up)


def _down_kernel(a_ref, w_ref, r_ref, o_ref, acc_ref):
    kk = pl.program_id(2)

    @pl.when(kk == 0)
    def _():
        acc_ref[...] = r_ref[...]

    acc_ref[...] += _dot(a_ref[...], w_ref[...])

    @pl.when(kk == pl.num_programs(2) - 1)
    def _():
        o_ref[...] = acc_ref[...]


def _down_residual(a, w, res, *, bm, bn, bk):
    m, k = a.shape
    n = w.shape[1]
    return pl.pallas_call(
        _down_kernel,
        grid=(m // bm, n // bn, k // bk),
        in_specs=[pl.BlockSpec((bm, bk), lambda i, j, kk: (i, kk)),
                  pl.BlockSpec((bk, bn), lambda i, j, kk: (kk, j)),
                  pl.BlockSpec((bm, bn), lambda i, j, kk: (i, j))],
        out_specs=pl.BlockSpec((bm, bn), lambda i, j, kk: (i, j)),
        out_shape=jax.ShapeDtypeStruct((m, n), F32),
        scratch_shapes=[pltpu.VMEM((bm, bn), F32)],
        compiler_params=_params("parallel", "arbitrary", "arbitrary"),
        name="down_proj_residual",
    )(a, w, res)


def _layer(x, batch, seq, g_mix, w_in, b_fgate, b_gates, q_norm, k_norm,
           lam_re, lam_im, log_step, b_re, b_im, c_re, c_im, s5_d,
           w_glu, b_glu, w_proj_fox, w_proj_s5, w_out, g_ffn, w_gate_up, w_down):
    m, d_model = x.shape
    fox_w = FOX_HEADS * FOX_HEAD_DIM
    s5_w = S5_GROUP * S5_GROUPS
    col_k, col_v = fox_w, 2 * fox_w
    col_f = 3 * fox_w
    col_s5 = col_f + FOX_HEADS
    col_g = col_s5 + s5_w
    d_ff = w_down.shape[0]
    w_in_b = w_in.astype(BF16)

    u = _rmsnorm(x, g_mix)

    head_scale = jnp.concatenate([
        jnp.tile(q_norm.astype(F32), FOX_HEADS) * (1.0 / math.sqrt(FOX_HEAD_DIM)),
        jnp.tile(k_norm.astype(F32), FOX_HEADS)]).reshape(1, 2 * fox_w)
    qk = _matmul_headnorm(u, w_in_b[:, :col_v], head_scale, bm=1024, bn=1024)
    v = _matmul(u, w_in_b[:, col_v:col_f], BF16, bm=1024, bn=1024, name="v_proj")
    s5_in = _matmul(u, w_in_b[:, col_s5:col_g], F32, bm=1024, bn=1024, name="s5_in_proj")

    blk = 512
    f_t = _forget_cumsum(u, w_in_b[:, col_f:col_s5].T, b_fgate.astype(F32).reshape(FOX_HEADS, 1),
                         batch, seq)
    f_rows = f_t.reshape(batch, FOX_HEADS, seq // blk, 1, blk)
    qk3 = qk.reshape(batch, seq, 2 * fox_w)
    attn = _attention(qk3, v.reshape(batch, seq, fox_w), f_rows, blk=blk).reshape(m, fox_w)

    chunks = seq // S5_CHUNK
    u5 = s5_in.reshape(batch, chunks, S5_CHUNK, S5_GROUPS, S5_GROUP)
    u5 = u5.transpose(3, 1, 0, 2, 4).reshape(S5_GROUPS, chunks * batch, S5_CHUNK * S5_GROUP)
    lam = jnp.stack([lam_re, lam_im], axis=1).astype(F32)
    y5 = _s5_scan(u5, lam, lam.transpose(0, 2, 1), log_step.astype(F32).reshape(S5_GROUPS, 1, 1),
                  jnp.stack([b_re, b_im], axis=1).astype(F32).transpose(0, 1, 3, 2),
                  jnp.stack([c_re, c_im], axis=1).astype(F32).transpose(0, 1, 3, 2),
                  s5_d.astype(F32).reshape(S5_GROUPS, 1, S5_GROUP), batch=batch)
    y5 = y5.reshape(S5_GROUPS, chunks, batch, S5_CHUNK, S5_GROUP)
    y5 = y5.transpose(2, 1, 3, 0, 4).reshape(m, s5_w)
    ssm = _glu(y5, w_glu.astype(BF16), b_glu.astype(F32).reshape(1, s5_w))

    b_gates = b_gates.astype(F32).reshape(1, 2 * d_model)
    merged = _merge(u, attn, ssm, w_in_b[:, col_g:col_g + d_model], w_in_b[:, col_g + d_model:],
                    w_proj_fox.astype(BF16), w_proj_s5.astype(BF16),
                    b_gates[:, :d_model], b_gates[:, d_model:])
    h = _matmul_residual(merged, w_out.astype(BF16), x, bm=1024, bn=1024)

    hn = _rmsnorm(h, g_ffn)
    act = _swiglu_up(hn, w_gate_up.astype(BF16), d_ff, bm=1024, bn=256)
    return _down_residual(act, w_down.astype(BF16), h, bm=1024, bn=512, bk=d_ff // 2)


def kernel(x, g_mix, w_in, b_fgate, b_gates, q_norm, k_norm, s5_lambda_re, s5_lambda_im, s5_log_step, s5_b_re, s5_b_im, s5_c_re, s5_c_im, s5_d, w_glu, b_glu, w_proj_fox, w_proj_s5, w_out, g_ffn, w_gate_up, w_down):
    batch, seq, d_model = x.shape
    h = x.reshape(batch * seq, d_model)
    for l in range(g_mix.shape[0]):
        h = _layer(h, batch, seq, g_mix[l], w_in[l], b_fgate[l], b_gates[l], q_norm[l], k_norm[l],
                   s5_lambda_re[l], s5_lambda_im[l], s5_log_step[l], s5_b_re[l], s5_b_im[l],
                   s5_c_re[l], s5_c_im[l], s5_d[l], w_glu[l], b_glu[l],
                   w_proj_fox[l], w_proj_s5[l], w_out[l], g_ffn[l], w_gate_up[l], w_down[l])
    return h.reshape(batch, seq, d_model)
```

```python
import functools
import math

import jax
import jax.numpy as jnp
from jax import lax
from jax.experimental import pallas as pl
from jax.experimental.pallas import tpu as pltpu

F32 = jnp.float32
BF16 = jnp.bfloat16

FOX_HEADS = 16
FOX_HEAD_DIM = 128
S5_GROUP = 16
S5_GROUPS = 64
S5_STATE = 64
S5_CHUNK = 16
RMS_EPS = 1e-6
MASK_VALUE = -1e30
LOG2_E = math.log2(math.e)

VMEM_LIMIT_BYTES = 56 * 1024 * 1024


def _params(*semantics):
    return pltpu.CompilerParams(dimension_semantics=semantics,
                                vmem_limit_bytes=VMEM_LIMIT_BYTES)


def _dot(a, b):
    return jnp.dot(a, b, preferred_element_type=F32)


def _rmsnorm_kernel(x_ref, g_ref, o_ref):
    x = x_ref[...]
    ms = jnp.mean(x * x, axis=-1, keepdims=True)
    o_ref[...] = (x * lax.rsqrt(ms + RMS_EPS) * g_ref[...]).astype(o_ref.dtype)


def _rmsnorm(x, g, *, bm=256):
    m, d = x.shape
    return pl.pallas_call(
        _rmsnorm_kernel,
        grid=(m // bm,),
        in_specs=[pl.BlockSpec((bm, d), lambda i: (i, 0)),
                  pl.BlockSpec((1, d), lambda i: (0, 0))],
        out_specs=pl.BlockSpec((bm, d), lambda i: (i, 0)),
        out_shape=jax.ShapeDtypeStruct((m, d), BF16),
        compiler_params=_params("parallel"),
        name="rmsnorm",
    )(x, g.reshape(1, d).astype(F32))


def _mm_kernel(a_ref, w_ref, o_ref):
    o_ref[...] = _dot(a_ref[...], w_ref[...]).astype(o_ref.dtype)


def _matmul(a, w, out_dtype, *, bm, bn, name):
    m, k = a.shape
    n = w.shape[1]
    return pl.pallas_call(
        _mm_kernel,
        grid=(m // bm, n // bn),
        in_specs=[pl.BlockSpec((bm, k), lambda i, j: (i, 0)),
                  pl.BlockSpec((k, bn), lambda i, j: (0, j))],
        out_specs=pl.BlockSpec((bm, bn), lambda i, j: (i, j)),
        out_shape=jax.ShapeDtypeStruct((m, n), out_dtype),
        compiler_params=_params("parallel", "arbitrary"),
        name=name,
    )(a, w)


def _mm_headnorm_kernel(a_ref, w_ref, s_ref, o_ref):
    acc = _dot(a_ref[...], w_ref[...])
    for h in range(acc.shape[1] // FOX_HEAD_DIM):
        cols = slice(h * FOX_HEAD_DIM, (h + 1) * FOX_HEAD_DIM)
        blk = acc[:, cols]
        ms = jnp.mean(blk * blk, axis=-1, keepdims=True)
        o_ref[:, cols] = (blk * lax.rsqrt(ms + RMS_EPS) * s_ref[:, cols]).astype(o_ref.dtype)


def _matmul_headnorm(a, w, scale, *, bm, bn):
    m, k = a.shape
    n = w.shape[1]
    return pl.pallas_call(
        _mm_headnorm_kernel,
        grid=(m // bm, n // bn),
        in_specs=[pl.BlockSpec((bm, k), lambda i, j: (i, 0)),
                  pl.BlockSpec((k, bn), lambda i, j: (0, j)),
                  pl.BlockSpec((1, bn), lambda i, j: (0, j))],
        out_specs=pl.BlockSpec((bm, bn), lambda i, j: (i, j)),
        out_shape=jax.ShapeDtypeStruct((m, n), BF16),
        compiler_params=_params("parallel", "arbitrary"),
        name="qk_proj_headnorm",
    )(a, w, scale)


def _fgate_kernel(u_ref, wt_ref, b_ref, o_ref, carry_ref):
    @pl.when(pl.program_id(1) == 0)
    def _():
        carry_ref[...] = jnp.zeros_like(carry_ref)

    z = lax.dot_general(wt_ref[...], u_ref[...], (((1,), (1,)), ((), ())),
                        preferred_element_type=F32) + b_ref[...]
    x = jnp.minimum(z, 0.0) - jnp.log1p(jnp.exp(-jnp.abs(z)))
    bs = x.shape[1]
    lane = lax.broadcasted_iota(jnp.int32, x.shape, 1)
    shift = 1
    while shift < bs:
        x = x + jnp.where(lane >= shift, pltpu.roll(x, shift, 1), 0.0)
        shift *= 2
    x = x + carry_ref[:, 0:1]
    o_ref[...] = x * LOG2_E
    carry_ref[...] = jnp.broadcast_to(x[:, bs - 1:bs], carry_ref.shape)


def _forget_cumsum(u, wt, bias, batch, seq, *, bs=512):
    h, d = wt.shape
    ns = seq // bs
    return pl.pallas_call(
        _fgate_kernel,
        grid=(batch, ns),
        in_specs=[pl.BlockSpec((bs, d), lambda b, j: (b * ns + j, 0)),
                  pl.BlockSpec((h, d), lambda b, j: (0, 0)),
                  pl.BlockSpec((h, 1), lambda b, j: (0, 0))],
        out_specs=pl.BlockSpec((None, h, bs), lambda b, j: (b, 0, j)),
        out_shape=jax.ShapeDtypeStruct((batch, h, seq), F32),
        scratch_shapes=[pltpu.VMEM((h, 128), F32)],
        compiler_params=_params("parallel", "arbitrary"),
        name="forget_cumsum",
    )(u, wt, bias)


ATTN_HEADS_PER_STEP = 4


def _attn_kernel(q_ref, k_ref, v_ref, f_ref, o_ref, *, blk):
    i = pl.program_id(2)
    dh = FOX_HEAD_DIM
    heads = range(ATTN_HEADS_PER_STEP)
    qs = [q_ref[:, h * dh:(h + 1) * dh] for h in heads]

    def block(j, carry, masked):
        start = pl.multiple_of(j * blk, blk)
        out = []
        for h in heads:
            m, l, acc = carry[h]
            ks = k_ref[pl.ds(start, blk), h * dh:(h + 1) * dh]
            vs = v_ref[pl.ds(start, blk), h * dh:(h + 1) * dh]
            s = lax.dot_general(qs[h], ks, (((1,), (1,)), ((), ())), preferred_element_type=F32)
            s = s - f_ref[h, j]
            if masked:
                row = lax.broadcasted_iota(jnp.int32, s.shape, 0)
                col = lax.broadcasted_iota(jnp.int32, s.shape, 1)
                s = jnp.where(col <= row, s, MASK_VALUE)
            m_new = jnp.maximum(m, jnp.max(s, axis=-1, keepdims=True))
            alpha = jnp.exp2(m - m_new)
            p = jnp.exp2(s - m_new)
            l = alpha * l + jnp.sum(p, axis=-1, keepdims=True)
            acc = alpha * acc + _dot(p.astype(BF16), vs)
            out.append((m_new, l, acc))
        return tuple(out)

    init = tuple((jnp.full((blk, 1), MASK_VALUE, F32), jnp.zeros((blk, 1), F32),
                  jnp.zeros((blk, dh), F32)) for _ in heads)
    carry = lax.fori_loop(0, i, lambda j, c: block(j, c, False), init)
    carry = block(i, carry, True)
    for h in heads:
        _, l, acc = carry[h]
        o_ref[:, h * dh:(h + 1) * dh] = (acc / l).astype(o_ref.dtype)


def _attn_shift_kernel(c_ref, q_ref, k_ref, v_ref, f_ref, o_ref, acc_ref, shift_ref, p_ref, *, blk):
    i = pl.program_id(2)
    dh = FOX_HEAD_DIM
    heads = range(ATTN_HEADS_PER_STEP)
    qs = [q_ref[:, h * dh:(h + 1) * dh] for h in heads]
    for h in heads:
        col = jnp.broadcast_to(f_ref[h, i], (8, blk)).T[:, 0:1] - c_ref[0]
        shift_ref[h] = jnp.broadcast_to(col, (blk, dh))

    ones = jnp.ones((blk, dh), BF16)
    acc_ref[...] = jnp.zeros_like(acc_ref)

    def weights(j, masked):
        start = pl.multiple_of(j * blk, blk)
        for h in heads:
            ks = k_ref[pl.ds(start, blk), h * dh:(h + 1) * dh]
            s = lax.dot_general(qs[h], ks, (((1,), (1,)), ((), ())), preferred_element_type=F32)
            s = (s + jnp.concatenate([shift_ref[h]] * (blk // dh), axis=1)) - f_ref[h, j]
            if masked:
                row = lax.broadcasted_iota(jnp.int32, s.shape, 0)
                col = lax.broadcasted_iota(jnp.int32, s.shape, 1)
                s = jnp.where(col <= row, s, MASK_VALUE)
            p_ref[h] = jnp.exp2(s).astype(BF16)

    def accumulate(j):
        start = pl.multiple_of(j * blk, blk)
        for h in heads:
            vs = v_ref[pl.ds(start, blk), h * dh:(h + 1) * dh]
            acc_ref[h] += _dot(p_ref[h], jnp.concatenate([vs, ones], axis=1))

    @pl.when(i > 0)
    def _():
        weights(0, False)

    @pl.loop(1, i)
    def _(j):
        accumulate(j - 1)
        weights(j, False)

    @pl.when(i > 0)
    def _():
        accumulate(i - 1)

    weights(i, True)
    accumulate(i)
    for h in heads:
        acc = acc_ref[h]
        o_ref[:, h * dh:(h + 1) * dh] = (acc[:, :dh] / acc[:, dh:]).astype(o_ref.dtype)


ATTN_SHIFT_LIMIT = 56.0


def _attention(qk, v, f_rows, qk_bound, *, blk=512):
    batch, seq, _ = v.shape
    nblk = seq // blk
    hps = ATTN_HEADS_PER_STEP
    wide = hps * FOX_HEAD_DIM
    k_off = FOX_HEADS // hps
    specs = [pl.BlockSpec((None, blk, wide), lambda b, h, i: (b, i, h)),
             pl.BlockSpec((None, seq, wide), lambda b, h, i: (b, 0, h + k_off)),
             pl.BlockSpec((None, seq, wide), lambda b, h, i: (b, 0, h)),
             pl.BlockSpec((None, hps, nblk, 1, blk), lambda b, h, i: (b, h, 0, 0, 0))]
    common = dict(
        grid=(batch, FOX_HEADS // hps, nblk),
        out_specs=pl.BlockSpec((None, blk, wide), lambda b, h, i: (b, i, h)),
        out_shape=jax.ShapeDtypeStruct(v.shape, BF16),
        compiler_params=_params("parallel", "parallel", "arbitrary"))

    def shifted(qk, v, f_rows, c):
        return pl.pallas_call(
            functools.partial(_attn_shift_kernel, blk=blk),
            in_specs=[pl.BlockSpec(memory_space=pltpu.SMEM)] + specs,
            scratch_shapes=[pltpu.VMEM((hps, blk, 2 * FOX_HEAD_DIM), F32),
                            pltpu.VMEM((hps, blk, FOX_HEAD_DIM), F32),
                            pltpu.VMEM((hps, blk, blk), BF16)],
            name="fox_attention_shift", **common)(c.reshape(1), qk, qk, v, f_rows)

    def online(qk, v, f_rows, c):
        return pl.pallas_call(functools.partial(_attn_kernel, blk=blk), in_specs=specs,
                              name="fox_attention_online", **common)(qk, qk, v, f_rows)

    return lax.cond(qk_bound <= ATTN_SHIFT_LIMIT, shifted, online, qk, v, f_rows, qk_bound)


def _s5_kernel(u_ref, lam_ref, lamc_ref, ls_ref, bt_ref, ct_ref, d_ref, y_ref,
               sr_ref, si_ref, xr_ref, xi_ref, *, batch):
    hdot = functools.partial(jnp.dot, precision=lax.Precision.HIGHEST, preferred_element_type=F32)
    t_len, grp, p_dim = S5_CHUNK, S5_GROUP, S5_STATE
    width = t_len * grp
    shift = grp.bit_length() - 1
    dt = jnp.exp(ls_ref[...])

    def lam_pow(kf, lr, li):
        mag = jnp.exp(kf * (lr * dt))
        ang = kf * (li * dt)
        return mag * jnp.cos(ang), mag * jnp.sin(ang)

    lr, li = lam_ref[0:1, :], lam_ref[1:2, :]
    lbr, lbi = lam_pow(1.0, lr, li)
    den = lr * lr + li * li
    nr = lbr - 1.0
    fr = (nr * lr + lbi * li) / den
    fi = (lbi * lr - nr * li) / den

    row = lax.broadcasted_iota(jnp.int32, (width, p_dim), 0)
    row_step = (row >> shift).astype(F32)
    sel_t = (lax.broadcasted_iota(jnp.int32, (width, grp), 0) & (grp - 1)
             == lax.broadcasted_iota(jnp.int32, (width, grp), 1)).astype(F32)
    b_re = hdot(sel_t, bt_ref[0])
    b_im = hdot(sel_t, bt_ref[1])
    bb_re = fr * b_re - fi * b_im
    bb_im = fr * b_im + fi * b_re
    pk_r, pk_i = lam_pow(row_step, lr, li)
    w_re = bb_re * pk_r - bb_im * pk_i
    w_im = bb_re * pk_i + bb_im * pk_r
    pe_r, pe_i = lam_pow((t_len - 1.0) - row_step, lr, li)
    bpow_r = bb_re * pe_r - bb_im * pe_i
    bpow_i = bb_re * pe_i + bb_im * pe_r

    lrc, lic = lamc_ref[:, 0:1], lamc_ref[:, 1:2]
    lane = lax.broadcasted_iota(jnp.int32, (p_dim, width), 1)
    sel = (lax.broadcasted_iota(jnp.int32, (grp, width), 0)
           == lax.broadcasted_iota(jnp.int32, (grp, width), 1) & (grp - 1)).astype(F32)
    c_re = hdot(ct_ref[0], sel)
    c_im = hdot(ct_ref[1], sel)
    pt_r, pt_i = lam_pow((lane >> shift).astype(F32) + 1.0, lrc, lic)
    cpow_r = c_re * pt_r - c_im * pt_i
    cpow_i = -(c_re * pt_i + c_im * pt_r)

    gall = hdot(w_re, c_re) - hdot(w_im, c_im)
    r2 = lax.broadcasted_iota(jnp.int32, (width, width), 0)
    c2 = lax.broadcasted_iota(jnp.int32, (width, width), 1)
    lag = (c2 >> shift) - (r2 >> shift)
    mat = jnp.zeros((width, width), F32)
    for k in range(t_len):
        blk = gall[k * grp:(k + 1) * grp, :]
        mat = jnp.where(lag == k, jnp.concatenate([blk] * t_len, axis=0), mat)
    mat = mat + jnp.where(r2 == c2, hdot(d_ref[...], sel), 0.0)

    u = u_ref[...]
    sr_ref[...] = hdot(u, bpow_r)
    si_ref[...] = hdot(u, bpow_i)

    l16r, l16i = lam_pow(float(t_len), lr, li)
    ar = jnp.broadcast_to(l16r, (batch, p_dim))
    ai = jnp.broadcast_to(l16i, (batch, p_dim))

    def step(c, carry):
        xr, xi = carry
        r0 = pl.multiple_of(c * batch, batch)
        xr_ref[pl.ds(r0, batch), :] = xr
        xi_ref[pl.ds(r0, batch), :] = xi
        sr = sr_ref[pl.ds(r0, batch), :]
        si = si_ref[pl.ds(r0, batch), :]
        return ar * xr - ai * xi + sr, ar * xi + ai * xr + si

    zero = jnp.zeros((batch, p_dim), F32)
    lax.fori_loop(0, u.shape[0] // batch, step, (zero, zero))

    y_ref[...] = hdot(u, mat) + hdot(xr_ref[...], cpow_r) + hdot(xi_ref[...], cpow_i)


def _s5_scan(u, lam, lamc, log_step, bt, ct, d, *, batch):
    groups, rows, width = u.shape
    p = S5_STATE
    return pl.pallas_call(
        functools.partial(_s5_kernel, batch=batch),
        grid=(groups,),
        in_specs=[pl.BlockSpec((None, rows, width), lambda g: (g, 0, 0)),
                  pl.BlockSpec((None, 2, p), lambda g: (g, 0, 0)),
                  pl.BlockSpec((None, p, 2), lambda g: (g, 0, 0)),
                  pl.BlockSpec((None, 1, 1), lambda g: (g, 0, 0)),
                  pl.BlockSpec((None, 2, S5_GROUP, p), lambda g: (g, 0, 0, 0)),
                  pl.BlockSpec((None, 2, p, S5_GROUP), lambda g: (g, 0, 0, 0)),
                  pl.BlockSpec((None, 1, S5_GROUP), lambda g: (g, 0, 0))],
        out_specs=pl.BlockSpec((None, rows, width), lambda g: (g, 0, 0)),
        out_shape=jax.ShapeDtypeStruct(u.shape, F32),
        scratch_shapes=[pltpu.VMEM((rows, p), F32)] * 4,
        compiler_params=_params("parallel"),
        name="s5_chunked",
    )(u, lam, lamc, log_step, bt, ct, d)


def _glu_kernel(y_ref, w_ref, b_ref, o_ref):
    g = jax.nn.gelu(y_ref[...])
    o_ref[...] = (g * jax.nn.sigmoid(_dot(g.astype(BF16), w_ref[...]) + b_ref[...])).astype(o_ref.dtype)


def _glu(y, w, b, *, bm=512):
    m, n = y.shape
    return pl.pallas_call(
        _glu_kernel,
        grid=(m // bm,),
        in_specs=[pl.BlockSpec((bm, n), lambda i: (i, 0)),
                  pl.BlockSpec((n, n), lambda i: (0, 0)),
                  pl.BlockSpec((1, n), lambda i: (0, 0))],
        out_specs=pl.BlockSpec((bm, n), lambda i: (i, 0)),
        out_shape=jax.ShapeDtypeStruct((m, n), BF16),
        compiler_params=_params("parallel"),
        name="s5_glu",
    )(y, w, b)


def _merge_kernel(u_ref, a_ref, s_ref, wgf_ref, wgs_ref, wpf_ref, wps_ref, bgf_ref, bgs_ref, o_ref):
    u = u_ref[...]
    gate_fox = jax.nn.sigmoid(_dot(u, wgf_ref[...]) + bgf_ref[...])
    gate_s5 = jax.nn.sigmoid(_dot(u, wgs_ref[...]) + bgs_ref[...])
    o_ref[...] = (gate_fox * _dot(a_ref[...], wpf_ref[...])
                  + gate_s5 * _dot(s_ref[...], wps_ref[...])).astype(o_ref.dtype)


def _merge(u, attn, ssm, wgf, wgs, wpf, wps, bgf, bgs, *, bm=512, bn=512):
    m, d = u.shape
    n = wgf.shape[1]
    act = lambda a: pl.BlockSpec((bm, a.shape[1]), lambda i, j: (i, 0))
    wgt = lambda w: pl.BlockSpec((w.shape[0], bn), lambda i, j: (0, j))
    return pl.pallas_call(
        _merge_kernel,
        grid=(m // bm, n // bn),
        in_specs=[act(u), act(attn), act(ssm), wgt(wgf), wgt(wgs), wgt(wpf), wgt(wps),
                  pl.BlockSpec((1, bn), lambda i, j: (0, j)),
                  pl.BlockSpec((1, bn), lambda i, j: (0, j))],
        out_specs=pl.BlockSpec((bm, bn), lambda i, j: (i, j)),
        out_shape=jax.ShapeDtypeStruct((m, n), BF16),
        compiler_params=_params("parallel", "arbitrary"),
        name="gated_merge",
    )(u, attn, ssm, wgf, wgs, wpf, wps, bgf, bgs)


def _mm_residual_kernel(a_ref, w_ref, r_ref, o_ref):
    o_ref[...] = r_ref[...] + _dot(a_ref[...], w_ref[...])


def _matmul_residual(a, w, res, *, bm, bn):
    m, k = a.shape
    n = w.shape[1]
    return pl.pallas_call(
        _mm_residual_kernel,
        grid=(m // bm, n // bn),
        in_specs=[pl.BlockSpec((bm, k), lambda i, j: (i, 0)),
                  pl.BlockSpec((k, bn), lambda i, j: (0, j)),
                  pl.BlockSpec((bm, bn), lambda i, j: (i, j))],
        out_specs=pl.BlockSpec((bm, bn), lambda i, j: (i, j)),
        out_shape=jax.ShapeDtypeStruct((m, n), F32),
        compiler_params=_params("parallel", "arbitrary"),
        name="out_proj_residual",
    )(a, w, res)


def _swiglu_kernel(a_ref, wg_ref, wu_ref, o_ref):
    a = a_ref[...]
    gate = _dot(a, wg_ref[...])
    o_ref[...] = (jax.nn.silu(gate) * _dot(a, wu_ref[...])).astype(o_ref.dtype)


def _swiglu_up(a, w_gate_up, d_ff, *, bm, bn):
    m, k = a.shape
    nj = d_ff // bn
    return pl.pallas_call(
        _swiglu_kernel,
        grid=(m // bm, nj),
        in_specs=[pl.BlockSpec((bm, k), lambda i, j: (i, 0)),
                  pl.BlockSpec((k, bn), lambda i, j: (0, j)),
                  pl.BlockSpec((k, bn), lambda i, j: (0, j + nj))],
        out_specs=pl.BlockSpec((bm, bn), lambda i, j: (i, j)),
        out_shape=jax.ShapeDtypeStruct((m, d_ff), BF16),
        compiler_params=_params("parallel", "arbitrary"),
        name="swiglu_up",
    )(a, w_gate_up, w_gate_up)


def _down_kernel(a_ref, w_ref, r_ref, o_ref, acc_ref):
    kk = pl.program_id(2)

    @pl.when(kk == 0)
    def _():
        acc_ref[...] = r_ref[...]

    acc_ref[...] += _dot(a_ref[...], w_ref[...])

    @pl.when(kk == pl.num_programs(2) - 1)
    def _():
        o_ref[...] = acc_ref[...]


def _down_residual(a, w, res, *, bm, bn, bk):
    m, k = a.shape
    n = w.shape[1]
    return pl.pallas_call(
        _down_kernel,
        grid=(m // bm, n // bn, k // bk),
        in_specs=[pl.BlockSpec((bm, bk), lambda i, j, kk: (i, kk)),
                  pl.BlockSpec((bk, bn), lambda i, j, kk: (kk, j)),
                  pl.BlockSpec((bm, bn), lambda i, j, kk: (i, j))],
        out_specs=pl.BlockSpec((bm, bn), lambda i, j, kk: (i, j)),
        out_shape=jax.ShapeDtypeStruct((m, n), F32),
        scratch_shapes=[pltpu.VMEM((bm, bn), F32)],
        compiler_params=_params("parallel", "arbitrary", "arbitrary"),
        name="down_proj_residual",
    )(a, w, res)


def _layer(x, batch, seq, g_mix, w_in, b_fgate, b_gates, q_norm, k_norm,
           lam_re, lam_im, log_step, b_re, b_im, c_re, c_im, s5_d,
           w_glu, b_glu, w_proj_fox, w_proj_s5, w_out, g_ffn, w_gate_up, w_down):
    m, d_model = x.shape
    fox_w = FOX_HEADS * FOX_HEAD_DIM
    s5_w = S5_GROUP * S5_GROUPS
    col_k, col_v = fox_w, 2 * fox_w
    col_f = 3 * fox_w
    col_s5 = col_f + FOX_HEADS
    col_g = col_s5 + s5_w
    d_ff = w_down.shape[0]
    w_in_b = w_in.astype(BF16)

    u = _rmsnorm(x, g_mix)

    head_scale = jnp.concatenate([
        jnp.tile(q_norm.astype(F32), FOX_HEADS) * (LOG2_E / math.sqrt(FOX_HEAD_DIM)),
        jnp.tile(k_norm.astype(F32), FOX_HEADS)]).reshape(1, 2 * fox_w)
    qk = _matmul_headnorm(u, w_in_b[:, :col_v], head_scale, bm=1024, bn=1024)
    v = _matmul(u, w_in_b[:, col_v:col_f], BF16, bm=1024, bn=1024, name="v_proj")
    s5_in = _matmul(u, w_in_b[:, col_s5:col_g], F32, bm=1024, bn=1024, name="s5_in_proj")

    blk = 512
    f_t = _forget_cumsum(u, w_in_b[:, col_f:col_s5].T, b_fgate.astype(F32).reshape(FOX_HEADS, 1),
                         batch, seq)
    f_rows = f_t.reshape(batch, FOX_HEADS, seq // blk, 1, blk)
    qk3 = qk.reshape(batch, seq, 2 * fox_w)
    qk_bound = (1.02 * LOG2_E * math.sqrt(FOX_HEAD_DIM)
                * jnp.max(jnp.abs(q_norm.astype(F32))) * jnp.max(jnp.abs(k_norm.astype(F32))))
    attn = _attention(qk3, v.reshape(batch, seq, fox_w), f_rows, qk_bound, blk=blk).reshape(m, fox_w)

    chunks = seq // S5_CHUNK
    u5 = s5_in.reshape(batch, chunks, S5_CHUNK, S5_GROUPS, S5_GROUP)
    u5 = u5.transpose(3, 1, 0, 2, 4).reshape(S5_GROUPS, chunks * batch, S5_CHUNK * S5_GROUP)
    lam = jnp.stack([lam_re, lam_im], axis=1).astype(F32)
    y5 = _s5_scan(u5, lam, lam.transpose(0, 2, 1), log_step.astype(F32).reshape(S5_GROUPS, 1, 1),
                  jnp.stack([b_re, b_im], axis=1).astype(F32).transpose(0, 1, 3, 2),
                  jnp.stack([c_re, c_im], axis=1).astype(F32).transpose(0, 1, 3, 2),
                  s5_d.astype(F32).reshape(S5_GROUPS, 1, S5_GROUP), batch=batch)
    y5 = y5.reshape(S5_GROUPS, chunks, batch, S5_CHUNK, S5_GROUP)
    y5 = y5.transpose(2, 1, 3, 0, 4).reshape(m, s5_w)
    ssm = _glu(y5, w_glu.astype(BF16), b_glu.astype(F32).reshape(1, s5_w))

    b_gates = b_gates.astype(F32).reshape(1, 2 * d_model)
    merged = _merge(u, attn, ssm, w_in_b[:, col_g:col_g + d_model], w_in_b[:, col_g + d_model:],
                    w_proj_fox.astype(BF16), w_proj_s5.astype(BF16),
                    b_gates[:, :d_model], b_gates[:, d_model:])
    h = _matmul_residual(merged, w_out.astype(BF16), x, bm=1024, bn=1024)

    hn = _rmsnorm(h, g_ffn)
    act = _swiglu_up(hn, w_gate_up.astype(BF16), d_ff, bm=1024, bn=256)
    return _down_residual(act, w_down.astype(BF16), h, bm=1024, bn=512, bk=d_ff // 2)


def kernel(x, g_mix, w_in, b_fgate, b_gates, q_norm, k_norm, s5_lambda_re, s5_lambda_im, s5_log_step, s5_b_re, s5_b_im, s5_c_re, s5_c_im, s5_d, w_glu, b_glu, w_proj_fox, w_proj_s5, w_out, g_ffn, w_gate_up, w_down):
    batch, seq, d_model = x.shape
    h = x.reshape(batch * seq, d_model)
    for l in range(g_mix.shape[0]):
        h = _layer(h, batch, seq, g_mix[l], w_in[l], b_fgate[l], b_gates[l], q_norm[l], k_norm[l],
                   s5_lambda_re[l], s5_lambda_im[l], s5_log_step[l], s5_b_re[l], s5_b_im[l],
                   s5_c_re[l], s5_c_im[l], s5_d[l], w_glu[l], b_glu[l],
                   w_proj_fox[l], w_proj_s5[l], w_out[l], g_ffn[l], w_gate_up[l], w_down[l])
    return h.reshape(batch, seq, d_model)
```

```python
import functools
import math

import jax
import jax.numpy as jnp
from jax import lax
from jax.experimental import pallas as pl
from jax.experimental.pallas import tpu as pltpu

F32 = jnp.float32
BF16 = jnp.bfloat16

FOX_HEADS = 16
FOX_HEAD_DIM = 128
S5_GROUP = 16
S5_GROUPS = 64
S5_STATE = 64
S5_CHUNK = 16
RMS_EPS = 1e-6
MASK_VALUE = -1e30
LOG2_E = math.log2(math.e)

VMEM_LIMIT_BYTES = 56 * 1024 * 1024


def _params(*semantics):
    return pltpu.CompilerParams(dimension_semantics=semantics,
                                vmem_limit_bytes=VMEM_LIMIT_BYTES)


def _dot(a, b):
    return jnp.dot(a, b, preferred_element_type=F32)


def _rmsnorm_kernel(x_ref, g_ref, o_ref):
    x = x_ref[...]
    ms = jnp.mean(x * x, axis=-1, keepdims=True)
    o_ref[...] = (x * lax.rsqrt(ms + RMS_EPS) * g_ref[...]).astype(o_ref.dtype)


def _rmsnorm(x, g, *, bm=256):
    m, d = x.shape
    return pl.pallas_call(
        _rmsnorm_kernel,
        grid=(m // bm,),
        in_specs=[pl.BlockSpec((bm, d), lambda i: (i, 0)),
                  pl.BlockSpec((1, d), lambda i: (0, 0))],
        out_specs=pl.BlockSpec((bm, d), lambda i: (i, 0)),
        out_shape=jax.ShapeDtypeStruct((m, d), BF16),
        compiler_params=_params("parallel"),
        name="rmsnorm",
    )(x, g.reshape(1, d).astype(F32))


def _mm_kernel(a_ref, w_ref, o_ref):
    o_ref[...] = _dot(a_ref[...], w_ref[...]).astype(o_ref.dtype)


def _matmul(a, w, out_dtype, *, bm, bn, name):
    m, k = a.shape
    n = w.shape[1]
    return pl.pallas_call(
        _mm_kernel,
        grid=(m // bm, n // bn),
        in_specs=[pl.BlockSpec((bm, k), lambda i, j: (i, 0)),
                  pl.BlockSpec((k, bn), lambda i, j: (0, j))],
        out_specs=pl.BlockSpec((bm, bn), lambda i, j: (i, j)),
        out_shape=jax.ShapeDtypeStruct((m, n), out_dtype),
        compiler_params=_params("parallel", "arbitrary"),
        name=name,
    )(a, w)


def _mm_headnorm_kernel(a_ref, w_ref, s_ref, o_ref):
    acc = _dot(a_ref[...], w_ref[...])
    for h in range(acc.shape[1] // FOX_HEAD_DIM):
        cols = slice(h * FOX_HEAD_DIM, (h + 1) * FOX_HEAD_DIM)
        blk = acc[:, cols]
        ms = jnp.mean(blk * blk, axis=-1, keepdims=True)
        o_ref[:, cols] = (blk * lax.rsqrt(ms + RMS_EPS) * s_ref[:, cols]).astype(o_ref.dtype)


def _matmul_headnorm(a, w, scale, *, bm, bn):
    m, k = a.shape
    n = w.shape[1]
    return pl.pallas_call(
        _mm_headnorm_kernel,
        grid=(m // bm, n // bn),
        in_specs=[pl.BlockSpec((bm, k), lambda i, j: (i, 0)),
                  pl.BlockSpec((k, bn), lambda i, j: (0, j)),
                  pl.BlockSpec((1, bn), lambda i, j: (0, j))],
        out_specs=pl.BlockSpec((bm, bn), lambda i, j: (i, j)),
        out_shape=jax.ShapeDtypeStruct((m, n), BF16),
        compiler_params=_params("parallel", "arbitrary"),
        name="qk_proj_headnorm",
    )(a, w, scale)


def _fgate_kernel(u_ref, wt_ref, b_ref, o_ref, carry_ref):
    @pl.when(pl.program_id(1) == 0)
    def _():
        carry_ref[...] = jnp.zeros_like(carry_ref)

    z = lax.dot_general(wt_ref[...], u_ref[...], (((1,), (1,)), ((), ())),
                        preferred_element_type=F32) + b_ref[...]
    x = jnp.minimum(z, 0.0) - jnp.log1p(jnp.exp(-jnp.abs(z)))
    bs = x.shape[1]
    lane = lax.broadcasted_iota(jnp.int32, x.shape, 1)
    shift = 1
    while shift < bs:
        x = x + jnp.where(lane >= shift, pltpu.roll(x, shift, 1), 0.0)
        shift *= 2
    x = x + carry_ref[:, 0:1]
    o_ref[...] = x * LOG2_E
    carry_ref[...] = jnp.broadcast_to(x[:, bs - 1:bs], carry_ref.shape)


def _forget_cumsum(u, wt, bias, batch, seq, *, bs=512):
    h, d = wt.shape
    ns = seq // bs
    return pl.pallas_call(
        _fgate_kernel,
        grid=(batch, ns),
        in_specs=[pl.BlockSpec((bs, d), lambda b, j: (b * ns + j, 0)),
                  pl.BlockSpec((h, d), lambda b, j: (0, 0)),
                  pl.BlockSpec((h, 1), lambda b, j: (0, 0))],
        out_specs=pl.BlockSpec((None, h, bs), lambda b, j: (b, 0, j)),
        out_shape=jax.ShapeDtypeStruct((batch, h, seq), F32),
        scratch_shapes=[pltpu.VMEM((h, 128), F32)],
        compiler_params=_params("parallel", "arbitrary"),
        name="forget_cumsum",
    )(u, wt, bias)


ATTN_HEADS_PER_STEP = 4


def _attn_kernel(q_ref, k_ref, v_ref, f_ref, o_ref, *, blk):
    i = pl.program_id(2)
    dh = FOX_HEAD_DIM
    heads = range(ATTN_HEADS_PER_STEP)
    qs = [q_ref[:, h * dh:(h + 1) * dh] for h in heads]

    def block(j, carry, masked):
        start = pl.multiple_of(j * blk, blk)
        out = []
        for h in heads:
            m, l, acc = carry[h]
            ks = k_ref[pl.ds(start, blk), h * dh:(h + 1) * dh]
            vs = v_ref[pl.ds(start, blk), h * dh:(h + 1) * dh]
            s = lax.dot_general(qs[h], ks, (((1,), (1,)), ((), ())), preferred_element_type=F32)
            s = s - f_ref[h, j]
            if masked:
                row = lax.broadcasted_iota(jnp.int32, s.shape, 0)
                col = lax.broadcasted_iota(jnp.int32, s.shape, 1)
                s = jnp.where(col <= row, s, MASK_VALUE)
            m_new = jnp.maximum(m, jnp.max(s, axis=-1, keepdims=True))
            alpha = jnp.exp2(m - m_new)
            p = jnp.exp2(s - m_new)
            l = alpha * l + jnp.sum(p, axis=-1, keepdims=True)
            acc = alpha * acc + _dot(p.astype(BF16), vs)
            out.append((m_new, l, acc))
        return tuple(out)

    init = tuple((jnp.full((blk, 1), MASK_VALUE, F32), jnp.zeros((blk, 1), F32),
                  jnp.zeros((blk, dh), F32)) for _ in heads)
    carry = lax.fori_loop(0, i, lambda j, c: block(j, c, False), init)
    carry = block(i, carry, True)
    for h in heads:
        _, l, acc = carry[h]
        o_ref[:, h * dh:(h + 1) * dh] = (acc / l).astype(o_ref.dtype)


def _attn_shift_kernel(c_ref, q_ref, k_ref, v_ref, f_ref, o_ref, acc_ref, shift_ref, p_ref, *, blk):
    i = pl.program_id(2)
    dh = FOX_HEAD_DIM
    heads = range(ATTN_HEADS_PER_STEP)
    qs = [q_ref[:, h * dh:(h + 1) * dh] for h in heads]
    for h in heads:
        col = jnp.broadcast_to(f_ref[h, i], (8, blk)).T[:, 0:1] - c_ref[0]
        shift_ref[h] = jnp.broadcast_to(col, (blk, dh))

    ones = jnp.ones((blk, dh), BF16)
    acc_ref[...] = jnp.zeros_like(acc_ref)

    def weights(j, masked):
        start = pl.multiple_of(j * blk, blk)
        for h in heads:
            ks = k_ref[pl.ds(start, blk), h * dh:(h + 1) * dh]
            s = lax.dot_general(qs[h], ks, (((1,), (1,)), ((), ())), preferred_element_type=F32)
            s = (s + jnp.concatenate([shift_ref[h]] * (blk // dh), axis=1)) - f_ref[h, j]
            if masked:
                row = lax.broadcasted_iota(jnp.int32, s.shape, 0)
                col = lax.broadcasted_iota(jnp.int32, s.shape, 1)
                s = jnp.where(col <= row, s, MASK_VALUE)
            p_ref[h] = jnp.exp2(s).astype(BF16)

    def accumulate(j):
        start = pl.multiple_of(j * blk, blk)
        for h in heads:
            vs = v_ref[pl.ds(start, blk), h * dh:(h + 1) * dh]
            acc_ref[h] += _dot(p_ref[h], jnp.concatenate([vs, ones], axis=1))

    @pl.when(i > 0)
    def _():
        weights(0, False)

    @pl.loop(1, i)
    def _(j):
        accumulate(j - 1)
        weights(j, False)

    @pl.when(i > 0)
    def _():
        accumulate(i - 1)

    weights(i, True)
    accumulate(i)
    for h in heads:
        acc = acc_ref[h]
        o_ref[:, h * dh:(h + 1) * dh] = (acc[:, :dh] / acc[:, dh:]).astype(o_ref.dtype)


ATTN_SHIFT_LIMIT = 56.0


def _attention(qk, v, f_rows, qk_bound, *, blk=512):
    batch, seq, _ = v.shape
    nblk = seq // blk
    hps = ATTN_HEADS_PER_STEP
    wide = hps * FOX_HEAD_DIM
    k_off = FOX_HEADS // hps
    specs = [pl.BlockSpec((None, blk, wide), lambda b, h, i: (b, i, h)),
             pl.BlockSpec((None, seq, wide), lambda b, h, i: (b, 0, h + k_off)),
             pl.BlockSpec((None, seq, wide), lambda b, h, i: (b, 0, h)),
             pl.BlockSpec((None, hps, nblk, 1, blk), lambda b, h, i: (b, h, 0, 0, 0))]
    common = dict(
        grid=(batch, FOX_HEADS // hps, nblk),
        out_specs=pl.BlockSpec((None, blk, wide), lambda b, h, i: (b, i, h)),
        out_shape=jax.ShapeDtypeStruct(v.shape, BF16),
        compiler_params=_params("parallel", "parallel", "arbitrary"))

    def shifted(qk, v, f_rows, c):
        return pl.pallas_call(
            functools.partial(_attn_shift_kernel, blk=blk),
            in_specs=[pl.BlockSpec(memory_space=pltpu.SMEM)] + specs,
            scratch_shapes=[pltpu.VMEM((hps, blk, 2 * FOX_HEAD_DIM), F32),
                            pltpu.VMEM((hps, blk, FOX_HEAD_DIM), F32),
                            pltpu.VMEM((hps, blk, blk), BF16)],
            name="fox_attention_shift", **common)(c.reshape(1), qk, qk, v, f_rows)

    def online(qk, v, f_rows, c):
        return pl.pallas_call(functools.partial(_attn_kernel, blk=blk), in_specs=specs,
                              name="fox_attention_online", **common)(qk, qk, v, f_rows)

    return lax.cond(qk_bound <= ATTN_SHIFT_LIMIT, shifted, online, qk, v, f_rows, qk_bound)


S5_SLAB_GROUPS = 16
S5_SLAB = S5_SLAB_GROUPS * S5_GROUP
S5_SLAB_STATES = S5_SLAB_GROUPS * S5_STATE
S5_SLABS = S5_GROUPS // S5_SLAB_GROUPS


def _dot_split(a, b):
    a_hi = a.astype(BF16)
    b_hi = b.astype(BF16)
    a_lo = (a - a_hi.astype(F32)).astype(BF16)
    b_lo = (b - b_hi.astype(F32)).astype(BF16)
    return _dot(a_hi, b_hi) + _dot(a_hi, b_lo) + _dot(a_lo, b_hi)


def _lam_pow(k, lr, li, dt):
    mag = jnp.exp(k * (lr * dt))
    ang = k * (li * dt)
    return mag * jnp.cos(ang), mag * jnp.sin(ang)


def _s5_matrices_kernel(lam_ref, ls_ref, bdb_ref, bdc_ref, bp_ref, cp_ref, gr_ref, l16_ref):
    t = pl.program_id(1)
    ns = S5_SLAB_STATES
    lr, li = lam_ref[0:1, :], lam_ref[1:2, :]
    dt = jnp.exp(ls_ref[...])
    lbr, lbi = _lam_pow(1.0, lr, li, dt)
    den = lr * lr + li * li
    nr = lbr - 1.0
    fr = (nr * lr + lbi * li) / den
    fi = (lbi * lr - nr * li) / den
    b_re, b_im = bdb_ref[0], bdb_ref[1]
    bb_re = fr * b_re - fi * b_im
    bb_im = fr * b_im + fi * b_re
    pr, pi = _lam_pow((S5_CHUNK - 1 - t).astype(F32), lr, li, dt)
    bp_re = bb_re * pr - bb_im * pi
    bp_im = bb_re * pi + bb_im * pr
    bp_ref[:, :ns] = bp_re.astype(BF16)
    bp_ref[:, ns:] = bp_im.astype(BF16)

    qr, qi = _lam_pow((t + 1).astype(F32), lr, li, dt)
    qr = jnp.broadcast_to(qr, (8, ns)).T[:, 0:1]
    qi = jnp.broadcast_to(qi, (8, ns)).T[:, 0:1]
    c_re, c_im = bdc_ref[0], bdc_ref[1]
    cp_ref[:ns, :] = (c_re * qr - c_im * qi).astype(BF16)
    cp_ref[ns:, :] = (-(c_re * qi + c_im * qr)).astype(BF16)

    gr_ref[...] = (_dot_split(bp_re, c_re) - _dot_split(bp_im, c_im)).astype(BF16)

    l16r, l16i = _lam_pow(float(S5_CHUNK), lr, li, dt)
    l16_ref[0:1, :] = l16r
    l16_ref[1:2, :] = l16i


def _s5_matrices(lam, ls, bdb, bdc):
    t_len, w, ns = S5_CHUNK, S5_SLAB, S5_SLAB_STATES
    return pl.pallas_call(
        _s5_matrices_kernel,
        grid=(S5_SLABS, t_len),
        in_specs=[pl.BlockSpec((None, 2, ns), lambda s, t: (s, 0, 0)),
                  pl.BlockSpec((None, 1, ns), lambda s, t: (s, 0, 0)),
                  pl.BlockSpec((None, 2, w, ns), lambda s, t: (s, 0, 0, 0)),
                  pl.BlockSpec((None, 2, ns, w), lambda s, t: (s, 0, 0, 0))],
        out_specs=[pl.BlockSpec((None, w, 2 * ns), lambda s, t: (s, t, 0)),
                   pl.BlockSpec((None, None, 2 * ns, w), lambda s, t: (s, t, 0, 0)),
                   pl.BlockSpec((None, w, w), lambda s, t: (s, t, 0)),
                   pl.BlockSpec((None, 2, ns), lambda s, t: (s, 0, 0))],
        out_shape=[jax.ShapeDtypeStruct((S5_SLABS, t_len * w, 2 * ns), BF16),
                   jax.ShapeDtypeStruct((S5_SLABS, t_len, 2 * ns, w), BF16),
                   jax.ShapeDtypeStruct((S5_SLABS, t_len * w, w), BF16),
                   jax.ShapeDtypeStruct((S5_SLABS, 2, ns), F32)],
        compiler_params=_params("parallel", "arbitrary"),
        name="s5_matrices",
    )(lam, ls, bdb, bdc)


def _chunk_rows(x_ref):
    return jnp.concatenate([x_ref[:, t, :] for t in range(S5_CHUNK)], axis=1).astype(BF16)


def _s5_states_kernel(x_ref, bp_ref, s_ref):
    s_ref[...] = _dot(_chunk_rows(x_ref), bp_ref[...])


def _s5_states(x3, bp, *, rows):
    nc = x3.shape[0]
    t_len, w, ns = S5_CHUNK, S5_SLAB, S5_SLAB_STATES
    return pl.pallas_call(
        _s5_states_kernel,
        grid=(S5_SLABS, nc // rows),
        in_specs=[pl.BlockSpec((rows, t_len, w), lambda s, r: (r, 0, s)),
                  pl.BlockSpec((None, t_len * w, 2 * ns), lambda s, r: (s, 0, 0),
                               pipeline_mode=pl.Buffered(1))],
        out_specs=pl.BlockSpec((rows, 2 * ns), lambda s, r: (r, s)),
        out_shape=jax.ShapeDtypeStruct((nc, S5_SLABS * 2 * ns), F32),
        compiler_params=_params("parallel", "arbitrary"),
        name="s5_chunk_states",
    )(x3, bp)


def _s5_chunk_scan_kernel(s_ref, l16_ref, xp_ref, *, batch, chunks):
    ns = S5_SLAB_STATES
    ar, ai = l16_ref[0:1, :], l16_ref[1:2, :]

    def step(c, carry):
        out = []
        for b, (xr, xi) in enumerate(carry):
            row = pl.ds(b * chunks + c, 1)
            xp_ref[row, :ns] = xr
            xp_ref[row, ns:] = xi
            sr = s_ref[row, :ns]
            si = s_ref[row, ns:]
            out.append((ar * xr - ai * xi + sr, ar * xi + ai * xr + si))
        return tuple(out)

    zero = jnp.zeros((1, ns), F32)
    lax.fori_loop(0, chunks, step, ((zero, zero),) * batch)


def _s5_chunk_scan(s, l16, *, batch):
    nc = s.shape[0]
    ns = S5_SLAB_STATES
    return pl.pallas_call(
        functools.partial(_s5_chunk_scan_kernel, batch=batch, chunks=nc // batch),
        grid=(S5_SLABS,),
        in_specs=[pl.BlockSpec((nc, 2 * ns), lambda s_: (0, s_)),
                  pl.BlockSpec((None, 2, ns), lambda s_: (s_, 0, 0))],
        out_specs=pl.BlockSpec((nc, 2 * ns), lambda s_: (0, s_)),
        out_shape=jax.ShapeDtypeStruct(s.shape, F32),
        compiler_params=_params("parallel"),
        name="s5_chunk_scan",
    )(s, l16)


def _s5_output_kernel(x_ref, xp_ref, gr_ref, cp_ref, d_ref, y_ref):
    t_len, w = S5_CHUNK, S5_SLAB
    xc = _chunk_rows(x_ref)
    xp = xp_ref[...].astype(BF16)
    d = d_ref[...]
    for t in range(t_len):
        y = _dot(xc[:, :(t + 1) * w], gr_ref[(t_len - 1 - t) * w:, :])
        y = y + _dot(xp, cp_ref[t])
        y_ref[:, t, :] = y + d * x_ref[:, t, :]


def _s5_output(x3, xp, gr, cp, d, *, rows):
    nc = x3.shape[0]
    t_len, w, ns = S5_CHUNK, S5_SLAB, S5_SLAB_STATES
    once = dict(pipeline_mode=pl.Buffered(1))
    return pl.pallas_call(
        _s5_output_kernel,
        grid=(S5_SLABS, nc // rows),
        in_specs=[pl.BlockSpec((rows, t_len, w), lambda s, r: (r, 0, s)),
                  pl.BlockSpec((rows, 2 * ns), lambda s, r: (r, s)),
                  pl.BlockSpec((None, t_len * w, w), lambda s, r: (s, 0, 0), **once),
                  pl.BlockSpec((None, t_len, 2 * ns, w), lambda s, r: (s, 0, 0, 0), **once),
                  pl.BlockSpec((None, 1, w), lambda s, r: (s, 0, 0))],
        out_specs=pl.BlockSpec((rows, t_len, w), lambda s, r: (r, 0, s)),
        out_shape=jax.ShapeDtypeStruct(x3.shape, F32),
        compiler_params=_params("parallel", "arbitrary"),
        name="s5_chunk_output",
    )(x3, xp, gr, cp, d)


def _s5(s5_in, lam_re, lam_im, log_step, b_re, b_im, c_re, c_im, d, *, batch):
    m, width = s5_in.shape
    sl, sg, p, grp = S5_SLABS, S5_SLAB_GROUPS, S5_STATE, S5_GROUP
    ns = S5_SLAB_STATES
    lam = jnp.stack([lam_re, lam_im], axis=0).astype(F32).reshape(2, sl, ns).transpose(1, 0, 2)
    ls = jnp.repeat(log_step.astype(F32), p).reshape(sl, 1, ns)
    eye = jnp.eye(sg, dtype=F32)
    b = jnp.stack([b_re, b_im], axis=0).astype(F32).reshape(2, sl, sg, p, grp)
    bdb = jnp.einsum('rsgpi,gh->srgihp', b, eye).reshape(sl, 2, sg * grp, ns)
    c = jnp.stack([c_re, c_im], axis=0).astype(F32).reshape(2, sl, sg, grp, p)
    bdc = jnp.einsum('rsgop,gh->srhpgo', c, eye).reshape(sl, 2, ns, sg * grp)
    bp, cp, gr, l16 = _s5_matrices(lam, ls, bdb, bdc)
    x3 = s5_in.reshape(m // S5_CHUNK, S5_CHUNK, width)
    states = _s5_states(x3, bp, rows=512)
    xprev = _s5_chunk_scan(states, l16, batch=batch)
    y3 = _s5_output(x3, xprev, gr, cp, d.astype(F32).reshape(sl, 1, sg * grp), rows=256)
    return y3.reshape(m, width)


def _glu_kernel(y_ref, w_ref, b_ref, o_ref):
    g = jax.nn.gelu(y_ref[...])
    o_ref[...] = (g * jax.nn.sigmoid(_dot(g.astype(BF16), w_ref[...]) + b_ref[...])).astype(o_ref.dtype)


def _glu(y, w, b, *, bm=512):
    m, n = y.shape
    return pl.pallas_call(
        _glu_kernel,
        grid=(m // bm,),
        in_specs=[pl.BlockSpec((bm, n), lambda i: (i, 0)),
                  pl.BlockSpec((n, n), lambda i: (0, 0)),
                  pl.BlockSpec((1, n), lambda i: (0, 0))],
        out_specs=pl.BlockSpec((bm, n), lambda i: (i, 0)),
        out_shape=jax.ShapeDtypeStruct((m, n), BF16),
        compiler_params=_params("parallel"),
        name="s5_glu",
    )(y, w, b)


def _merge_kernel(u_ref, a_ref, s_ref, wgf_ref, wgs_ref, wpf_ref, wps_ref, bgf_ref, bgs_ref, o_ref):
    u = u_ref[...]
    gate_fox = jax.nn.sigmoid(_dot(u, wgf_ref[...]) + bgf_ref[...])
    gate_s5 = jax.nn.sigmoid(_dot(u, wgs_ref[...]) + bgs_ref[...])
    o_ref[...] = (gate_fox * _dot(a_ref[...], wpf_ref[...])
                  + gate_s5 * _dot(s_ref[...], wps_ref[...])).astype(o_ref.dtype)


def _merge(u, attn, ssm, wgf, wgs, wpf, wps, bgf, bgs, *, bm=512, bn=512):
    m, d = u.shape
    n = wgf.shape[1]
    act = lambda a: pl.BlockSpec((bm, a.shape[1]), lambda i, j: (i, 0))
    wgt = lambda w: pl.BlockSpec((w.shape[0], bn), lambda i, j: (0, j))
    return pl.pallas_call(
        _merge_kernel,
        grid=(m // bm, n // bn),
        in_specs=[act(u), act(attn), act(ssm), wgt(wgf), wgt(wgs), wgt(wpf), wgt(wps),
                  pl.BlockSpec((1, bn), lambda i, j: (0, j)),
                  pl.BlockSpec((1, bn), lambda i, j: (0, j))],
        out_specs=pl.BlockSpec((bm, bn), lambda i, j: (i, j)),
        out_shape=jax.ShapeDtypeStruct((m, n), BF16),
        compiler_params=_params("parallel", "arbitrary"),
        name="gated_merge",
    )(u, attn, ssm, wgf, wgs, wpf, wps, bgf, bgs)


def _mm_residual_kernel(a_ref, w_ref, r_ref, o_ref):
    o_ref[...] = r_ref[...] + _dot(a_ref[...], w_ref[...])


def _matmul_residual(a, w, res, *, bm, bn):
    m, k = a.shape
    n = w.shape[1]
    return pl.pallas_call(
        _mm_residual_kernel,
        grid=(m // bm, n // bn),
        in_specs=[pl.BlockSpec((bm, k), lambda i, j: (i, 0)),
                  pl.BlockSpec((k, bn), lambda i, j: (0, j)),
                  pl.BlockSpec((bm, bn), lambda i, j: (i, j))],
        out_specs=pl.BlockSpec((bm, bn), lambda i, j: (i, j)),
        out_shape=jax.ShapeDtypeStruct((m, n), F32),
        compiler_params=_params("parallel", "arbitrary"),
        name="out_proj_residual",
    )(a, w, res)


def _swiglu_kernel(a_ref, wg_ref, wu_ref, o_ref):
    a = a_ref[...]
    gate = _dot(a, wg_ref[...])
    o_ref[...] = (jax.nn.silu(gate) * _dot(a, wu_ref[...])).astype(o_ref.dtype)


def _swiglu_up(a, w_gate_up, d_ff, *, bm, bn):
    m, k = a.shape
    nj = d_ff // bn
    return pl.pallas_call(
        _swiglu_kernel,
        grid=(m // bm, nj),
        in_specs=[pl.BlockSpec((bm, k), lambda i, j: (i, 0)),
                  pl.BlockSpec((k, bn), lambda i, j: (0, j)),
                  pl.BlockSpec((k, bn), lambda i, j: (0, j + nj))],
        out_specs=pl.BlockSpec((bm, bn), lambda i, j: (i, j)),
        out_shape=jax.ShapeDtypeStruct((m, d_ff), BF16),
        compiler_params=_params("parallel", "arbitrary"),
        name="swiglu_up",
    )(a, w_gate_up, w_gate_up)


def _down_kernel(a_ref, w_ref, r_ref, o_ref, acc_ref):
    kk = pl.program_id(2)

    @pl.when(kk == 0)
    def _():
        acc_ref[...] = r_ref[...]

    acc_ref[...] += _dot(a_ref[...], w_ref[...])

    @pl.when(kk == pl.num_programs(2) - 1)
    def _():
        o_ref[...] = acc_ref[...]


def _down_residual(a, w, res, *, bm, bn, bk):
    m, k = a.shape
    n = w.shape[1]
    return pl.pallas_call(
        _down_kernel,
        grid=(m // bm, n // bn, k // bk),
        in_specs=[pl.BlockSpec((bm, bk), lambda i, j, kk: (i, kk)),
                  pl.BlockSpec((bk, bn), lambda i, j, kk: (kk, j)),
                  pl.BlockSpec((bm, bn), lambda i, j, kk: (i, j))],
        out_specs=pl.BlockSpec((bm, bn), lambda i, j, kk: (i, j)),
        out_shape=jax.ShapeDtypeStruct((m, n), F32),
        scratch_shapes=[pltpu.VMEM((bm, bn), F32)],
        compiler_params=_params("parallel", "arbitrary", "arbitrary"),
        name="down_proj_residual",
    )(a, w, res)


def _layer(x, batch, seq, g_mix, w_in, b_fgate, b_gates, q_norm, k_norm,
           lam_re, lam_im, log_step, b_re, b_im, c_re, c_im, s5_d,
           w_glu, b_glu, w_proj_fox, w_proj_s5, w_out, g_ffn, w_gate_up, w_down):
    m, d_model = x.shape
    fox_w = FOX_HEADS * FOX_HEAD_DIM
    s5_w = S5_GROUP * S5_GROUPS
    col_k, col_v = fox_w, 2 * fox_w
    col_f = 3 * fox_w
    col_s5 = col_f + FOX_HEADS
    col_g = col_s5 + s5_w
    d_ff = w_down.shape[0]
    w_in_b = w_in.astype(BF16)

    u = _rmsnorm(x, g_mix)

    head_scale = jnp.concatenate([
        jnp.tile(q_norm.astype(F32), FOX_HEADS) * (LOG2_E / math.sqrt(FOX_HEAD_DIM)),
        jnp.tile(k_norm.astype(F32), FOX_HEADS)]).reshape(1, 2 * fox_w)
    qk = _matmul_headnorm(u, w_in_b[:, :col_v], head_scale, bm=1024, bn=1024)
    v = _matmul(u, w_in_b[:, col_v:col_f], BF16, bm=1024, bn=1024, name="v_proj")
    s5_in = _matmul(u, w_in_b[:, col_s5:col_g], F32, bm=1024, bn=1024, name="s5_in_proj")

    blk = 512
    f_t = _forget_cumsum(u, w_in_b[:, col_f:col_s5].T, b_fgate.astype(F32).reshape(FOX_HEADS, 1),
                         batch, seq)
    f_rows = f_t.reshape(batch, FOX_HEADS, seq // blk, 1, blk)
    qk3 = qk.reshape(batch, seq, 2 * fox_w)
    qk_bound = (1.02 * LOG2_E * math.sqrt(FOX_HEAD_DIM)
                * jnp.max(jnp.abs(q_norm.astype(F32))) * jnp.max(jnp.abs(k_norm.astype(F32))))
    attn = _attention(qk3, v.reshape(batch, seq, fox_w), f_rows, qk_bound, blk=blk).reshape(m, fox_w)

    y5 = _s5(s5_in, lam_re, lam_im, log_step, b_re, b_im, c_re, c_im, s5_d, batch=batch)
    ssm = _glu(y5, w_glu.astype(BF16), b_glu.astype(F32).reshape(1, s5_w))

    b_gates = b_gates.astype(F32).reshape(1, 2 * d_model)
    merged = _merge(u, attn, ssm, w_in_b[:, col_g:col_g + d_model], w_in_b[:, col_g + d_model:],
                    w_proj_fox.astype(BF16), w_proj_s5.astype(BF16),
                    b_gates[:, :d_model], b_gates[:, d_model:])
    h = _matmul_residual(merged, w_out.astype(BF16), x, bm=1024, bn=1024)

    hn = _rmsnorm(h, g_ffn)
    act = _swiglu_up(hn, w_gate_up.astype(BF16), d_ff, bm=1024, bn=256)
    return _down_residual(act, w_down.astype(BF16), h, bm=1024, bn=512, bk=d_ff // 2)


def kernel(x, g_mix, w_in, b_fgate, b_gates, q_norm, k_norm, s5_lambda_re, s5_lambda_im, s5_log_step, s5_b_re, s5_b_im, s5_c_re, s5_c_im, s5_d, w_glu, b_glu, w_proj_fox, w_proj_s5, w_out, g_ffn, w_gate_up, w_down):
    batch, seq, d_model = x.shape
    h = x.reshape(batch * seq, d_model)
    for l in range(g_mix.shape[0]):
        h = _layer(h, batch, seq, g_mix[l], w_in[l], b_fgate[l], b_gates[l], q_norm[l], k_norm[l],
                   s5_lambda_re[l], s5_lambda_im[l], s5_log_step[l], s5_b_re[l], s5_b_im[l],
                   s5_c_re[l], s5_c_im[l], s5_d[l], w_glu[l], b_glu[l],
                   w_proj_fox[l], w_proj_s5[l], w_out[l], g_ffn[l], w_gate_up[l], w_down[l])
    return h.reshape(batch, seq, d_model)
```

```python
import functools
import math

import jax
import jax.numpy as jnp
from jax import lax
from jax.experimental import pallas as pl
from jax.experimental.pallas import tpu as pltpu

F32 = jnp.float32
BF16 = jnp.bfloat16

FOX_HEADS = 16
FOX_HEAD_DIM = 128
S5_GROUP = 16
S5_GROUPS = 64
S5_STATE = 64
S5_CHUNK = 16
RMS_EPS = 1e-6
MASK_VALUE = -1e30
LOG2_E = math.log2(math.e)

VMEM_LIMIT_BYTES = 56 * 1024 * 1024


def _params(*semantics):
    return pltpu.CompilerParams(dimension_semantics=semantics,
                                vmem_limit_bytes=VMEM_LIMIT_BYTES)


def _dot(a, b):
    return jnp.dot(a, b, preferred_element_type=F32)


def _rmsnorm_kernel(x_ref, g_ref, o_ref):
    x = x_ref[...]
    ms = jnp.mean(x * x, axis=-1, keepdims=True)
    o_ref[...] = (x * lax.rsqrt(ms + RMS_EPS) * g_ref[...]).astype(o_ref.dtype)


def _rmsnorm(x, g, *, bm=256):
    m, d = x.shape
    return pl.pallas_call(
        _rmsnorm_kernel,
        grid=(m // bm,),
        in_specs=[pl.BlockSpec((bm, d), lambda i: (i, 0)),
                  pl.BlockSpec((1, d), lambda i: (0, 0))],
        out_specs=pl.BlockSpec((bm, d), lambda i: (i, 0)),
        out_shape=jax.ShapeDtypeStruct((m, d), BF16),
        compiler_params=_params("parallel"),
        name="rmsnorm",
    )(x, g.reshape(1, d).astype(F32))


def _mm_kernel(a_ref, w_ref, o_ref):
    o_ref[...] = _dot(a_ref[...], w_ref[...]).astype(o_ref.dtype)


def _matmul(a, w, out_dtype, *, bm, bn, name):
    m, k = a.shape
    n = w.shape[1]
    return pl.pallas_call(
        _mm_kernel,
        grid=(m // bm, n // bn),
        in_specs=[pl.BlockSpec((bm, k), lambda i, j: (i, 0)),
                  pl.BlockSpec((k, bn), lambda i, j: (0, j))],
        out_specs=pl.BlockSpec((bm, bn), lambda i, j: (i, j)),
        out_shape=jax.ShapeDtypeStruct((m, n), out_dtype),
        compiler_params=_params("parallel", "arbitrary"),
        name=name,
    )(a, w)


def _round_weights_once(w_ref, wb_ref):
    @pl.when(pl.program_id(1) == 0)
    def _():
        wb_ref[...] = w_ref[...].astype(BF16)


def _qkv_kernel(a_ref, w_ref, s_ref, o_ref, wb_ref, *, qk_tiles):
    _round_weights_once(w_ref, wb_ref)
    acc = _dot(a_ref[...], wb_ref[...])
    is_qk = pl.program_id(0) < qk_tiles
    for h in range(acc.shape[1] // FOX_HEAD_DIM):
        cols = slice(h * FOX_HEAD_DIM, (h + 1) * FOX_HEAD_DIM)
        blk = acc[:, cols]
        ms = jnp.mean(blk * blk, axis=-1, keepdims=True)
        inv = jnp.where(is_qk, lax.rsqrt(ms + RMS_EPS), 1.0)
        o_ref[:, cols] = (blk * inv * s_ref[:, cols]).astype(o_ref.dtype)


def _qkv_proj(a, w_in, scale, *, bm, bn):
    m, k = a.shape
    n = scale.shape[1]
    qk_tiles = 2 * FOX_HEADS * FOX_HEAD_DIM // bn
    return pl.pallas_call(
        functools.partial(_qkv_kernel, qk_tiles=qk_tiles),
        grid=(n // bn, m // bm),
        in_specs=[pl.BlockSpec((bm, k), lambda j, i: (i, 0)),
                  pl.BlockSpec((k, bn), lambda j, i: (0, j)),
                  pl.BlockSpec((1, bn), lambda j, i: (0, j))],
        out_specs=pl.BlockSpec((bm, bn), lambda j, i: (i, j)),
        out_shape=jax.ShapeDtypeStruct((m, n), BF16),
        scratch_shapes=[pltpu.VMEM((k, bn), BF16)],
        compiler_params=_params("parallel", "arbitrary"),
        name="qkv_proj",
    )(a, w_in, scale)


def _fgate_kernel(u_ref, wt_ref, b_ref, o_ref, carry_ref):
    @pl.when(pl.program_id(1) == 0)
    def _():
        carry_ref[...] = jnp.zeros_like(carry_ref)

    z = lax.dot_general(wt_ref[...], u_ref[...], (((1,), (1,)), ((), ())),
                        preferred_element_type=F32) + b_ref[...]
    x = jnp.minimum(z, 0.0) - jnp.log1p(jnp.exp(-jnp.abs(z)))
    bs = x.shape[1]
    lane = lax.broadcasted_iota(jnp.int32, x.shape, 1)
    shift = 1
    while shift < bs:
        x = x + jnp.where(lane >= shift, pltpu.roll(x, shift, 1), 0.0)
        shift *= 2
    x = x + carry_ref[:, 0:1]
    o_ref[...] = x * LOG2_E
    carry_ref[...] = jnp.broadcast_to(x[:, bs - 1:bs], carry_ref.shape)


def _forget_cumsum(u, wt, bias, batch, seq, *, bs=512):
    h, d = wt.shape
    ns = seq // bs
    return pl.pallas_call(
        _fgate_kernel,
        grid=(batch, ns),
        in_specs=[pl.BlockSpec((bs, d), lambda b, j: (b * ns + j, 0)),
                  pl.BlockSpec((h, d), lambda b, j: (0, 0)),
                  pl.BlockSpec((h, 1), lambda b, j: (0, 0))],
        out_specs=pl.BlockSpec((None, h, bs), lambda b, j: (b, 0, j)),
        out_shape=jax.ShapeDtypeStruct((batch, h, seq), F32),
        scratch_shapes=[pltpu.VMEM((h, 128), F32)],
        compiler_params=_params("parallel", "arbitrary"),
        name="forget_cumsum",
    )(u, wt, bias)


ATTN_HEADS_PER_STEP = 4


def _attn_kernel(q_ref, k_ref, v_ref, f_ref, o_ref, *, blk):
    i = pl.program_id(2)
    dh = FOX_HEAD_DIM
    heads = range(ATTN_HEADS_PER_STEP)
    qs = [q_ref[:, h * dh:(h + 1) * dh] for h in heads]

    def block(j, carry, masked):
        start = pl.multiple_of(j * blk, blk)
        out = []
        for h in heads:
            m, l, acc = carry[h]
            ks = k_ref[pl.ds(start, blk), h * dh:(h + 1) * dh]
            vs = v_ref[pl.ds(start, blk), h * dh:(h + 1) * dh]
            s = lax.dot_general(qs[h], ks, (((1,), (1,)), ((), ())), preferred_element_type=F32)
            s = s - f_ref[h, j]
            if masked:
                row = lax.broadcasted_iota(jnp.int32, s.shape, 0)
                col = lax.broadcasted_iota(jnp.int32, s.shape, 1)
                s = jnp.where(col <= row, s, MASK_VALUE)
            m_new = jnp.maximum(m, jnp.max(s, axis=-1, keepdims=True))
            alpha = jnp.exp2(m - m_new)
            p = jnp.exp2(s - m_new)
            l = alpha * l + jnp.sum(p, axis=-1, keepdims=True)
            acc = alpha * acc + _dot(p.astype(BF16), vs)
            out.append((m_new, l, acc))
        return tuple(out)

    init = tuple((jnp.full((blk, 1), MASK_VALUE, F32), jnp.zeros((blk, 1), F32),
                  jnp.zeros((blk, dh), F32)) for _ in heads)
    carry = lax.fori_loop(0, i, lambda j, c: block(j, c, False), init)
    carry = block(i, carry, True)
    for h in heads:
        _, l, acc = carry[h]
        o_ref[:, h * dh:(h + 1) * dh] = (acc / l).astype(o_ref.dtype)


def _attn_shift_kernel(c_ref, q_ref, k_ref, v_ref, f_ref, o_ref, acc_ref, shift_ref, p_ref, *, blk):
    i = pl.program_id(2)
    dh = FOX_HEAD_DIM
    heads = range(ATTN_HEADS_PER_STEP)
    qs = [q_ref[:, h * dh:(h + 1) * dh] for h in heads]
    ones = jnp.ones((blk, dh), BF16)

    def setup():
        for h in heads:
            col = jnp.broadcast_to(f_ref[h, i], (8, blk)).T[:, 0:1] - c_ref[0]
            shift_ref[h] = jnp.broadcast_to(col, (blk, dh))
        acc_ref[...] = jnp.zeros_like(acc_ref)

    def probabilities(h, j, masked):
        start = pl.multiple_of(j * blk, blk)
        ks = k_ref[pl.ds(start, blk), h * dh:(h + 1) * dh]
        s = lax.dot_general(qs[h], ks, (((1,), (1,)), ((), ())), preferred_element_type=F32)
        s = (s + jnp.concatenate([shift_ref[h]] * (blk // dh), axis=1)) - f_ref[h, j]
        if masked:
            row = lax.broadcasted_iota(jnp.int32, s.shape, 0)
            col = lax.broadcasted_iota(jnp.int32, s.shape, 1)
            s = jnp.where(col <= row, s, MASK_VALUE)
        return jnp.exp2(s).astype(BF16)

    def values(h, j):
        start = pl.multiple_of(j * blk, blk)
        return jnp.concatenate([v_ref[pl.ds(start, blk), h * dh:(h + 1) * dh], ones], axis=1)

    setup()

    @pl.when(i == 0)
    def _():
        p_ref[...] = jnp.zeros_like(p_ref)

    @pl.when(i > 0)
    def _():
        for h in heads:
            p_ref[h] = probabilities(h, 0, False)

    @pl.loop(1, i)
    def _(j):
        for h in heads:
            acc_ref[h] += _dot(p_ref[h], values(h, j - 1))
        for h in heads:
            p_ref[h] = probabilities(h, j, False)

    prev = jnp.maximum(i - 1, 0)
    for h in heads:
        acc_ref[h] += _dot(p_ref[h], values(h, prev)) + _dot(probabilities(h, i, True), values(h, i))
    for h in heads:
        acc = acc_ref[h]
        o_ref[:, h * dh:(h + 1) * dh] = (acc[:, :dh] / acc[:, dh:]).astype(o_ref.dtype)


ATTN_SHIFT_LIMIT = 56.0


def _attention(qkv, f_rows, qk_bound, *, blk=512):
    batch, seq, _ = qkv.shape
    nblk = seq // blk
    hps = ATTN_HEADS_PER_STEP
    wide = hps * FOX_HEAD_DIM
    third = FOX_HEADS // hps
    specs = [pl.BlockSpec((None, blk, wide), lambda b, h, i: (b, i, h)),
             pl.BlockSpec((None, seq, wide), lambda b, h, i: (b, 0, h + third)),
             pl.BlockSpec((None, seq, wide), lambda b, h, i: (b, 0, h + 2 * third)),
             pl.BlockSpec((None, hps, nblk, 1, blk), lambda b, h, i: (b, h, 0, 0, 0))]
    common = dict(
        grid=(batch, FOX_HEADS // hps, nblk),
        out_specs=pl.BlockSpec((None, blk, wide), lambda b, h, i: (b, i, h)),
        out_shape=jax.ShapeDtypeStruct((batch, seq, FOX_HEADS * FOX_HEAD_DIM), BF16),
        compiler_params=_params("parallel", "parallel", "arbitrary"))

    def shifted(qkv, f_rows, c):
        return pl.pallas_call(
            functools.partial(_attn_shift_kernel, blk=blk),
            in_specs=[pl.BlockSpec(memory_space=pltpu.SMEM)] + specs,
            scratch_shapes=[pltpu.VMEM((hps, blk, 2 * FOX_HEAD_DIM), F32),
                            pltpu.VMEM((hps, blk, FOX_HEAD_DIM), F32),
                            pltpu.VMEM((hps, blk, blk), BF16)],
            name="fox_attention_shift", **common)(c.reshape(1), qkv, qkv, qkv, f_rows)

    def online(qkv, f_rows, c):
        return pl.pallas_call(functools.partial(_attn_kernel, blk=blk), in_specs=specs,
                              name="fox_attention_online", **common)(qkv, qkv, qkv, f_rows)

    return lax.cond(qk_bound <= ATTN_SHIFT_LIMIT, shifted, online, qkv, f_rows, qk_bound)


S5_SLAB_GROUPS = 16
S5_SLAB = S5_SLAB_GROUPS * S5_GROUP
S5_SLAB_STATES = S5_SLAB_GROUPS * S5_STATE
S5_SLABS = S5_GROUPS // S5_SLAB_GROUPS


def _dot_split(a, b):
    a_hi = a.astype(BF16)
    b_hi = b.astype(BF16)
    a_lo = (a - a_hi.astype(F32)).astype(BF16)
    b_lo = (b - b_hi.astype(F32)).astype(BF16)
    return _dot(a_hi, b_hi) + _dot(a_hi, b_lo) + _dot(a_lo, b_hi)


def _lam_pow(k, lr, li, dt):
    mag = jnp.exp(k * (lr * dt))
    ang = k * (li * dt)
    return mag * jnp.cos(ang), mag * jnp.sin(ang)


def _s5_matrices_kernel(lam_ref, ls_ref, bdb_ref, bdc_ref, bp_ref, cp_ref, gr_ref, l16_ref):
    t = pl.program_id(1)
    ns = S5_SLAB_STATES
    lr, li = lam_ref[0:1, :], lam_ref[1:2, :]
    dt = jnp.exp(ls_ref[...])
    lbr, lbi = _lam_pow(1.0, lr, li, dt)
    den = lr * lr + li * li
    nr = lbr - 1.0
    fr = (nr * lr + lbi * li) / den
    fi = (lbi * lr - nr * li) / den
    b_re, b_im = bdb_ref[0], bdb_ref[1]
    bb_re = fr * b_re - fi * b_im
    bb_im = fr * b_im + fi * b_re
    pr, pi = _lam_pow((S5_CHUNK - 1 - t).astype(F32), lr, li, dt)
    bp_re = bb_re * pr - bb_im * pi
    bp_im = bb_re * pi + bb_im * pr
    bp_ref[:, :ns] = bp_re.astype(BF16)
    bp_ref[:, ns:] = bp_im.astype(BF16)

    qr, qi = _lam_pow((t + 1).astype(F32), lr, li, dt)
    qr = jnp.broadcast_to(qr, (8, ns)).T[:, 0:1]
    qi = jnp.broadcast_to(qi, (8, ns)).T[:, 0:1]
    c_re, c_im = bdc_ref[0], bdc_ref[1]
    cp_ref[:ns, :] = (c_re * qr - c_im * qi).astype(BF16)
    cp_ref[ns:, :] = (-(c_re * qi + c_im * qr)).astype(BF16)

    gr_ref[...] = (_dot_split(bp_re, c_re) - _dot_split(bp_im, c_im)).astype(BF16)

    l16r, l16i = _lam_pow(float(S5_CHUNK), lr, li, dt)
    l16_ref[0:1, :] = l16r
    l16_ref[1:2, :] = l16i


def _s5_matrices(lam, ls, bdb, bdc):
    t_len, w, ns = S5_CHUNK, S5_SLAB, S5_SLAB_STATES
    return pl.pallas_call(
        _s5_matrices_kernel,
        grid=(S5_SLABS, t_len),
        in_specs=[pl.BlockSpec((None, 2, ns), lambda s, t: (s, 0, 0)),
                  pl.BlockSpec((None, 1, ns), lambda s, t: (s, 0, 0)),
                  pl.BlockSpec((None, 2, w, ns), lambda s, t: (s, 0, 0, 0)),
                  pl.BlockSpec((None, 2, ns, w), lambda s, t: (s, 0, 0, 0))],
        out_specs=[pl.BlockSpec((None, w, 2 * ns), lambda s, t: (s, t, 0)),
                   pl.BlockSpec((None, None, 2 * ns, w), lambda s, t: (s, t, 0, 0)),
                   pl.BlockSpec((None, w, w), lambda s, t: (s, t, 0)),
                   pl.BlockSpec((None, 2, ns), lambda s, t: (s, 0, 0))],
        out_shape=[jax.ShapeDtypeStruct((S5_SLABS, t_len * w, 2 * ns), BF16),
                   jax.ShapeDtypeStruct((S5_SLABS, t_len, 2 * ns, w), BF16),
                   jax.ShapeDtypeStruct((S5_SLABS, t_len * w, w), BF16),
                   jax.ShapeDtypeStruct((S5_SLABS, 2, ns), F32)],
        compiler_params=_params("parallel", "arbitrary"),
        name="s5_matrices",
    )(lam, ls, bdb, bdc)


def _chunk_rows(x_ref):
    return jnp.concatenate([x_ref[:, t, :] for t in range(S5_CHUNK)], axis=1).astype(BF16)


def _s5_states_kernel(x_ref, bp_ref, s_ref):
    s_ref[...] = _dot(_chunk_rows(x_ref), bp_ref[...])


def _s5_states(x3, bp, *, rows):
    nc = x3.shape[0]
    t_len, w, ns = S5_CHUNK, S5_SLAB, S5_SLAB_STATES
    return pl.pallas_call(
        _s5_states_kernel,
        grid=(S5_SLABS, nc // rows),
        in_specs=[pl.BlockSpec((rows, t_len, w), lambda s, r: (r, 0, s)),
                  pl.BlockSpec((None, t_len * w, 2 * ns), lambda s, r: (s, 0, 0),
                               pipeline_mode=pl.Buffered(1))],
        out_specs=pl.BlockSpec((rows, 2 * ns), lambda s, r: (r, s)),
        out_shape=jax.ShapeDtypeStruct((nc, S5_SLABS * 2 * ns), F32),
        compiler_params=_params("parallel", "arbitrary"),
        name="s5_chunk_states",
    )(x3, bp)


def _s5_chunk_scan_kernel(s_ref, l16_ref, xp_ref, *, batch, chunks):
    ns = S5_SLAB_STATES
    ar, ai = l16_ref[0:1, :], l16_ref[1:2, :]

    def step(c, carry):
        out = []
        for b, (xr, xi) in enumerate(carry):
            row = pl.ds(b * chunks + c, 1)
            xp_ref[row, :ns] = xr
            xp_ref[row, ns:] = xi
            sr = s_ref[row, :ns]
            si = s_ref[row, ns:]
            out.append((ar * xr - ai * xi + sr, ar * xi + ai * xr + si))
        return tuple(out)

    zero = jnp.zeros((1, ns), F32)
    lax.fori_loop(0, chunks, step, ((zero, zero),) * batch)


def _s5_chunk_scan(s, l16, *, batch):
    nc = s.shape[0]
    ns = S5_SLAB_STATES
    return pl.pallas_call(
        functools.partial(_s5_chunk_scan_kernel, batch=batch, chunks=nc // batch),
        grid=(S5_SLABS,),
        in_specs=[pl.BlockSpec((nc, 2 * ns), lambda s_: (0, s_)),
                  pl.BlockSpec((None, 2, ns), lambda s_: (s_, 0, 0))],
        out_specs=pl.BlockSpec((nc, 2 * ns), lambda s_: (0, s_)),
        out_shape=jax.ShapeDtypeStruct(s.shape, F32),
        compiler_params=_params("parallel"),
        name="s5_chunk_scan",
    )(s, l16)


def _s5_output_kernel(x_ref, xp_ref, gr_ref, cp_ref, d_ref, y_ref):
    t_len, w = S5_CHUNK, S5_SLAB
    xc = _chunk_rows(x_ref)
    xp = xp_ref[...].astype(BF16)
    d = d_ref[...]
    for t in range(t_len):
        y = _dot(xc[:, :(t + 1) * w], gr_ref[(t_len - 1 - t) * w:, :])
        y = y + _dot(xp, cp_ref[t])
        y_ref[:, t, :] = y + d * x_ref[:, t, :]


def _s5_output(x3, xp, gr, cp, d, *, rows):
    nc = x3.shape[0]
    t_len, w, ns = S5_CHUNK, S5_SLAB, S5_SLAB_STATES
    once = dict(pipeline_mode=pl.Buffered(1))
    return pl.pallas_call(
        _s5_output_kernel,
        grid=(S5_SLABS, nc // rows),
        in_specs=[pl.BlockSpec((rows, t_len, w), lambda s, r: (r, 0, s)),
                  pl.BlockSpec((rows, 2 * ns), lambda s, r: (r, s)),
                  pl.BlockSpec((None, t_len * w, w), lambda s, r: (s, 0, 0), **once),
                  pl.BlockSpec((None, t_len, 2 * ns, w), lambda s, r: (s, 0, 0, 0), **once),
                  pl.BlockSpec((None, 1, w), lambda s, r: (s, 0, 0))],
        out_specs=pl.BlockSpec((rows, t_len, w), lambda s, r: (r, 0, s)),
        out_shape=jax.ShapeDtypeStruct(x3.shape, F32),
        compiler_params=_params("parallel", "arbitrary"),
        name="s5_chunk_output",
    )(x3, xp, gr, cp, d)


def _s5(s5_in, lam_re, lam_im, log_step, b_re, b_im, c_re, c_im, d, *, batch):
    m, width = s5_in.shape
    sl, sg, p, grp = S5_SLABS, S5_SLAB_GROUPS, S5_STATE, S5_GROUP
    ns = S5_SLAB_STATES
    lam = jnp.stack([lam_re, lam_im], axis=0).astype(F32).reshape(2, sl, ns).transpose(1, 0, 2)
    ls = jnp.repeat(log_step.astype(F32), p).reshape(sl, 1, ns)
    eye = jnp.eye(sg, dtype=F32)
    b = jnp.stack([b_re, b_im], axis=0).astype(F32).reshape(2, sl, sg, p, grp)
    bdb = jnp.einsum('rsgpi,gh->srgihp', b, eye).reshape(sl, 2, sg * grp, ns)
    c = jnp.stack([c_re, c_im], axis=0).astype(F32).reshape(2, sl, sg, grp, p)
    bdc = jnp.einsum('rsgop,gh->srhpgo', c, eye).reshape(sl, 2, ns, sg * grp)
    bp, cp, gr, l16 = _s5_matrices(lam, ls, bdb, bdc)
    x3 = s5_in.reshape(m // S5_CHUNK, S5_CHUNK, width)
    states = _s5_states(x3, bp, rows=512)
    xprev = _s5_chunk_scan(states, l16, batch=batch)
    y3 = _s5_output(x3, xprev, gr, cp, d.astype(F32).reshape(sl, 1, sg * grp), rows=256)
    return y3.reshape(m, width)


def _glu_kernel(y_ref, w_ref, b_ref, o_ref):
    g = jax.nn.gelu(y_ref[...])
    o_ref[...] = (g * jax.nn.sigmoid(_dot(g.astype(BF16), w_ref[...]) + b_ref[...])).astype(o_ref.dtype)


def _glu(y, w, b, *, bm=512):
    m, n = y.shape
    return pl.pallas_call(
        _glu_kernel,
        grid=(m // bm,),
        in_specs=[pl.BlockSpec((bm, n), lambda i: (i, 0)),
                  pl.BlockSpec((n, n), lambda i: (0, 0)),
                  pl.BlockSpec((1, n), lambda i: (0, 0))],
        out_specs=pl.BlockSpec((bm, n), lambda i: (i, 0)),
        out_shape=jax.ShapeDtypeStruct((m, n), BF16),
        compiler_params=_params("parallel"),
        name="s5_glu",
    )(y, w, b)


def _merge_kernel(u_ref, a_ref, s_ref, wgf_ref, wgs_ref, wpf_ref, wps_ref, bgf_ref, bgs_ref, o_ref):
    u = u_ref[...]
    gate_fox = jax.nn.sigmoid(_dot(u, wgf_ref[...]) + bgf_ref[...])
    gate_s5 = jax.nn.sigmoid(_dot(u, wgs_ref[...]) + bgs_ref[...])
    o_ref[...] = (gate_fox * _dot(a_ref[...], wpf_ref[...])
                  + gate_s5 * _dot(s_ref[...], wps_ref[...])).astype(o_ref.dtype)


def _merge(u, attn, ssm, wgf, wgs, wpf, wps, bgf, bgs, *, bm=512, bn=512):
    m, d = u.shape
    n = wgf.shape[1]
    act = lambda a: pl.BlockSpec((bm, a.shape[1]), lambda i, j: (i, 0))
    wgt = lambda w: pl.BlockSpec((w.shape[0], bn), lambda i, j: (0, j))
    return pl.pallas_call(
        _merge_kernel,
        grid=(m // bm, n // bn),
        in_specs=[act(u), act(attn), act(ssm), wgt(wgf), wgt(wgs), wgt(wpf), wgt(wps),
                  pl.BlockSpec((1, bn), lambda i, j: (0, j)),
                  pl.BlockSpec((1, bn), lambda i, j: (0, j))],
        out_specs=pl.BlockSpec((bm, bn), lambda i, j: (i, j)),
        out_shape=jax.ShapeDtypeStruct((m, n), BF16),
        compiler_params=_params("parallel", "arbitrary"),
        name="gated_merge",
    )(u, attn, ssm, wgf, wgs, wpf, wps, bgf, bgs)


def _mm_residual_kernel(a_ref, w_ref, r_ref, o_ref, wb_ref):
    _round_weights_once(w_ref, wb_ref)
    o_ref[...] = r_ref[...] + _dot(a_ref[...], wb_ref[...])


def _matmul_residual(a, w, res, *, bm, bn):
    m, k = a.shape
    n = w.shape[1]
    return pl.pallas_call(
        _mm_residual_kernel,
        grid=(n // bn, m // bm),
        in_specs=[pl.BlockSpec((bm, k), lambda j, i: (i, 0)),
                  pl.BlockSpec((k, bn), lambda j, i: (0, j)),
                  pl.BlockSpec((bm, bn), lambda j, i: (i, j))],
        out_specs=pl.BlockSpec((bm, bn), lambda j, i: (i, j)),
        out_shape=jax.ShapeDtypeStruct((m, n), F32),
        scratch_shapes=[pltpu.VMEM((k, bn), BF16)],
        compiler_params=_params("parallel", "arbitrary"),
        name="out_proj_residual",
    )(a, w, res)


def _swiglu_kernel(a_ref, wg_ref, wu_ref, o_ref, wgb_ref, wub_ref):
    _round_weights_once(wg_ref, wgb_ref)
    _round_weights_once(wu_ref, wub_ref)
    a = a_ref[...]
    gate = _dot(a, wgb_ref[...])
    o_ref[...] = (jax.nn.silu(gate) * _dot(a, wub_ref[...])).astype(o_ref.dtype)


def _swiglu_up(a, w_gate_up, d_ff, *, bm, bn):
    m, k = a.shape
    nj = d_ff // bn
    return pl.pallas_call(
        _swiglu_kernel,
        grid=(nj, m // bm),
        in_specs=[pl.BlockSpec((bm, k), lambda j, i: (i, 0)),
                  pl.BlockSpec((k, bn), lambda j, i: (0, j)),
                  pl.BlockSpec((k, bn), lambda j, i: (0, j + nj))],
        out_specs=pl.BlockSpec((bm, bn), lambda j, i: (i, j)),
        out_shape=jax.ShapeDtypeStruct((m, d_ff), BF16),
        scratch_shapes=[pltpu.VMEM((k, bn), BF16)] * 2,
        compiler_params=_params("parallel", "arbitrary"),
        name="swiglu_up",
    )(a, w_gate_up, w_gate_up)


def _down_kernel(a_ref, w_ref, r_ref, o_ref, acc_ref):
    kk = pl.program_id(2)

    @pl.when(kk == 0)
    def _():
        acc_ref[...] = r_ref[...]

    acc_ref[...] += _dot(a_ref[...], w_ref[...])

    @pl.when(kk == pl.num_programs(2) - 1)
    def _():
        o_ref[...] = acc_ref[...]


def _down_residual(a, w, res, *, bm, bn, bk):
    m, k = a.shape
    n = w.shape[1]
    return pl.pallas_call(
        _down_kernel,
        grid=(m // bm, n // bn, k // bk),
        in_specs=[pl.BlockSpec((bm, bk), lambda i, j, kk: (i, kk)),
                  pl.BlockSpec((bk, bn), lambda i, j, kk: (kk, j)),
                  pl.BlockSpec((bm, bn), lambda i, j, kk: (i, j))],
        out_specs=pl.BlockSpec((bm, bn), lambda i, j, kk: (i, j)),
        out_shape=jax.ShapeDtypeStruct((m, n), F32),
        scratch_shapes=[pltpu.VMEM((bm, bn), F32)],
        compiler_params=_params("parallel", "arbitrary", "arbitrary"),
        name="down_proj_residual",
    )(a, w, res)


def _layer(x, batch, seq, g_mix, w_in, b_fgate, b_gates, q_norm, k_norm,
           lam_re, lam_im, log_step, b_re, b_im, c_re, c_im, s5_d,
           w_glu, b_glu, w_proj_fox, w_proj_s5, w_out, g_ffn, w_gate_up, w_down):
    m, d_model = x.shape
    fox_w = FOX_HEADS * FOX_HEAD_DIM
    s5_w = S5_GROUP * S5_GROUPS
    col_k, col_v = fox_w, 2 * fox_w
    col_f = 3 * fox_w
    col_s5 = col_f + FOX_HEADS
    col_g = col_s5 + s5_w
    d_ff = w_down.shape[0]
    narrow = lambda lo, hi: w_in[:, lo:hi].astype(BF16)

    u = _rmsnorm(x, g_mix)

    head_scale = jnp.concatenate([
        jnp.tile(q_norm.astype(F32), FOX_HEADS) * (LOG2_E / math.sqrt(FOX_HEAD_DIM)),
        jnp.tile(k_norm.astype(F32), FOX_HEADS),
        jnp.ones((fox_w,), F32)]).reshape(1, 3 * fox_w)
    qkv = _qkv_proj(u, w_in, head_scale, bm=1024, bn=512)
    s5_in = _matmul(u, narrow(col_s5, col_g), F32, bm=1024, bn=1024, name="s5_in_proj")

    blk = 512
    f_t = _forget_cumsum(u, narrow(col_f, col_s5).T, b_fgate.astype(F32).reshape(FOX_HEADS, 1),
                         batch, seq)
    f_rows = f_t.reshape(batch, FOX_HEADS, seq // blk, 1, blk)
    qk_bound = (1.02 * LOG2_E * math.sqrt(FOX_HEAD_DIM)
                * jnp.max(jnp.abs(q_norm.astype(F32))) * jnp.max(jnp.abs(k_norm.astype(F32))))
    attn = _attention(qkv.reshape(batch, seq, 3 * fox_w), f_rows, qk_bound, blk=blk).reshape(m, fox_w)

    y5 = _s5(s5_in, lam_re, lam_im, log_step, b_re, b_im, c_re, c_im, s5_d, batch=batch)
    ssm = _glu(y5, w_glu.astype(BF16), b_glu.astype(F32).reshape(1, s5_w))

    b_gates = b_gates.astype(F32).reshape(1, 2 * d_model)
    merged = _merge(u, attn, ssm, narrow(col_g, col_g + d_model), narrow(col_g + d_model, col_g + 2 * d_model),
                    w_proj_fox.astype(BF16), w_proj_s5.astype(BF16),
                    b_gates[:, :d_model], b_gates[:, d_model:])
    h = _matmul_residual(merged, w_out, x, bm=1024, bn=512)

    hn = _rmsnorm(h, g_ffn)
    act = _swiglu_up(hn, w_gate_up, d_ff, bm=1024, bn=256)
    return _down_residual(act, w_down.astype(BF16), h, bm=1024, bn=512, bk=d_ff // 2)


def kernel(x, g_mix, w_in, b_fgate, b_gates, q_norm, k_norm, s5_lambda_re, s5_lambda_im, s5_log_step, s5_b_re, s5_b_im, s5_c_re, s5_c_im, s5_d, w_glu, b_glu, w_proj_fox, w_proj_s5, w_out, g_ffn, w_gate_up, w_down):
    batch, seq, d_model = x.shape
    h = x.reshape(batch * seq, d_model)
    for l in range(g_mix.shape[0]):
        h = _layer(h, batch, seq, g_mix[l], w_in[l], b_fgate[l], b_gates[l], q_norm[l], k_norm[l],
                   s5_lambda_re[l], s5_lambda_im[l], s5_log_step[l], s5_b_re[l], s5_b_im[l],
                   s5_c_re[l], s5_c_im[l], s5_d[l], w_glu[l], b_glu[l],
                   w_proj_fox[l], w_proj_s5[l], w_out[l], g_ffn[l], w_gate_up[l], w_down[l])
    return h.reshape(batch, seq, d_model)
```

```python
import functools
import math

import jax
import jax.numpy as jnp
from jax import lax
from jax.experimental import pallas as pl
from jax.experimental.pallas import tpu as pltpu

F32 = jnp.float32
BF16 = jnp.bfloat16

FOX_HEADS = 16
FOX_HEAD_DIM = 128
S5_GROUP = 16
S5_GROUPS = 64
S5_STATE = 64
S5_CHUNK = 16
RMS_EPS = 1e-6
MASK_VALUE = -1e30
LOG2_E = math.log2(math.e)

VMEM_LIMIT_BYTES = 56 * 1024 * 1024


def _params(*semantics):
    return pltpu.CompilerParams(dimension_semantics=semantics,
                                vmem_limit_bytes=VMEM_LIMIT_BYTES)


def _dot(a, b):
    return jnp.dot(a, b, preferred_element_type=F32)


def _rmsnorm_kernel(x_ref, g_ref, o_ref):
    x = x_ref[...]
    ms = jnp.mean(x * x, axis=-1, keepdims=True)
    o_ref[...] = (x * lax.rsqrt(ms + RMS_EPS) * g_ref[...]).astype(o_ref.dtype)


def _rmsnorm(x, g, *, bm=256):
    m, d = x.shape
    return pl.pallas_call(
        _rmsnorm_kernel,
        grid=(m // bm,),
        in_specs=[pl.BlockSpec((bm, d), lambda i: (i, 0)),
                  pl.BlockSpec((1, d), lambda i: (0, 0))],
        out_specs=pl.BlockSpec((bm, d), lambda i: (i, 0)),
        out_shape=jax.ShapeDtypeStruct((m, d), BF16),
        compiler_params=_params("parallel"),
        name="rmsnorm",
    )(x, g.reshape(1, d).astype(F32))


def _mm_kernel(a_ref, w_ref, o_ref):
    o_ref[...] = _dot(a_ref[...], w_ref[...]).astype(o_ref.dtype)


def _matmul(a, w, out_dtype, *, bm, bn, name):
    m, k = a.shape
    n = w.shape[1]
    return pl.pallas_call(
        _mm_kernel,
        grid=(m // bm, n // bn),
        in_specs=[pl.BlockSpec((bm, k), lambda i, j: (i, 0)),
                  pl.BlockSpec((k, bn), lambda i, j: (0, j))],
        out_specs=pl.BlockSpec((bm, bn), lambda i, j: (i, j)),
        out_shape=jax.ShapeDtypeStruct((m, n), out_dtype),
        compiler_params=_params("parallel", "arbitrary"),
        name=name,
    )(a, w)


def _qkv_kernel(a_ref, w_ref, s_ref, o_ref, *, qk_tiles):
    acc = _dot(a_ref[...], w_ref[...])
    is_qk = pl.program_id(1) < qk_tiles
    for h in range(acc.shape[1] // FOX_HEAD_DIM):
        cols = slice(h * FOX_HEAD_DIM, (h + 1) * FOX_HEAD_DIM)
        blk = acc[:, cols]
        ms = jnp.mean(blk * blk, axis=-1, keepdims=True)
        inv = jnp.where(is_qk, lax.rsqrt(ms + RMS_EPS), 1.0)
        o_ref[:, cols] = (blk * inv * s_ref[:, cols]).astype(o_ref.dtype)


def _qkv_proj(a, w_in, scale, *, bm, bn):
    m, k = a.shape
    n = scale.shape[1]
    qk_tiles = 2 * FOX_HEADS * FOX_HEAD_DIM // bn
    return pl.pallas_call(
        functools.partial(_qkv_kernel, qk_tiles=qk_tiles),
        grid=(m // bm, n // bn),
        in_specs=[pl.BlockSpec((bm, k), lambda i, j: (i, 0)),
                  pl.BlockSpec((k, bn), lambda i, j: (0, j)),
                  pl.BlockSpec((1, bn), lambda i, j: (0, j))],
        out_specs=pl.BlockSpec((bm, bn), lambda i, j: (i, j)),
        out_shape=jax.ShapeDtypeStruct((m, n), BF16),
        compiler_params=_params("parallel", "arbitrary"),
        name="qkv_proj",
    )(a, w_in, scale)


def _fgate_kernel(u_ref, wt_ref, b_ref, o_ref, carry_ref):
    @pl.when(pl.program_id(1) == 0)
    def _():
        carry_ref[...] = jnp.zeros_like(carry_ref)

    z = lax.dot_general(wt_ref[...], u_ref[...], (((1,), (1,)), ((), ())),
                        preferred_element_type=F32) + b_ref[...]
    x = jnp.minimum(z, 0.0) - jnp.log1p(jnp.exp(-jnp.abs(z)))
    bs = x.shape[1]
    lane = lax.broadcasted_iota(jnp.int32, x.shape, 1)
    shift = 1
    while shift < bs:
        x = x + jnp.where(lane >= shift, pltpu.roll(x, shift, 1), 0.0)
        shift *= 2
    x = x + carry_ref[:, 0:1]
    o_ref[...] = x * LOG2_E
    carry_ref[...] = jnp.broadcast_to(x[:, bs - 1:bs], carry_ref.shape)


def _forget_cumsum(u, wt, bias, batch, seq, *, bs=512):
    h, d = wt.shape
    ns = seq // bs
    return pl.pallas_call(
        _fgate_kernel,
        grid=(batch, ns),
        in_specs=[pl.BlockSpec((bs, d), lambda b, j: (b * ns + j, 0)),
                  pl.BlockSpec((h, d), lambda b, j: (0, 0)),
                  pl.BlockSpec((h, 1), lambda b, j: (0, 0))],
        out_specs=pl.BlockSpec((None, h, bs), lambda b, j: (b, 0, j)),
        out_shape=jax.ShapeDtypeStruct((batch, h, seq), F32),
        scratch_shapes=[pltpu.VMEM((h, 128), F32)],
        compiler_params=_params("parallel", "arbitrary"),
        name="forget_cumsum",
    )(u, wt, bias)


ATTN_HEADS_PER_STEP = 4


def _attn_kernel(q_ref, k_ref, v_ref, f_ref, o_ref, *, blk):
    i = pl.program_id(2)
    dh = FOX_HEAD_DIM
    heads = range(ATTN_HEADS_PER_STEP)
    qs = [q_ref[:, h * dh:(h + 1) * dh] for h in heads]

    def block(j, carry, masked):
        start = pl.multiple_of(j * blk, blk)
        out = []
        for h in heads:
            m, l, acc = carry[h]
            ks = k_ref[pl.ds(start, blk), h * dh:(h + 1) * dh]
            vs = v_ref[pl.ds(start, blk), h * dh:(h + 1) * dh]
            s = lax.dot_general(qs[h], ks, (((1,), (1,)), ((), ())), preferred_element_type=F32)
            s = s - f_ref[h, j]
            if masked:
                row = lax.broadcasted_iota(jnp.int32, s.shape, 0)
                col = lax.broadcasted_iota(jnp.int32, s.shape, 1)
                s = jnp.where(col <= row, s, MASK_VALUE)
            m_new = jnp.maximum(m, jnp.max(s, axis=-1, keepdims=True))
            alpha = jnp.exp2(m - m_new)
            p = jnp.exp2(s - m_new)
            l = alpha * l + jnp.sum(p, axis=-1, keepdims=True)
            acc = alpha * acc + _dot(p.astype(BF16), vs)
            out.append((m_new, l, acc))
        return tuple(out)

    init = tuple((jnp.full((blk, 1), MASK_VALUE, F32), jnp.zeros((blk, 1), F32),
                  jnp.zeros((blk, dh), F32)) for _ in heads)
    carry = lax.fori_loop(0, i, lambda j, c: block(j, c, False), init)
    carry = block(i, carry, True)
    for h in heads:
        _, l, acc = carry[h]
        o_ref[:, h * dh:(h + 1) * dh] = (acc / l).astype(o_ref.dtype)


def _attn_shift_kernel(c_ref, q_ref, k_ref, v_ref, f_ref, o_ref, acc_ref, shift_ref, p_ref, *, blk):
    i = pl.program_id(2)
    dh = FOX_HEAD_DIM
    heads = range(ATTN_HEADS_PER_STEP)
    qs = [q_ref[:, h * dh:(h + 1) * dh] for h in heads]
    ones = jnp.ones((blk, dh), BF16)

    def setup():
        for h in heads:
            col = jnp.broadcast_to(f_ref[h, i], (8, blk)).T[:, 0:1] - c_ref[0]
            shift_ref[h] = jnp.broadcast_to(col, (blk, dh))
        acc_ref[...] = jnp.zeros_like(acc_ref)

    def probabilities(h, j, masked):
        start = pl.multiple_of(j * blk, blk)
        ks = k_ref[pl.ds(start, blk), h * dh:(h + 1) * dh]
        s = lax.dot_general(qs[h], ks, (((1,), (1,)), ((), ())), preferred_element_type=F32)
        s = (s + jnp.concatenate([shift_ref[h]] * (blk // dh), axis=1)) - f_ref[h, j]
        if masked:
            row = lax.broadcasted_iota(jnp.int32, s.shape, 0)
            col = lax.broadcasted_iota(jnp.int32, s.shape, 1)
            s = jnp.where(col <= row, s, MASK_VALUE)
        return jnp.exp2(s).astype(BF16)

    def values(h, j):
        start = pl.multiple_of(j * blk, blk)
        return jnp.concatenate([v_ref[pl.ds(start, blk), h * dh:(h + 1) * dh], ones], axis=1)

    setup()

    @pl.when(i == 0)
    def _():
        p_ref[...] = jnp.zeros_like(p_ref)

    @pl.when(i > 0)
    def _():
        for h in heads:
            p_ref[h] = probabilities(h, 0, False)

    @pl.loop(1, i)
    def _(j):
        for h in heads:
            acc_ref[h] += _dot(p_ref[h], values(h, j - 1))
        for h in heads:
            p_ref[h] = probabilities(h, j, False)

    prev = jnp.maximum(i - 1, 0)
    for h in heads:
        acc_ref[h] += _dot(p_ref[h], values(h, prev)) + _dot(probabilities(h, i, True), values(h, i))
    for h in heads:
        acc = acc_ref[h]
        o_ref[:, h * dh:(h + 1) * dh] = (acc[:, :dh] / acc[:, dh:]).astype(o_ref.dtype)


ATTN_SHIFT_LIMIT = 56.0


def _attention(qkv, f_rows, qk_bound, *, blk=512):
    batch, seq, _ = qkv.shape
    nblk = seq // blk
    hps = ATTN_HEADS_PER_STEP
    wide = hps * FOX_HEAD_DIM
    third = FOX_HEADS // hps
    specs = [pl.BlockSpec((None, blk, wide), lambda b, h, i: (b, i, h)),
             pl.BlockSpec((None, seq, wide), lambda b, h, i: (b, 0, h + third)),
             pl.BlockSpec((None, seq, wide), lambda b, h, i: (b, 0, h + 2 * third)),
             pl.BlockSpec((None, hps, nblk, 1, blk), lambda b, h, i: (b, h, 0, 0, 0))]
    common = dict(
        grid=(batch, FOX_HEADS // hps, nblk),
        out_specs=pl.BlockSpec((None, blk, wide), lambda b, h, i: (b, i, h)),
        out_shape=jax.ShapeDtypeStruct((batch, seq, FOX_HEADS * FOX_HEAD_DIM), BF16),
        compiler_params=_params("parallel", "parallel", "arbitrary"))

    def shifted(qkv, f_rows, c):
        return pl.pallas_call(
            functools.partial(_attn_shift_kernel, blk=blk),
            in_specs=[pl.BlockSpec(memory_space=pltpu.SMEM)] + specs,
            scratch_shapes=[pltpu.VMEM((hps, blk, 2 * FOX_HEAD_DIM), F32),
                            pltpu.VMEM((hps, blk, FOX_HEAD_DIM), F32),
                            pltpu.VMEM((hps, blk, blk), BF16)],
            name="fox_attention_shift", **common)(c.reshape(1), qkv, qkv, qkv, f_rows)

    def online(qkv, f_rows, c):
        return pl.pallas_call(functools.partial(_attn_kernel, blk=blk), in_specs=specs,
                              name="fox_attention_online", **common)(qkv, qkv, qkv, f_rows)

    return lax.cond(qk_bound <= ATTN_SHIFT_LIMIT, shifted, online, qkv, f_rows, qk_bound)


S5_SLAB_GROUPS = 16
S5_SLAB = S5_SLAB_GROUPS * S5_GROUP
S5_SLAB_STATES = S5_SLAB_GROUPS * S5_STATE
S5_SLABS = S5_GROUPS // S5_SLAB_GROUPS


def _dot_split(a, b):
    a_hi = a.astype(BF16)
    b_hi = b.astype(BF16)
    a_lo = (a - a_hi.astype(F32)).astype(BF16)
    b_lo = (b - b_hi.astype(F32)).astype(BF16)
    return _dot(a_hi, b_hi) + _dot(a_hi, b_lo) + _dot(a_lo, b_hi)


def _lam_pow(k, lr, li, dt):
    mag = jnp.exp(k * (lr * dt))
    ang = k * (li * dt)
    return mag * jnp.cos(ang), mag * jnp.sin(ang)


def _s5_matrices_kernel(lam_ref, ls_ref, bdb_ref, bdc_ref, bp_ref, cp_ref, gr_ref, l16_ref):
    t = pl.program_id(1)
    ns = S5_SLAB_STATES
    lr, li = lam_ref[0:1, :], lam_ref[1:2, :]
    dt = jnp.exp(ls_ref[...])
    lbr, lbi = _lam_pow(1.0, lr, li, dt)
    den = lr * lr + li * li
    nr = lbr - 1.0
    fr = (nr * lr + lbi * li) / den
    fi = (lbi * lr - nr * li) / den
    b_re, b_im = bdb_ref[0], bdb_ref[1]
    bb_re = fr * b_re - fi * b_im
    bb_im = fr * b_im + fi * b_re
    pr, pi = _lam_pow((S5_CHUNK - 1 - t).astype(F32), lr, li, dt)
    bp_re = bb_re * pr - bb_im * pi
    bp_im = bb_re * pi + bb_im * pr
    bp_ref[:, :ns] = bp_re.astype(BF16)
    bp_ref[:, ns:] = bp_im.astype(BF16)

    qr, qi = _lam_pow((t + 1).astype(F32), lr, li, dt)
    qr = jnp.broadcast_to(qr, (8, ns)).T[:, 0:1]
    qi = jnp.broadcast_to(qi, (8, ns)).T[:, 0:1]
    c_re, c_im = bdc_ref[0], bdc_ref[1]
    cp_ref[:ns, :] = (c_re * qr - c_im * qi).astype(BF16)
    cp_ref[ns:, :] = (-(c_re * qi + c_im * qr)).astype(BF16)

    gr_ref[...] = (_dot_split(bp_re, c_re) - _dot_split(bp_im, c_im)).astype(BF16)

    l16r, l16i = _lam_pow(float(S5_CHUNK), lr, li, dt)
    l16_ref[0:1, :] = l16r
    l16_ref[1:2, :] = l16i


def _s5_matrices(lam, ls, bdb, bdc):
    t_len, w, ns = S5_CHUNK, S5_SLAB, S5_SLAB_STATES
    return pl.pallas_call(
        _s5_matrices_kernel,
        grid=(S5_SLABS, t_len),
        in_specs=[pl.BlockSpec((None, 2, ns), lambda s, t: (s, 0, 0)),
                  pl.BlockSpec((None, 1, ns), lambda s, t: (s, 0, 0)),
                  pl.BlockSpec((None, 2, w, ns), lambda s, t: (s, 0, 0, 0)),
                  pl.BlockSpec((None, 2, ns, w), lambda s, t: (s, 0, 0, 0))],
        out_specs=[pl.BlockSpec((None, w, 2 * ns), lambda s, t: (s, t, 0)),
                   pl.BlockSpec((None, None, 2 * ns, w), lambda s, t: (s, t, 0, 0)),
                   pl.BlockSpec((None, w, w), lambda s, t: (s, t, 0)),
                   pl.BlockSpec((None, 2, ns), lambda s, t: (s, 0, 0))],
        out_shape=[jax.ShapeDtypeStruct((S5_SLABS, t_len * w, 2 * ns), BF16),
                   jax.ShapeDtypeStruct((S5_SLABS, t_len, 2 * ns, w), BF16),
                   jax.ShapeDtypeStruct((S5_SLABS, t_len * w, w), BF16),
                   jax.ShapeDtypeStruct((S5_SLABS, 2, ns), F32)],
        compiler_params=_params("parallel", "arbitrary"),
        name="s5_matrices",
    )(lam, ls, bdb, bdc)


def _chunk_rows(x_ref):
    return jnp.concatenate([x_ref[:, t, :] for t in range(S5_CHUNK)], axis=1).astype(BF16)


def _s5_states_kernel(x_ref, bp_ref, s_ref):
    s_ref[...] = _dot(_chunk_rows(x_ref), bp_ref[...])


def _s5_states(x3, bp, *, rows):
    nc = x3.shape[0]
    t_len, w, ns = S5_CHUNK, S5_SLAB, S5_SLAB_STATES
    return pl.pallas_call(
        _s5_states_kernel,
        grid=(S5_SLABS, nc // rows),
        in_specs=[pl.BlockSpec((rows, t_len, w), lambda s, r: (r, 0, s)),
                  pl.BlockSpec((None, t_len * w, 2 * ns), lambda s, r: (s, 0, 0),
                               pipeline_mode=pl.Buffered(1))],
        out_specs=pl.BlockSpec((rows, 2 * ns), lambda s, r: (r, s)),
        out_shape=jax.ShapeDtypeStruct((nc, S5_SLABS * 2 * ns), F32),
        compiler_params=_params("parallel", "arbitrary"),
        name="s5_chunk_states",
    )(x3, bp)


def _s5_chunk_scan_kernel(s_ref, l16_ref, xp_ref, *, batch, chunks):
    ns = S5_SLAB_STATES
    ar, ai = l16_ref[0:1, :], l16_ref[1:2, :]

    def step(c, carry):
        out = []
        for b, (xr, xi) in enumerate(carry):
            row = pl.ds(b * chunks + c, 1)
            xp_ref[row, :ns] = xr
            xp_ref[row, ns:] = xi
            sr = s_ref[row, :ns]
            si = s_ref[row, ns:]
            out.append((ar * xr - ai * xi + sr, ar * xi + ai * xr + si))
        return tuple(out)

    zero = jnp.zeros((1, ns), F32)
    lax.fori_loop(0, chunks, step, ((zero, zero),) * batch)


def _s5_chunk_scan(s, l16, *, batch):
    nc = s.shape[0]
    ns = S5_SLAB_STATES
    return pl.pallas_call(
        functools.partial(_s5_chunk_scan_kernel, batch=batch, chunks=nc // batch),
        grid=(S5_SLABS,),
        in_specs=[pl.BlockSpec((nc, 2 * ns), lambda s_: (0, s_)),
                  pl.BlockSpec((None, 2, ns), lambda s_: (s_, 0, 0))],
        out_specs=pl.BlockSpec((nc, 2 * ns), lambda s_: (0, s_)),
        out_shape=jax.ShapeDtypeStruct(s.shape, F32),
        compiler_params=_params("parallel"),
        name="s5_chunk_scan",
    )(s, l16)


def _s5_output_kernel(x_ref, xp_ref, gr_ref, cp_ref, d_ref, y_ref):
    t_len, w = S5_CHUNK, S5_SLAB
    xc = _chunk_rows(x_ref)
    xp = xp_ref[...].astype(BF16)
    d = d_ref[...]
    for t in range(t_len):
        y = _dot(xc[:, :(t + 1) * w], gr_ref[(t_len - 1 - t) * w:, :])
        y = y + _dot(xp, cp_ref[t])
        y_ref[:, t, :] = y + d * x_ref[:, t, :]


def _s5_output(x3, xp, gr, cp, d, *, rows):
    nc = x3.shape[0]
    t_len, w, ns = S5_CHUNK, S5_SLAB, S5_SLAB_STATES
    once = dict(pipeline_mode=pl.Buffered(1))
    return pl.pallas_call(
        _s5_output_kernel,
        grid=(S5_SLABS, nc // rows),
        in_specs=[pl.BlockSpec((rows, t_len, w), lambda s, r: (r, 0, s)),
                  pl.BlockSpec((rows, 2 * ns), lambda s, r: (r, s)),
                  pl.BlockSpec((None, t_len * w, w), lambda s, r: (s, 0, 0), **once),
                  pl.BlockSpec((None, t_len, 2 * ns, w), lambda s, r: (s, 0, 0, 0), **once),
                  pl.BlockSpec((None, 1, w), lambda s, r: (s, 0, 0))],
        out_specs=pl.BlockSpec((rows, t_len, w), lambda s, r: (r, 0, s)),
        out_shape=jax.ShapeDtypeStruct(x3.shape, F32),
        compiler_params=_params("parallel", "arbitrary"),
        name="s5_chunk_output",
    )(x3, xp, gr, cp, d)


def _s5(s5_in, lam_re, lam_im, log_step, b_re, b_im, c_re, c_im, d, *, batch):
    m, width = s5_in.shape
    sl, sg, p, grp = S5_SLABS, S5_SLAB_GROUPS, S5_STATE, S5_GROUP
    ns = S5_SLAB_STATES
    lam = jnp.stack([lam_re, lam_im], axis=0).astype(F32).reshape(2, sl, ns).transpose(1, 0, 2)
    ls = jnp.repeat(log_step.astype(F32), p).reshape(sl, 1, ns)
    eye = jnp.eye(sg, dtype=F32)
    b = jnp.stack([b_re, b_im], axis=0).astype(F32).reshape(2, sl, sg, p, grp)
    bdb = jnp.einsum('rsgpi,gh->srgihp', b, eye).reshape(sl, 2, sg * grp, ns)
    c = jnp.stack([c_re, c_im], axis=0).astype(F32).reshape(2, sl, sg, grp, p)
    bdc = jnp.einsum('rsgop,gh->srhpgo', c, eye).reshape(sl, 2, ns, sg * grp)
    bp, cp, gr, l16 = _s5_matrices(lam, ls, bdb, bdc)
    x3 = s5_in.reshape(m // S5_CHUNK, S5_CHUNK, width)
    states = _s5_states(x3, bp, rows=512)
    xprev = _s5_chunk_scan(states, l16, batch=batch)
    y3 = _s5_output(x3, xprev, gr, cp, d.astype(F32).reshape(sl, 1, sg * grp), rows=256)
    return y3.reshape(m, width)


def _glu_kernel(y_ref, w_ref, b_ref, o_ref):
    g = jax.nn.gelu(y_ref[...])
    o_ref[...] = (g * jax.nn.sigmoid(_dot(g.astype(BF16), w_ref[...]) + b_ref[...])).astype(o_ref.dtype)


def _glu(y, w, b, *, bm=512):
    m, n = y.shape
    return pl.pallas_call(
        _glu_kernel,
        grid=(m // bm,),
        in_specs=[pl.BlockSpec((bm, n), lambda i: (i, 0)),
                  pl.BlockSpec((n, n), lambda i: (0, 0)),
                  pl.BlockSpec((1, n), lambda i: (0, 0))],
        out_specs=pl.BlockSpec((bm, n), lambda i: (i, 0)),
        out_shape=jax.ShapeDtypeStruct((m, n), BF16),
        compiler_params=_params("parallel"),
        name="s5_glu",
    )(y, w, b)


def _merge_kernel(u_ref, a_ref, s_ref, wgf_ref, wgs_ref, wpf_ref, wps_ref, bgf_ref, bgs_ref, o_ref):
    u = u_ref[...]
    gate_fox = jax.nn.sigmoid(_dot(u, wgf_ref[...]) + bgf_ref[...])
    gate_s5 = jax.nn.sigmoid(_dot(u, wgs_ref[...]) + bgs_ref[...])
    o_ref[...] = (gate_fox * _dot(a_ref[...], wpf_ref[...])
                  + gate_s5 * _dot(s_ref[...], wps_ref[...])).astype(o_ref.dtype)


def _merge(u, attn, ssm, wgf, wgs, wpf, wps, bgf, bgs, *, bm=512, bn=512):
    m, d = u.shape
    n = wgf.shape[1]
    act = lambda a: pl.BlockSpec((bm, a.shape[1]), lambda i, j: (i, 0))
    wgt = lambda w: pl.BlockSpec((w.shape[0], bn), lambda i, j: (0, j))
    return pl.pallas_call(
        _merge_kernel,
        grid=(m // bm, n // bn),
        in_specs=[act(u), act(attn), act(ssm), wgt(wgf), wgt(wgs), wgt(wpf), wgt(wps),
                  pl.BlockSpec((1, bn), lambda i, j: (0, j)),
                  pl.BlockSpec((1, bn), lambda i, j: (0, j))],
        out_specs=pl.BlockSpec((bm, bn), lambda i, j: (i, j)),
        out_shape=jax.ShapeDtypeStruct((m, n), BF16),
        compiler_params=_params("parallel", "arbitrary"),
        name="gated_merge",
    )(u, attn, ssm, wgf, wgs, wpf, wps, bgf, bgs)


NORM_LANES = 128


def _out_proj_kernel(a_ref, w_ref, x_ref, g_ref, h_ref, hg_ref, r_ref, ssq_ref, *, d_model):
    j = pl.program_id(1)

    @pl.when(j == 0)
    def _():
        ssq_ref[...] = jnp.zeros_like(ssq_ref)

    h = x_ref[...] + _dot(a_ref[...], w_ref[...])
    h_ref[...] = h
    hg_ref[...] = (h * g_ref[...]).astype(hg_ref.dtype)
    ssq_ref[...] += jnp.sum(h * h, axis=-1, keepdims=True)

    @pl.when(j == pl.num_programs(1) - 1)
    def _():
        r_ref[...] = jnp.broadcast_to(lax.rsqrt(ssq_ref[...] * (1.0 / d_model) + RMS_EPS), r_ref.shape)


def _out_proj(a, w, x, g, *, bm, bn):
    m, k = a.shape
    n = w.shape[1]
    return pl.pallas_call(
        functools.partial(_out_proj_kernel, d_model=n),
        grid=(m // bm, n // bn),
        in_specs=[pl.BlockSpec((bm, k), lambda i, j: (i, 0)),
                  pl.BlockSpec((k, bn), lambda i, j: (0, j)),
                  pl.BlockSpec((bm, bn), lambda i, j: (i, j)),
                  pl.BlockSpec((1, bn), lambda i, j: (0, j))],
        out_specs=[pl.BlockSpec((bm, bn), lambda i, j: (i, j)),
                   pl.BlockSpec((bm, bn), lambda i, j: (i, j)),
                   pl.BlockSpec((bm, NORM_LANES), lambda i, j: (i, 0))],
        out_shape=[jax.ShapeDtypeStruct((m, n), F32),
                   jax.ShapeDtypeStruct((m, n), BF16),
                   jax.ShapeDtypeStruct((m, NORM_LANES), F32)],
        scratch_shapes=[pltpu.VMEM((bm, 1), F32)],
        compiler_params=_params("parallel", "arbitrary"),
        name="out_proj_residual",
    )(a, w, x, g)


def _round_weights_once(w_ref, wb_ref):
    @pl.when(pl.program_id(1) == 0)
    def _():
        wb_ref[...] = w_ref[...].astype(BF16)


def _swiglu_kernel(a_ref, r_ref, wg_ref, wu_ref, o_ref, wgb_ref, wub_ref):
    _round_weights_once(wg_ref, wgb_ref)
    _round_weights_once(wu_ref, wub_ref)
    a = a_ref[...]
    r = jnp.concatenate([r_ref[...]] * (o_ref.shape[1] // NORM_LANES), axis=1)
    gate = _dot(a, wgb_ref[...]) * r
    up = _dot(a, wub_ref[...]) * r
    o_ref[...] = (jax.nn.silu(gate) * up).astype(o_ref.dtype)


def _swiglu_up(a, r, w_gate_up, d_ff, *, bm, bn):
    m, k = a.shape
    nj = d_ff // bn
    return pl.pallas_call(
        _swiglu_kernel,
        grid=(nj, m // bm),
        in_specs=[pl.BlockSpec((bm, k), lambda j, i: (i, 0)),
                  pl.BlockSpec((bm, NORM_LANES), lambda j, i: (i, 0)),
                  pl.BlockSpec((k, bn), lambda j, i: (0, j)),
                  pl.BlockSpec((k, bn), lambda j, i: (0, j + nj))],
        out_specs=pl.BlockSpec((bm, bn), lambda j, i: (i, j)),
        out_shape=jax.ShapeDtypeStruct((m, d_ff), BF16),
        scratch_shapes=[pltpu.VMEM((k, bn), BF16)] * 2,
        compiler_params=_params("parallel", "arbitrary"),
        name="swiglu_up",
    )(a, r, w_gate_up, w_gate_up)


def _down_kernel(a_ref, w_ref, r_ref, o_ref, acc_ref):
    kk = pl.program_id(2)

    @pl.when(kk == 0)
    def _():
        acc_ref[...] = r_ref[...]

    acc_ref[...] += _dot(a_ref[...], w_ref[...])

    @pl.when(kk == pl.num_programs(2) - 1)
    def _():
        o_ref[...] = acc_ref[...]


def _down_residual(a, w, res, *, bm, bn, bk):
    m, k = a.shape
    n = w.shape[1]
    return pl.pallas_call(
        _down_kernel,
        grid=(m // bm, n // bn, k // bk),
        in_specs=[pl.BlockSpec((bm, bk), lambda i, j, kk: (i, kk)),
                  pl.BlockSpec((bk, bn), lambda i, j, kk: (kk, j)),
                  pl.BlockSpec((bm, bn), lambda i, j, kk: (i, j))],
        out_specs=pl.BlockSpec((bm, bn), lambda i, j, kk: (i, j)),
        out_shape=jax.ShapeDtypeStruct((m, n), F32),
        scratch_shapes=[pltpu.VMEM((bm, bn), F32)],
        compiler_params=_params("parallel", "arbitrary", "arbitrary"),
        name="down_proj_residual",
    )(a, w, res)


def _layer(x, batch, seq, g_mix, w_in, b_fgate, b_gates, q_norm, k_norm,
           lam_re, lam_im, log_step, b_re, b_im, c_re, c_im, s5_d,
           w_glu, b_glu, w_proj_fox, w_proj_s5, w_out, g_ffn, w_gate_up, w_down):
    m, d_model = x.shape
    fox_w = FOX_HEADS * FOX_HEAD_DIM
    s5_w = S5_GROUP * S5_GROUPS
    col_k, col_v = fox_w, 2 * fox_w
    col_f = 3 * fox_w
    col_s5 = col_f + FOX_HEADS
    col_g = col_s5 + s5_w
    d_ff = w_down.shape[0]
    w_in_b = w_in.astype(BF16)
    narrow = lambda lo, hi: w_in_b[:, lo:hi]

    u = _rmsnorm(x, g_mix)

    head_scale = jnp.concatenate([
        jnp.tile(q_norm.astype(F32), FOX_HEADS) * (LOG2_E / math.sqrt(FOX_HEAD_DIM)),
        jnp.tile(k_norm.astype(F32), FOX_HEADS),
        jnp.ones((fox_w,), F32)]).reshape(1, 3 * fox_w)
    qkv = _qkv_proj(u, w_in_b, head_scale, bm=1024, bn=1024)
    s5_in = _matmul(u, narrow(col_s5, col_g), F32, bm=1024, bn=1024, name="s5_in_proj")

    blk = 512
    f_t = _forget_cumsum(u, narrow(col_f, col_s5).T, b_fgate.astype(F32).reshape(FOX_HEADS, 1),
                         batch, seq)
    f_rows = f_t.reshape(batch, FOX_HEADS, seq // blk, 1, blk)
    qk_bound = (1.02 * LOG2_E * math.sqrt(FOX_HEAD_DIM)
                * jnp.max(jnp.abs(q_norm.astype(F32))) * jnp.max(jnp.abs(k_norm.astype(F32))))
    attn = _attention(qkv.reshape(batch, seq, 3 * fox_w), f_rows, qk_bound, blk=blk).reshape(m, fox_w)

    y5 = _s5(s5_in, lam_re, lam_im, log_step, b_re, b_im, c_re, c_im, s5_d, batch=batch)
    ssm = _glu(y5, w_glu.astype(BF16), b_glu.astype(F32).reshape(1, s5_w))

    b_gates = b_gates.astype(F32).reshape(1, 2 * d_model)
    merged = _merge(u, attn, ssm, narrow(col_g, col_g + d_model), narrow(col_g + d_model, col_g + 2 * d_model),
                    w_proj_fox.astype(BF16), w_proj_s5.astype(BF16),
                    b_gates[:, :d_model], b_gates[:, d_model:])
    h, hg, h_inv_rms = _out_proj(merged, w_out.astype(BF16), x, g_ffn.astype(F32).reshape(1, d_model),
                                 bm=1024, bn=512)
    act = _swiglu_up(hg, h_inv_rms, w_gate_up, d_ff, bm=1024, bn=256)
    return _down_residual(act, w_down.astype(BF16), h, bm=1024, bn=512, bk=d_ff // 2)


def kernel(x, g_mix, w_in, b_fgate, b_gates, q_norm, k_norm, s5_lambda_re, s5_lambda_im, s5_log_step, s5_b_re, s5_b_im, s5_c_re, s5_c_im, s5_d, w_glu, b_glu, w_proj_fox, w_proj_s5, w_out, g_ffn, w_gate_up, w_down):
    batch, seq, d_model = x.shape
    h = x.reshape(batch * seq, d_model)
    for l in range(g_mix.shape[0]):
        h = _layer(h, batch, seq, g_mix[l], w_in[l], b_fgate[l], b_gates[l], q_norm[l], k_norm[l],
                   s5_lambda_re[l], s5_lambda_im[l], s5_log_step[l], s5_b_re[l], s5_b_im[l],
                   s5_c_re[l], s5_c_im[l], s5_d[l], w_glu[l], b_glu[l],
                   w_proj_fox[l], w_proj_s5[l], w_out[l], g_ffn[l], w_gate_up[l], w_down[l])
    return h.reshape(batch, seq, d_model)
```

```python
import functools
import math

import jax
import jax.numpy as jnp
from jax import lax
from jax.experimental import pallas as pl
from jax.experimental.pallas import tpu as pltpu

F32 = jnp.float32
BF16 = jnp.bfloat16

FOX_HEADS = 16
FOX_HEAD_DIM = 128
S5_GROUP = 16
S5_GROUPS = 64
S5_STATE = 64
S5_CHUNK = 16
RMS_EPS = 1e-6
MASK_VALUE = -1e30
LOG2_E = math.log2(math.e)

VMEM_LIMIT_BYTES = 56 * 1024 * 1024


def _params(*semantics):
    return pltpu.CompilerParams(dimension_semantics=semantics,
                                vmem_limit_bytes=VMEM_LIMIT_BYTES)


def _dot(a, b):
    return jnp.dot(a, b, preferred_element_type=F32)


def _rmsnorm_kernel(x_ref, g_ref, o_ref):
    x = x_ref[...]
    ms = jnp.mean(x * x, axis=-1, keepdims=True)
    o_ref[...] = (x * lax.rsqrt(ms + RMS_EPS) * g_ref[...]).astype(o_ref.dtype)


def _rmsnorm(x, g, *, bm=256):
    m, d = x.shape
    return pl.pallas_call(
        _rmsnorm_kernel,
        grid=(m // bm,),
        in_specs=[pl.BlockSpec((bm, d), lambda i: (i, 0)),
                  pl.BlockSpec((1, d), lambda i: (0, 0))],
        out_specs=pl.BlockSpec((bm, d), lambda i: (i, 0)),
        out_shape=jax.ShapeDtypeStruct((m, d), BF16),
        compiler_params=_params("parallel"),
        name="rmsnorm",
    )(x, g.reshape(1, d).astype(F32))


def _mm_kernel(a_ref, w_ref, o_ref):
    o_ref[...] = _dot(a_ref[...], w_ref[...]).astype(o_ref.dtype)


def _matmul(a, w, out_dtype, *, bm, bn, name):
    m, k = a.shape
    n = w.shape[1]
    return pl.pallas_call(
        _mm_kernel,
        grid=(m // bm, n // bn),
        in_specs=[pl.BlockSpec((bm, k), lambda i, j: (i, 0)),
                  pl.BlockSpec((k, bn), lambda i, j: (0, j))],
        out_specs=pl.BlockSpec((bm, bn), lambda i, j: (i, j)),
        out_shape=jax.ShapeDtypeStruct((m, n), out_dtype),
        compiler_params=_params("parallel", "arbitrary"),
        name=name,
    )(a, w)


def _split_in_proj_kernel(w_ref, *out_refs, bounds):
    w = w_ref[...]
    for o_ref, (lo, hi) in zip(out_refs, bounds):
        o_ref[...] = w[:, lo:hi].astype(o_ref.dtype)


def _split_in_proj(w_in, bounds, *, rows=256):
    k, n = w_in.shape
    return pl.pallas_call(
        functools.partial(_split_in_proj_kernel, bounds=bounds),
        grid=(k // rows,),
        in_specs=[pl.BlockSpec((rows, n), lambda i: (i, 0))],
        out_specs=[pl.BlockSpec((rows, hi - lo), lambda i: (i, 0)) for lo, hi in bounds],
        out_shape=[jax.ShapeDtypeStruct((k, hi - lo), BF16) for lo, hi in bounds],
        compiler_params=_params("parallel"),
        name="split_in_proj",
    )(w_in)


def _qkv_kernel(a_ref, w_ref, s_ref, o_ref, *, qk_tiles):
    acc = _dot(a_ref[...], w_ref[...])
    is_qk = pl.program_id(1) < qk_tiles
    for h in range(acc.shape[1] // FOX_HEAD_DIM):
        cols = slice(h * FOX_HEAD_DIM, (h + 1) * FOX_HEAD_DIM)
        blk = acc[:, cols]
        ms = jnp.mean(blk * blk, axis=-1, keepdims=True)
        inv = jnp.where(is_qk, lax.rsqrt(ms + RMS_EPS), 1.0)
        o_ref[:, cols] = (blk * inv * s_ref[:, cols]).astype(o_ref.dtype)


def _qkv_proj(a, w_in, scale, *, bm, bn):
    m, k = a.shape
    n = scale.shape[1]
    qk_tiles = 2 * FOX_HEADS * FOX_HEAD_DIM // bn
    return pl.pallas_call(
        functools.partial(_qkv_kernel, qk_tiles=qk_tiles),
        grid=(m // bm, n // bn),
        in_specs=[pl.BlockSpec((bm, k), lambda i, j: (i, 0)),
                  pl.BlockSpec((k, bn), lambda i, j: (0, j)),
                  pl.BlockSpec((1, bn), lambda i, j: (0, j))],
        out_specs=pl.BlockSpec((bm, bn), lambda i, j: (i, j)),
        out_shape=jax.ShapeDtypeStruct((m, n), BF16),
        compiler_params=_params("parallel", "arbitrary"),
        name="qkv_proj",
    )(a, w_in, scale)


def _fgate_kernel(u_ref, wt_ref, b_ref, o_ref, carry_ref):
    @pl.when(pl.program_id(1) == 0)
    def _():
        carry_ref[...] = jnp.zeros_like(carry_ref)

    z = lax.dot_general(wt_ref[...], u_ref[...], (((1,), (1,)), ((), ())),
                        preferred_element_type=F32) + b_ref[...]
    x = jnp.minimum(z, 0.0) - jnp.log1p(jnp.exp(-jnp.abs(z)))
    bs = x.shape[1]
    lane = lax.broadcasted_iota(jnp.int32, x.shape, 1)
    shift = 1
    while shift < bs:
        x = x + jnp.where(lane >= shift, pltpu.roll(x, shift, 1), 0.0)
        shift *= 2
    x = x + carry_ref[:, 0:1]
    o_ref[...] = x * LOG2_E
    carry_ref[...] = jnp.broadcast_to(x[:, bs - 1:bs], carry_ref.shape)


def _forget_cumsum(u, wt, bias, batch, seq, *, bs=512):
    h, d = wt.shape
    ns = seq // bs
    return pl.pallas_call(
        _fgate_kernel,
        grid=(batch, ns),
        in_specs=[pl.BlockSpec((bs, d), lambda b, j: (b * ns + j, 0)),
                  pl.BlockSpec((h, d), lambda b, j: (0, 0)),
                  pl.BlockSpec((h, 1), lambda b, j: (0, 0))],
        out_specs=pl.BlockSpec((None, h, bs), lambda b, j: (b, 0, j)),
        out_shape=jax.ShapeDtypeStruct((batch, h, seq), F32),
        scratch_shapes=[pltpu.VMEM((h, 128), F32)],
        compiler_params=_params("parallel", "arbitrary"),
        name="forget_cumsum",
    )(u, wt, bias)


ATTN_HEADS_PER_STEP = 4


def _attn_kernel(q_ref, k_ref, v_ref, f_ref, o_ref, *, blk):
    i = pl.program_id(2)
    dh = FOX_HEAD_DIM
    heads = range(ATTN_HEADS_PER_STEP)
    qs = [q_ref[:, h * dh:(h + 1) * dh] for h in heads]

    def block(j, carry, masked):
        start = pl.multiple_of(j * blk, blk)
        out = []
        for h in heads:
            m, l, acc = carry[h]
            ks = k_ref[pl.ds(start, blk), h * dh:(h + 1) * dh]
            vs = v_ref[pl.ds(start, blk), h * dh:(h + 1) * dh]
            s = lax.dot_general(qs[h], ks, (((1,), (1,)), ((), ())), preferred_element_type=F32)
            s = s - f_ref[h, j]
            if masked:
                row = lax.broadcasted_iota(jnp.int32, s.shape, 0)
                col = lax.broadcasted_iota(jnp.int32, s.shape, 1)
                s = jnp.where(col <= row, s, MASK_VALUE)
            m_new = jnp.maximum(m, jnp.max(s, axis=-1, keepdims=True))
            alpha = jnp.exp2(m - m_new)
            p = jnp.exp2(s - m_new)
            l = alpha * l + jnp.sum(p, axis=-1, keepdims=True)
            acc = alpha * acc + _dot(p.astype(BF16), vs)
            out.append((m_new, l, acc))
        return tuple(out)

    init = tuple((jnp.full((blk, 1), MASK_VALUE, F32), jnp.zeros((blk, 1), F32),
                  jnp.zeros((blk, dh), F32)) for _ in heads)
    carry = lax.fori_loop(0, i, lambda j, c: block(j, c, False), init)
    carry = block(i, carry, True)
    for h in heads:
        _, l, acc = carry[h]
        o_ref[:, h * dh:(h + 1) * dh] = (acc / l).astype(o_ref.dtype)


def _attn_shift_kernel(c_ref, q_ref, k_ref, v_ref, f_ref, o_ref, acc_ref, shift_ref, p_ref, *, blk):
    i = pl.program_id(2)
    dh = FOX_HEAD_DIM
    heads = range(ATTN_HEADS_PER_STEP)
    qs = [q_ref[:, h * dh:(h + 1) * dh] for h in heads]
    ones = jnp.ones((blk, dh), BF16)

    def setup():
        for h in heads:
            col = jnp.broadcast_to(f_ref[h, i], (8, blk)).T[:, 0:1] - c_ref[0]
            shift_ref[h] = jnp.broadcast_to(col, (blk, dh))
        acc_ref[...] = jnp.zeros_like(acc_ref)

    def probabilities(h, j, masked):
        start = pl.multiple_of(j * blk, blk)
        ks = k_ref[pl.ds(start, blk), h * dh:(h + 1) * dh]
        s = lax.dot_general(qs[h], ks, (((1,), (1,)), ((), ())), preferred_element_type=F32)
        s = (s + jnp.concatenate([shift_ref[h]] * (blk // dh), axis=1)) - f_ref[h, j]
        if masked:
            row = lax.broadcasted_iota(jnp.int32, s.shape, 0)
            col = lax.broadcasted_iota(jnp.int32, s.shape, 1)
            s = jnp.where(col <= row, s, MASK_VALUE)
        return jnp.exp2(s).astype(BF16)

    def values(h, j):
        start = pl.multiple_of(j * blk, blk)
        return jnp.concatenate([v_ref[pl.ds(start, blk), h * dh:(h + 1) * dh], ones], axis=1)

    setup()

    @pl.when(i == 0)
    def _():
        p_ref[...] = jnp.zeros_like(p_ref)

    @pl.when(i > 0)
    def _():
        for h in heads:
            p_ref[h] = probabilities(h, 0, False)

    @pl.loop(1, i)
    def _(j):
        for h in heads:
            acc_ref[h] += _dot(p_ref[h], values(h, j - 1))
        for h in heads:
            p_ref[h] = probabilities(h, j, False)

    prev = jnp.maximum(i - 1, 0)
    for h in heads:
        acc_ref[h] += _dot(p_ref[h], values(h, prev)) + _dot(probabilities(h, i, True), values(h, i))
    for h in heads:
        acc = acc_ref[h]
        o_ref[:, h * dh:(h + 1) * dh] = (acc[:, :dh] / acc[:, dh:]).astype(o_ref.dtype)


ATTN_SHIFT_LIMIT = 56.0


def _attention(qkv, f_rows, qk_bound, *, blk=512):
    batch, seq, _ = qkv.shape
    nblk = seq // blk
    hps = ATTN_HEADS_PER_STEP
    wide = hps * FOX_HEAD_DIM
    third = FOX_HEADS // hps
    specs = [pl.BlockSpec((None, blk, wide), lambda b, h, i: (b, i, h)),
             pl.BlockSpec((None, seq, wide), lambda b, h, i: (b, 0, h + third)),
             pl.BlockSpec((None, seq, wide), lambda b, h, i: (b, 0, h + 2 * third)),
             pl.BlockSpec((None, hps, nblk, 1, blk), lambda b, h, i: (b, h, 0, 0, 0))]
    common = dict(
        grid=(batch, FOX_HEADS // hps, nblk),
        out_specs=pl.BlockSpec((None, blk, wide), lambda b, h, i: (b, i, h)),
        out_shape=jax.ShapeDtypeStruct((batch, seq, FOX_HEADS * FOX_HEAD_DIM), BF16),
        compiler_params=_params("parallel", "parallel", "arbitrary"))

    def shifted(qkv, f_rows, c):
        return pl.pallas_call(
            functools.partial(_attn_shift_kernel, blk=blk),
            in_specs=[pl.BlockSpec(memory_space=pltpu.SMEM)] + specs,
            scratch_shapes=[pltpu.VMEM((hps, blk, 2 * FOX_HEAD_DIM), F32),
                            pltpu.VMEM((hps, blk, FOX_HEAD_DIM), F32),
                            pltpu.VMEM((hps, blk, blk), BF16)],
            name="fox_attention_shift", **common)(c.reshape(1), qkv, qkv, qkv, f_rows)

    def online(qkv, f_rows, c):
        return pl.pallas_call(functools.partial(_attn_kernel, blk=blk), in_specs=specs,
                              name="fox_attention_online", **common)(qkv, qkv, qkv, f_rows)

    return lax.cond(qk_bound <= ATTN_SHIFT_LIMIT, shifted, online, qkv, f_rows, qk_bound)


S5_SLAB_GROUPS = 16
S5_SLAB = S5_SLAB_GROUPS * S5_GROUP
S5_SLAB_STATES = S5_SLAB_GROUPS * S5_STATE
S5_SLABS = S5_GROUPS // S5_SLAB_GROUPS


def _dot_split(a, b):
    a_hi = a.astype(BF16)
    b_hi = b.astype(BF16)
    a_lo = (a - a_hi.astype(F32)).astype(BF16)
    b_lo = (b - b_hi.astype(F32)).astype(BF16)
    return _dot(a_hi, b_hi) + _dot(a_hi, b_lo) + _dot(a_lo, b_hi)


def _lam_pow(k, lr, li, dt):
    mag = jnp.exp(k * (lr * dt))
    ang = k * (li * dt)
    return mag * jnp.cos(ang), mag * jnp.sin(ang)


def _s5_matrices_kernel(lam_ref, ls_ref, bdb_ref, bdc_ref, bp_ref, cp_ref, gr_ref, l16_ref):
    t = pl.program_id(1)
    ns = S5_SLAB_STATES
    lr, li = lam_ref[0:1, :], lam_ref[1:2, :]
    dt = jnp.exp(ls_ref[...])
    lbr, lbi = _lam_pow(1.0, lr, li, dt)
    den = lr * lr + li * li
    nr = lbr - 1.0
    fr = (nr * lr + lbi * li) / den
    fi = (lbi * lr - nr * li) / den
    b_re, b_im = bdb_ref[0], bdb_ref[1]
    bb_re = fr * b_re - fi * b_im
    bb_im = fr * b_im + fi * b_re
    pr, pi = _lam_pow((S5_CHUNK - 1 - t).astype(F32), lr, li, dt)
    bp_re = bb_re * pr - bb_im * pi
    bp_im = bb_re * pi + bb_im * pr
    bp_ref[:, :ns] = bp_re.astype(BF16)
    bp_ref[:, ns:] = bp_im.astype(BF16)

    qr, qi = _lam_pow((t + 1).astype(F32), lr, li, dt)
    qr = jnp.broadcast_to(qr, (8, ns)).T[:, 0:1]
    qi = jnp.broadcast_to(qi, (8, ns)).T[:, 0:1]
    c_re, c_im = bdc_ref[0], bdc_ref[1]
    cp_ref[:ns, :] = (c_re * qr - c_im * qi).astype(BF16)
    cp_ref[ns:, :] = (-(c_re * qi + c_im * qr)).astype(BF16)

    gr_ref[...] = (_dot_split(bp_re, c_re) - _dot_split(bp_im, c_im)).astype(BF16)

    l16r, l16i = _lam_pow(float(S5_CHUNK), lr, li, dt)
    l16_ref[0:1, :] = l16r
    l16_ref[1:2, :] = l16i


def _s5_matrices(lam, ls, bdb, bdc):
    t_len, w, ns = S5_CHUNK, S5_SLAB, S5_SLAB_STATES
    return pl.pallas_call(
        _s5_matrices_kernel,
        grid=(S5_SLABS, t_len),
        in_specs=[pl.BlockSpec((None, 2, ns), lambda s, t: (s, 0, 0)),
                  pl.BlockSpec((None, 1, ns), lambda s, t: (s, 0, 0)),
                  pl.BlockSpec((None, 2, w, ns), lambda s, t: (s, 0, 0, 0)),
                  pl.BlockSpec((None, 2, ns, w), lambda s, t: (s, 0, 0, 0))],
        out_specs=[pl.BlockSpec((None, w, 2 * ns), lambda s, t: (s, t, 0)),
                   pl.BlockSpec((None, None, 2 * ns, w), lambda s, t: (s, t, 0, 0)),
                   pl.BlockSpec((None, w, w), lambda s, t: (s, t, 0)),
                   pl.BlockSpec((None, 2, ns), lambda s, t: (s, 0, 0))],
        out_shape=[jax.ShapeDtypeStruct((S5_SLABS, t_len * w, 2 * ns), BF16),
                   jax.ShapeDtypeStruct((S5_SLABS, t_len, 2 * ns, w), BF16),
                   jax.ShapeDtypeStruct((S5_SLABS, t_len * w, w), BF16),
                   jax.ShapeDtypeStruct((S5_SLABS, 2, ns), F32)],
        compiler_params=_params("parallel", "arbitrary"),
        name="s5_matrices",
    )(lam, ls, bdb, bdc)


def _chunk_rows(x_ref):
    return jnp.concatenate([x_ref[:, t, :] for t in range(S5_CHUNK)], axis=1).astype(BF16)


def _s5_states_kernel(x_ref, bp_ref, s_ref):
    s_ref[...] = _dot(_chunk_rows(x_ref), bp_ref[...])


def _s5_states(x3, bp, *, rows):
    nc = x3.shape[0]
    t_len, w, ns = S5_CHUNK, S5_SLAB, S5_SLAB_STATES
    return pl.pallas_call(
        _s5_states_kernel,
        grid=(S5_SLABS, nc // rows),
        in_specs=[pl.BlockSpec((rows, t_len, w), lambda s, r: (r, 0, s)),
                  pl.BlockSpec((None, t_len * w, 2 * ns), lambda s, r: (s, 0, 0),
                               pipeline_mode=pl.Buffered(1))],
        out_specs=pl.BlockSpec((rows, 2 * ns), lambda s, r: (r, s)),
        out_shape=jax.ShapeDtypeStruct((nc, S5_SLABS * 2 * ns), F32),
        compiler_params=_params("parallel", "arbitrary"),
        name="s5_chunk_states",
    )(x3, bp)


def _s5_chunk_scan_kernel(s_ref, l16_ref, xp_ref, *, batch, chunks):
    ns = S5_SLAB_STATES
    ar, ai = l16_ref[0:1, :], l16_ref[1:2, :]

    def step(c, carry):
        out = []
        for b, (xr, xi) in enumerate(carry):
            row = pl.ds(b * chunks + c, 1)
            xp_ref[row, :ns] = xr
            xp_ref[row, ns:] = xi
            sr = s_ref[row, :ns]
            si = s_ref[row, ns:]
            out.append((ar * xr - ai * xi + sr, ar * xi + ai * xr + si))
        return tuple(out)

    zero = jnp.zeros((1, ns), F32)
    lax.fori_loop(0, chunks, step, ((zero, zero),) * batch)


def _s5_chunk_scan(s, l16, *, batch):
    nc = s.shape[0]
    ns = S5_SLAB_STATES
    return pl.pallas_call(
        functools.partial(_s5_chunk_scan_kernel, batch=batch, chunks=nc // batch),
        grid=(S5_SLABS,),
        in_specs=[pl.BlockSpec((nc, 2 * ns), lambda s_: (0, s_)),
                  pl.BlockSpec((None, 2, ns), lambda s_: (s_, 0, 0))],
        out_specs=pl.BlockSpec((nc, 2 * ns), lambda s_: (0, s_)),
        out_shape=jax.ShapeDtypeStruct(s.shape, F32),
        compiler_params=_params("parallel"),
        name="s5_chunk_scan",
    )(s, l16)


def _s5_output_kernel(x_ref, xp_ref, gr_ref, cp_ref, d_ref, y_ref):
    t_len, w = S5_CHUNK, S5_SLAB
    xc = _chunk_rows(x_ref)
    xp = xp_ref[...].astype(BF16)
    d = d_ref[...]
    for t in range(t_len):
        y = _dot(xc[:, :(t + 1) * w], gr_ref[(t_len - 1 - t) * w:, :])
        y = y + _dot(xp, cp_ref[t])
        y_ref[:, t, :] = y + d * x_ref[:, t, :]


def _s5_output(x3, xp, gr, cp, d, *, rows):
    nc = x3.shape[0]
    t_len, w, ns = S5_CHUNK, S5_SLAB, S5_SLAB_STATES
    once = dict(pipeline_mode=pl.Buffered(1))
    return pl.pallas_call(
        _s5_output_kernel,
        grid=(S5_SLABS, nc // rows),
        in_specs=[pl.BlockSpec((rows, t_len, w), lambda s, r: (r, 0, s)),
                  pl.BlockSpec((rows, 2 * ns), lambda s, r: (r, s)),
                  pl.BlockSpec((None, t_len * w, w), lambda s, r: (s, 0, 0), **once),
                  pl.BlockSpec((None, t_len, 2 * ns, w), lambda s, r: (s, 0, 0, 0), **once),
                  pl.BlockSpec((None, 1, w), lambda s, r: (s, 0, 0))],
        out_specs=pl.BlockSpec((rows, t_len, w), lambda s, r: (r, 0, s)),
        out_shape=jax.ShapeDtypeStruct(x3.shape, F32),
        compiler_params=_params("parallel", "arbitrary"),
        name="s5_chunk_output",
    )(x3, xp, gr, cp, d)


def _s5(s5_in, lam_re, lam_im, log_step, b_re, b_im, c_re, c_im, d, *, batch):
    m, width = s5_in.shape
    sl, sg, p, grp = S5_SLABS, S5_SLAB_GROUPS, S5_STATE, S5_GROUP
    ns = S5_SLAB_STATES
    lam = jnp.stack([lam_re, lam_im], axis=0).astype(F32).reshape(2, sl, ns).transpose(1, 0, 2)
    ls = jnp.repeat(log_step.astype(F32), p).reshape(sl, 1, ns)
    eye = jnp.eye(sg, dtype=F32)
    b = jnp.stack([b_re, b_im], axis=0).astype(F32).reshape(2, sl, sg, p, grp)
    bdb = jnp.einsum('rsgpi,gh->srgihp', b, eye).reshape(sl, 2, sg * grp, ns)
    c = jnp.stack([c_re, c_im], axis=0).astype(F32).reshape(2, sl, sg, grp, p)
    bdc = jnp.einsum('rsgop,gh->srhpgo', c, eye).reshape(sl, 2, ns, sg * grp)
    bp, cp, gr, l16 = _s5_matrices(lam, ls, bdb, bdc)
    x3 = s5_in.reshape(m // S5_CHUNK, S5_CHUNK, width)
    states = _s5_states(x3, bp, rows=512)
    xprev = _s5_chunk_scan(states, l16, batch=batch)
    y3 = _s5_output(x3, xprev, gr, cp, d.astype(F32).reshape(sl, 1, sg * grp), rows=256)
    return y3.reshape(m, width)


def _glu_kernel(y_ref, w_ref, b_ref, o_ref):
    g = jax.nn.gelu(y_ref[...])
    o_ref[...] = (g * jax.nn.sigmoid(_dot(g.astype(BF16), w_ref[...]) + b_ref[...])).astype(o_ref.dtype)


def _glu(y, w, b, *, bm=512):
    m, n = y.shape
    return pl.pallas_call(
        _glu_kernel,
        grid=(m // bm,),
        in_specs=[pl.BlockSpec((bm, n), lambda i: (i, 0)),
                  pl.BlockSpec((n, n), lambda i: (0, 0)),
                  pl.BlockSpec((1, n), lambda i: (0, 0))],
        out_specs=pl.BlockSpec((bm, n), lambda i: (i, 0)),
        out_shape=jax.ShapeDtypeStruct((m, n), BF16),
        compiler_params=_params("parallel"),
        name="s5_glu",
    )(y, w, b)


def _merge_kernel(u_ref, a_ref, s_ref, wgf_ref, wgs_ref, wpf_ref, wps_ref, bgf_ref, bgs_ref, o_ref):
    u = u_ref[...]
    gate_fox = jax.nn.sigmoid(_dot(u, wgf_ref[...]) + bgf_ref[...])
    gate_s5 = jax.nn.sigmoid(_dot(u, wgs_ref[...]) + bgs_ref[...])
    o_ref[...] = (gate_fox * _dot(a_ref[...], wpf_ref[...])
                  + gate_s5 * _dot(s_ref[...], wps_ref[...])).astype(o_ref.dtype)


def _merge(u, attn, ssm, wgf, wgs, wpf, wps, bgf, bgs, *, bm=512, bn=512):
    m, d = u.shape
    n = wgf.shape[1]
    act = lambda a: pl.BlockSpec((bm, a.shape[1]), lambda i, j: (i, 0))
    wgt = lambda w: pl.BlockSpec((w.shape[0], bn), lambda i, j: (0, j))
    return pl.pallas_call(
        _merge_kernel,
        grid=(m // bm, n // bn),
        in_specs=[act(u), act(attn), act(ssm), wgt(wgf), wgt(wgs), wgt(wpf), wgt(wps),
                  pl.BlockSpec((1, bn), lambda i, j: (0, j)),
                  pl.BlockSpec((1, bn), lambda i, j: (0, j))],
        out_specs=pl.BlockSpec((bm, bn), lambda i, j: (i, j)),
        out_shape=jax.ShapeDtypeStruct((m, n), BF16),
        compiler_params=_params("parallel", "arbitrary"),
        name="gated_merge",
    )(u, attn, ssm, wgf, wgs, wpf, wps, bgf, bgs)


NORM_LANES = 128


def _out_proj_kernel(a_ref, w_ref, x_ref, g_ref, h_ref, hg_ref, r_ref, ssq_ref, *, d_model):
    j = pl.program_id(1)

    @pl.when(j == 0)
    def _():
        ssq_ref[...] = jnp.zeros_like(ssq_ref)

    h = x_ref[...] + _dot(a_ref[...], w_ref[...])
    h_ref[...] = h
    hg_ref[...] = (h * g_ref[...]).astype(hg_ref.dtype)
    ssq_ref[...] += jnp.sum(h * h, axis=-1, keepdims=True)

    @pl.when(j == pl.num_programs(1) - 1)
    def _():
        r_ref[...] = jnp.broadcast_to(lax.rsqrt(ssq_ref[...] * (1.0 / d_model) + RMS_EPS), r_ref.shape)


def _out_proj(a, w, x, g, *, bm, bn):
    m, k = a.shape
    n = w.shape[1]
    return pl.pallas_call(
        functools.partial(_out_proj_kernel, d_model=n),
        grid=(m // bm, n // bn),
        in_specs=[pl.BlockSpec((bm, k), lambda i, j: (i, 0)),
                  pl.BlockSpec((k, bn), lambda i, j: (0, j)),
                  pl.BlockSpec((bm, bn), lambda i, j: (i, j)),
                  pl.BlockSpec((1, bn), lambda i, j: (0, j))],
        out_specs=[pl.BlockSpec((bm, bn), lambda i, j: (i, j)),
                   pl.BlockSpec((bm, bn), lambda i, j: (i, j)),
                   pl.BlockSpec((bm, NORM_LANES), lambda i, j: (i, 0))],
        out_shape=[jax.ShapeDtypeStruct((m, n), F32),
                   jax.ShapeDtypeStruct((m, n), BF16),
                   jax.ShapeDtypeStruct((m, NORM_LANES), F32)],
        scratch_shapes=[pltpu.VMEM((bm, 1), F32)],
        compiler_params=_params("parallel", "arbitrary"),
        name="out_proj_residual",
    )(a, w, x, g)


def _round_weights_once(w_ref, wb_ref):
    @pl.when(pl.program_id(1) == 0)
    def _():
        wb_ref[...] = w_ref[...].astype(BF16)


def _swiglu_kernel(a_ref, r_ref, wg_ref, wu_ref, o_ref, wgb_ref, wub_ref):
    _round_weights_once(wg_ref, wgb_ref)
    _round_weights_once(wu_ref, wub_ref)
    a = a_ref[...]
    r = jnp.concatenate([r_ref[...]] * (o_ref.shape[1] // NORM_LANES), axis=1)
    gate = _dot(a, wgb_ref[...]) * r
    up = _dot(a, wub_ref[...]) * r
    o_ref[...] = (jax.nn.silu(gate) * up).astype(o_ref.dtype)


def _swiglu_up(a, r, w_gate_up, d_ff, *, bm, bn):
    m, k = a.shape
    nj = d_ff // bn
    return pl.pallas_call(
        _swiglu_kernel,
        grid=(nj, m // bm),
        in_specs=[pl.BlockSpec((bm, k), lambda j, i: (i, 0)),
                  pl.BlockSpec((bm, NORM_LANES), lambda j, i: (i, 0)),
                  pl.BlockSpec((k, bn), lambda j, i: (0, j)),
                  pl.BlockSpec((k, bn), lambda j, i: (0, j + nj))],
        out_specs=pl.BlockSpec((bm, bn), lambda j, i: (i, j)),
        out_shape=jax.ShapeDtypeStruct((m, d_ff), BF16),
        scratch_shapes=[pltpu.VMEM((k, bn), BF16)] * 2,
        compiler_params=_params("parallel", "arbitrary"),
        name="swiglu_up",
    )(a, r, w_gate_up, w_gate_up)


def _down_kernel(a_ref, w_ref, r_ref, o_ref):
    o_ref[...] = r_ref[...] + _dot(a_ref[...], w_ref[...])


def _down_residual(a, w, res, *, bm, bn):
    m, k = a.shape
    n = w.shape[1]
    return pl.pallas_call(
        _down_kernel,
        grid=(m // bm, n // bn),
        in_specs=[pl.BlockSpec((bm, k), lambda i, j: (i, 0)),
                  pl.BlockSpec((k, bn), lambda i, j: (0, j)),
                  pl.BlockSpec((bm, bn), lambda i, j: (i, j))],
        out_specs=pl.BlockSpec((bm, bn), lambda i, j: (i, j)),
        out_shape=jax.ShapeDtypeStruct((m, n), F32),
        compiler_params=_params("parallel", "arbitrary"),
        name="down_proj_residual",
    )(a, w, res)


def _layer(x, batch, seq, g_mix, w_in, b_fgate, b_gates, q_norm, k_norm,
           lam_re, lam_im, log_step, b_re, b_im, c_re, c_im, s5_d,
           w_glu, b_glu, w_proj_fox, w_proj_s5, w_out, g_ffn, w_gate_up, w_down):
    m, d_model = x.shape
    fox_w = FOX_HEADS * FOX_HEAD_DIM
    s5_w = S5_GROUP * S5_GROUPS
    col_k, col_v = fox_w, 2 * fox_w
    col_f = 3 * fox_w
    col_s5 = col_f + FOX_HEADS
    col_g = col_s5 + s5_w
    d_ff = w_down.shape[0]
    w_qkv, w_f, w_s5, w_gate_fox, w_gate_s5 = _split_in_proj(
        w_in, ((0, col_f), (col_f, col_s5), (col_s5, col_g), (col_g, col_g + d_model),
               (col_g + d_model, col_g + 2 * d_model)))

    u = _rmsnorm(x, g_mix)

    head_scale = jnp.concatenate([
        jnp.tile(q_norm.astype(F32), FOX_HEADS) * (LOG2_E / math.sqrt(FOX_HEAD_DIM)),
        jnp.tile(k_norm.astype(F32), FOX_HEADS),
        jnp.ones((fox_w,), F32)]).reshape(1, 3 * fox_w)
    qkv = _qkv_proj(u, w_qkv, head_scale, bm=1024, bn=1024)
    s5_in = _matmul(u, w_s5, F32, bm=1024, bn=1024, name="s5_in_proj")

    blk = 512
    f_t = _forget_cumsum(u, w_f.T, b_fgate.astype(F32).reshape(FOX_HEADS, 1),
                         batch, seq)
    f_rows = f_t.reshape(batch, FOX_HEADS, seq // blk, 1, blk)
    qk_bound = (1.02 * LOG2_E * math.sqrt(FOX_HEAD_DIM)
                * jnp.max(jnp.abs(q_norm.astype(F32))) * jnp.max(jnp.abs(k_norm.astype(F32))))
    attn = _attention(qkv.reshape(batch, seq, 3 * fox_w), f_rows, qk_bound, blk=blk).reshape(m, fox_w)

    y5 = _s5(s5_in, lam_re, lam_im, log_step, b_re, b_im, c_re, c_im, s5_d, batch=batch)
    ssm = _glu(y5, w_glu.astype(BF16), b_glu.astype(F32).reshape(1, s5_w))

    b_gates = b_gates.astype(F32).reshape(1, 2 * d_model)
    merged = _merge(u, attn, ssm, w_gate_fox, w_gate_s5,
                    w_proj_fox.astype(BF16), w_proj_s5.astype(BF16),
                    b_gates[:, :d_model], b_gates[:, d_model:])
    h, hg, h_inv_rms = _out_proj(merged, w_out.astype(BF16), x, g_ffn.astype(F32).reshape(1, d_model),
                                 bm=1024, bn=512)
    act = _swiglu_up(hg, h_inv_rms, w_gate_up, d_ff, bm=1024, bn=256)
    return _down_residual(act, w_down.astype(BF16), h, bm=512, bn=512)


def kernel(x, g_mix, w_in, b_fgate, b_gates, q_norm, k_norm, s5_lambda_re, s5_lambda_im, s5_log_step, s5_b_re, s5_b_im, s5_c_re, s5_c_im, s5_d, w_glu, b_glu, w_proj_fox, w_proj_s5, w_out, g_ffn, w_gate_up, w_down):
    batch, seq, d_model = x.shape
    h = x.reshape(batch * seq, d_model)
    for l in range(g_mix.shape[0]):
        h = _layer(h, batch, seq, g_mix[l], w_in[l], b_fgate[l], b_gates[l], q_norm[l], k_norm[l],
                   s5_lambda_re[l], s5_lambda_im[l], s5_log_step[l], s5_b_re[l], s5_b_im[l],
                   s5_c_re[l], s5_c_im[l], s5_d[l], w_glu[l], b_glu[l],
                   w_proj_fox[l], w_proj_s5[l], w_out[l], g_ffn[l], w_gate_up[l], w_down[l])
    return h.reshape(batch, seq, d_model)
```

```python
import functools
import math

import jax
import jax.numpy as jnp
from jax import lax
from jax.experimental import pallas as pl
from jax.experimental.pallas import tpu as pltpu

F32 = jnp.float32
BF16 = jnp.bfloat16

FOX_HEADS = 16
FOX_HEAD_DIM = 128
S5_GROUP = 16
S5_GROUPS = 64
S5_STATE = 64
S5_CHUNK = 16
RMS_EPS = 1e-6
MASK_VALUE = -1e30
LOG2_E = math.log2(math.e)

VMEM_LIMIT_BYTES = 56 * 1024 * 1024


def _params(*semantics):
    return pltpu.CompilerParams(dimension_semantics=semantics,
                                vmem_limit_bytes=VMEM_LIMIT_BYTES)


def _dot(a, b):
    return jnp.dot(a, b, preferred_element_type=F32)


def _rmsnorm_kernel(x_ref, g_ref, o_ref):
    x = x_ref[...]
    ms = jnp.mean(x * x, axis=-1, keepdims=True)
    o_ref[...] = (x * lax.rsqrt(ms + RMS_EPS) * g_ref[...]).astype(o_ref.dtype)


def _rmsnorm(x, g, *, bm=256):
    m, d = x.shape
    return pl.pallas_call(
        _rmsnorm_kernel,
        grid=(m // bm,),
        in_specs=[pl.BlockSpec((bm, d), lambda i: (i, 0)),
                  pl.BlockSpec((1, d), lambda i: (0, 0))],
        out_specs=pl.BlockSpec((bm, d), lambda i: (i, 0)),
        out_shape=jax.ShapeDtypeStruct((m, d), BF16),
        compiler_params=_params("parallel"),
        name="rmsnorm",
    )(x, g.reshape(1, d).astype(F32))


def _mm_kernel(a_ref, w_ref, o_ref):
    o_ref[...] = _dot(a_ref[...], w_ref[...]).astype(o_ref.dtype)


def _matmul(a, w, out_dtype, *, bm, bn, name):
    m, k = a.shape
    n = w.shape[1]
    return pl.pallas_call(
        _mm_kernel,
        grid=(m // bm, n // bn),
        in_specs=[pl.BlockSpec((bm, k), lambda i, j: (i, 0)),
                  pl.BlockSpec((k, bn), lambda i, j: (0, j))],
        out_specs=pl.BlockSpec((bm, bn), lambda i, j: (i, j)),
        out_shape=jax.ShapeDtypeStruct((m, n), out_dtype),
        compiler_params=_params("parallel", "arbitrary"),
        name=name,
    )(a, w)


def _qkv_kernel(a_ref, w_ref, s_ref, o_ref, *, qk_tiles):
    acc = _dot(a_ref[...], w_ref[...])
    is_qk = pl.program_id(1) < qk_tiles
    for h in range(acc.shape[1] // FOX_HEAD_DIM):
        cols = slice(h * FOX_HEAD_DIM, (h + 1) * FOX_HEAD_DIM)
        blk = acc[:, cols]
        ms = jnp.mean(blk * blk, axis=-1, keepdims=True)
        inv = jnp.where(is_qk, lax.rsqrt(ms + RMS_EPS), 1.0)
        o_ref[:, cols] = (blk * inv * s_ref[:, cols]).astype(o_ref.dtype)


def _qkv_proj(a, w_in, scale, *, bm, bn):
    m, k = a.shape
    n = scale.shape[1]
    qk_tiles = 2 * FOX_HEADS * FOX_HEAD_DIM // bn
    return pl.pallas_call(
        functools.partial(_qkv_kernel, qk_tiles=qk_tiles),
        grid=(m // bm, n // bn),
        in_specs=[pl.BlockSpec((bm, k), lambda i, j: (i, 0)),
                  pl.BlockSpec((k, bn), lambda i, j: (0, j)),
                  pl.BlockSpec((1, bn), lambda i, j: (0, j))],
        out_specs=pl.BlockSpec((bm, bn), lambda i, j: (i, j)),
        out_shape=jax.ShapeDtypeStruct((m, n), BF16),
        compiler_params=_params("parallel", "arbitrary"),
        name="qkv_proj",
    )(a, w_in, scale)


def _fgate_kernel(u_ref, wt_ref, b_ref, o_ref, carry_ref):
    @pl.when(pl.program_id(1) == 0)
    def _():
        carry_ref[...] = jnp.zeros_like(carry_ref)

    z = lax.dot_general(wt_ref[...], u_ref[...], (((1,), (1,)), ((), ())),
                        preferred_element_type=F32) + b_ref[...]
    x = jnp.minimum(z, 0.0) - jnp.log1p(jnp.exp(-jnp.abs(z)))
    bs = x.shape[1]
    lane = lax.broadcasted_iota(jnp.int32, x.shape, 1)
    shift = 1
    while shift < bs:
        x = x + jnp.where(lane >= shift, pltpu.roll(x, shift, 1), 0.0)
        shift *= 2
    x = x + carry_ref[:, 0:1]
    o_ref[...] = x * LOG2_E
    carry_ref[...] = jnp.broadcast_to(x[:, bs - 1:bs], carry_ref.shape)


def _forget_cumsum(u, wt, bias, batch, seq, *, bs=512):
    h, d = wt.shape
    ns = seq // bs
    return pl.pallas_call(
        _fgate_kernel,
        grid=(batch, ns),
        in_specs=[pl.BlockSpec((bs, d), lambda b, j: (b * ns + j, 0)),
                  pl.BlockSpec((h, d), lambda b, j: (0, 0)),
                  pl.BlockSpec((h, 1), lambda b, j: (0, 0))],
        out_specs=pl.BlockSpec((None, h, bs), lambda b, j: (b, 0, j)),
        out_shape=jax.ShapeDtypeStruct((batch, h, seq), F32),
        scratch_shapes=[pltpu.VMEM((h, 128), F32)],
        compiler_params=_params("parallel", "arbitrary"),
        name="forget_cumsum",
    )(u, wt, bias)


ATTN_HEADS_PER_STEP = 4


def _attn_kernel(q_ref, k_ref, v_ref, f_ref, o_ref, *, blk):
    i = pl.program_id(2)
    dh = FOX_HEAD_DIM
    heads = range(ATTN_HEADS_PER_STEP)
    qs = [q_ref[:, h * dh:(h + 1) * dh] for h in heads]

    def block(j, carry, masked):
        start = pl.multiple_of(j * blk, blk)
        out = []
        for h in heads:
            m, l, acc = carry[h]
            ks = k_ref[pl.ds(start, blk), h * dh:(h + 1) * dh]
            vs = v_ref[pl.ds(start, blk), h * dh:(h + 1) * dh]
            s = lax.dot_general(qs[h], ks, (((1,), (1,)), ((), ())), preferred_element_type=F32)
            s = s - f_ref[h, j]
            if masked:
                row = lax.broadcasted_iota(jnp.int32, s.shape, 0)
                col = lax.broadcasted_iota(jnp.int32, s.shape, 1)
                s = jnp.where(col <= row, s, MASK_VALUE)
            m_new = jnp.maximum(m, jnp.max(s, axis=-1, keepdims=True))
            alpha = jnp.exp2(m - m_new)
            p = jnp.exp2(s - m_new)
            l = alpha * l + jnp.sum(p, axis=-1, keepdims=True)
            acc = alpha * acc + _dot(p.astype(BF16), vs)
            out.append((m_new, l, acc))
        return tuple(out)

    init = tuple((jnp.full((blk, 1), MASK_VALUE, F32), jnp.zeros((blk, 1), F32),
                  jnp.zeros((blk, dh), F32)) for _ in heads)
    carry = lax.fori_loop(0, i, lambda j, c: block(j, c, False), init)
    carry = block(i, carry, True)
    for h in heads:
        _, l, acc = carry[h]
        o_ref[:, h * dh:(h + 1) * dh] = (acc / l).astype(o_ref.dtype)


def _attn_shift_kernel(c_ref, q_ref, k_ref, v_ref, f_ref, o_ref, acc_ref, shift_ref, p_ref, *, blk):
    i = pl.program_id(2)
    dh = FOX_HEAD_DIM
    heads = range(ATTN_HEADS_PER_STEP)
    qs = [q_ref[:, h * dh:(h + 1) * dh] for h in heads]
    ones = jnp.ones((blk, dh), BF16)

    def setup():
        for h in heads:
            col = jnp.broadcast_to(f_ref[h, i], (8, blk)).T[:, 0:1] - c_ref[0]
            shift_ref[h] = jnp.broadcast_to(col, (blk, dh))
        acc_ref[...] = jnp.zeros_like(acc_ref)

    def probabilities(h, j, masked):
        start = pl.multiple_of(j * blk, blk)
        ks = k_ref[pl.ds(start, blk), h * dh:(h + 1) * dh]
        s = lax.dot_general(qs[h], ks, (((1,), (1,)), ((), ())), preferred_element_type=F32)
        s = (s + jnp.concatenate([shift_ref[h]] * (blk // dh), axis=1)) - f_ref[h, j]
        if masked:
            row = lax.broadcasted_iota(jnp.int32, s.shape, 0)
            col = lax.broadcasted_iota(jnp.int32, s.shape, 1)
            s = jnp.where(col <= row, s, MASK_VALUE)
        return jnp.exp2(s).astype(BF16)

    def values(h, j):
        start = pl.multiple_of(j * blk, blk)
        return jnp.concatenate([v_ref[pl.ds(start, blk), h * dh:(h + 1) * dh], ones], axis=1)

    setup()

    @pl.when(i == 0)
    def _():
        p_ref[...] = jnp.zeros_like(p_ref)

    @pl.when(i > 0)
    def _():
        for h in heads:
            p_ref[h] = probabilities(h, 0, False)

    @pl.loop(1, i)
    def _(j):
        for h in heads:
            acc_ref[h] += _dot(p_ref[h], values(h, j - 1))
        for h in heads:
            p_ref[h] = probabilities(h, j, False)

    prev = jnp.maximum(i - 1, 0)
    for h in heads:
        acc_ref[h] += _dot(p_ref[h], values(h, prev)) + _dot(probabilities(h, i, True), values(h, i))
    for h in heads:
        acc = acc_ref[h]
        o_ref[:, h * dh:(h + 1) * dh] = (acc[:, :dh] / acc[:, dh:]).astype(o_ref.dtype)


ATTN_SHIFT_LIMIT = 56.0


def _attention(qkv, f_rows, qk_bound, *, blk=512):
    batch, seq, _ = qkv.shape
    nblk = seq // blk
    hps = ATTN_HEADS_PER_STEP
    wide = hps * FOX_HEAD_DIM
    third = FOX_HEADS // hps
    specs = [pl.BlockSpec((None, blk, wide), lambda b, h, i: (b, i, h)),
             pl.BlockSpec((None, seq, wide), lambda b, h, i: (b, 0, h + third)),
             pl.BlockSpec((None, seq, wide), lambda b, h, i: (b, 0, h + 2 * third)),
             pl.BlockSpec((None, hps, nblk, 1, blk), lambda b, h, i: (b, h, 0, 0, 0))]
    common = dict(
        grid=(batch, FOX_HEADS // hps, nblk),
        out_specs=pl.BlockSpec((None, blk, wide), lambda b, h, i: (b, i, h)),
        out_shape=jax.ShapeDtypeStruct((batch, seq, FOX_HEADS * FOX_HEAD_DIM), BF16),
        compiler_params=_params("parallel", "parallel", "arbitrary"))

    def shifted(qkv, f_rows, c):
        return pl.pallas_call(
            functools.partial(_attn_shift_kernel, blk=blk),
            in_specs=[pl.BlockSpec(memory_space=pltpu.SMEM)] + specs,
            scratch_shapes=[pltpu.VMEM((hps, blk, 2 * FOX_HEAD_DIM), F32),
                            pltpu.VMEM((hps, blk, FOX_HEAD_DIM), F32),
                            pltpu.VMEM((hps, blk, blk), BF16)],
            name="fox_attention_shift", **common)(c.reshape(1), qkv, qkv, qkv, f_rows)

    def online(qkv, f_rows, c):
        return pl.pallas_call(functools.partial(_attn_kernel, blk=blk), in_specs=specs,
                              name="fox_attention_online", **common)(qkv, qkv, qkv, f_rows)

    return lax.cond(qk_bound <= ATTN_SHIFT_LIMIT, shifted, online, qkv, f_rows, qk_bound)


S5_SLAB_GROUPS = 16
S5_SLAB = S5_SLAB_GROUPS * S5_GROUP
S5_SLAB_STATES = S5_SLAB_GROUPS * S5_STATE
S5_SLABS = S5_GROUPS // S5_SLAB_GROUPS


def _split_bf16(a):
    hi = a.astype(BF16)
    return hi, (a - hi.astype(F32)).astype(BF16)


def _dot_split(a, b_hi, b_lo):
    a_hi, a_lo = _split_bf16(a)
    return _dot(a_hi, b_hi) + _dot(a_hi, b_lo) + _dot(a_lo, b_hi)


def _lam_pow(k, lr, li, dt):
    mag = jnp.exp(k * (lr * dt))
    ang = k * (li * dt)
    return mag * jnp.cos(ang), mag * jnp.sin(ang)


def _s5_matrices_kernel(lam_ref, ls_ref, bdb_ref, bdc_ref, bp_ref, cp_ref, gr_ref, l16_ref,
                        bb_ref, chi_ref, clo_ref):
    t = pl.program_id(1)
    ns = S5_SLAB_STATES
    lr, li = lam_ref[0:1, :], lam_ref[1:2, :]
    dt = jnp.exp(ls_ref[...])

    @pl.when(t == 0)
    def _():
        lbr, lbi = _lam_pow(1.0, lr, li, dt)
        den = lr * lr + li * li
        nr = lbr - 1.0
        fr = (nr * lr + lbi * li) / den
        fi = (lbi * lr - nr * li) / den
        b_re, b_im = bdb_ref[0], bdb_ref[1]
        bb_ref[0] = fr * b_re - fi * b_im
        bb_ref[1] = fr * b_im + fi * b_re
        for part in range(2):
            chi_ref[part], clo_ref[part] = _split_bf16(bdc_ref[part])

    bb_re, bb_im = bb_ref[0], bb_ref[1]
    pr, pi = _lam_pow((S5_CHUNK - 1 - t).astype(F32), lr, li, dt)
    bp_re = bb_re * pr - bb_im * pi
    bp_im = bb_re * pi + bb_im * pr
    bp_ref[:, :ns] = bp_re.astype(BF16)
    bp_ref[:, ns:] = bp_im.astype(BF16)

    qr, qi = _lam_pow((t + 1).astype(F32), lr, li, dt)
    qr = jnp.broadcast_to(qr, (8, ns)).T[:, 0:1]
    qi = jnp.broadcast_to(qi, (8, ns)).T[:, 0:1]
    c_re, c_im = bdc_ref[0], bdc_ref[1]
    cp_ref[:ns, :] = (c_re * qr - c_im * qi).astype(BF16)
    cp_ref[ns:, :] = (-(c_re * qi + c_im * qr)).astype(BF16)

    gr_ref[...] = (_dot_split(bp_re, chi_ref[0], clo_ref[0])
                   - _dot_split(bp_im, chi_ref[1], clo_ref[1])).astype(BF16)

    l16r, l16i = _lam_pow(float(S5_CHUNK), lr, li, dt)
    l16_ref[0:1, :] = l16r
    l16_ref[1:2, :] = l16i


def _s5_matrices(lam, ls, bdb, bdc):
    t_len, w, ns = S5_CHUNK, S5_SLAB, S5_SLAB_STATES
    return pl.pallas_call(
        _s5_matrices_kernel,
        grid=(S5_SLABS, t_len),
        in_specs=[pl.BlockSpec((None, 2, ns), lambda s, t: (s, 0, 0)),
                  pl.BlockSpec((None, 1, ns), lambda s, t: (s, 0, 0)),
                  pl.BlockSpec((None, 2, w, ns), lambda s, t: (s, 0, 0, 0)),
                  pl.BlockSpec((None, 2, ns, w), lambda s, t: (s, 0, 0, 0))],
        out_specs=[pl.BlockSpec((None, w, 2 * ns), lambda s, t: (s, t, 0)),
                   pl.BlockSpec((None, None, 2 * ns, w), lambda s, t: (s, t, 0, 0)),
                   pl.BlockSpec((None, w, w), lambda s, t: (s, t, 0)),
                   pl.BlockSpec((None, 2, ns), lambda s, t: (s, 0, 0))],
        out_shape=[jax.ShapeDtypeStruct((S5_SLABS, t_len * w, 2 * ns), BF16),
                   jax.ShapeDtypeStruct((S5_SLABS, t_len, 2 * ns, w), BF16),
                   jax.ShapeDtypeStruct((S5_SLABS, t_len * w, w), BF16),
                   jax.ShapeDtypeStruct((S5_SLABS, 2, ns), F32)],
        scratch_shapes=[pltpu.VMEM((2, w, ns), F32), pltpu.VMEM((2, ns, w), BF16), pltpu.VMEM((2, ns, w), BF16)],
        compiler_params=_params("parallel", "arbitrary"),
        name="s5_matrices",
    )(lam, ls, bdb, bdc)


def _chunk_rows(x_ref):
    return jnp.concatenate([x_ref[:, t, :] for t in range(S5_CHUNK)], axis=1).astype(BF16)


def _s5_states_kernel(x_ref, bp_ref, s_ref):
    s_ref[...] = _dot(_chunk_rows(x_ref), bp_ref[...])


def _s5_states(x3, bp, *, rows):
    nc = x3.shape[0]
    t_len, w, ns = S5_CHUNK, S5_SLAB, S5_SLAB_STATES
    return pl.pallas_call(
        _s5_states_kernel,
        grid=(S5_SLABS, nc // rows),
        in_specs=[pl.BlockSpec((rows, t_len, w), lambda s, r: (r, 0, s)),
                  pl.BlockSpec((None, t_len * w, 2 * ns), lambda s, r: (s, 0, 0),
                               pipeline_mode=pl.Buffered(1))],
        out_specs=pl.BlockSpec((rows, 2 * ns), lambda s, r: (r, s)),
        out_shape=jax.ShapeDtypeStruct((nc, S5_SLABS * 2 * ns), F32),
        compiler_params=_params("parallel", "arbitrary"),
        name="s5_chunk_states",
    )(x3, bp)


def _s5_chunk_scan_kernel(s_ref, l16_ref, xp_ref, *, batch, chunks):
    ns = S5_SLAB_STATES
    ar, ai = l16_ref[0:1, :], l16_ref[1:2, :]

    def step(c, carry):
        out = []
        for b, (xr, xi) in enumerate(carry):
            row = pl.ds(b * chunks + c, 1)
            xp_ref[row, :ns] = xr
            xp_ref[row, ns:] = xi
            sr = s_ref[row, :ns]
            si = s_ref[row, ns:]
            out.append((ar * xr - ai * xi + sr, ar * xi + ai * xr + si))
        return tuple(out)

    zero = jnp.zeros((1, ns), F32)
    lax.fori_loop(0, chunks, step, ((zero, zero),) * batch)


def _s5_chunk_scan(s, l16, *, batch):
    nc = s.shape[0]
    ns = S5_SLAB_STATES
    return pl.pallas_call(
        functools.partial(_s5_chunk_scan_kernel, batch=batch, chunks=nc // batch),
        grid=(S5_SLABS,),
        in_specs=[pl.BlockSpec((nc, 2 * ns), lambda s_: (0, s_)),
                  pl.BlockSpec((None, 2, ns), lambda s_: (s_, 0, 0))],
        out_specs=pl.BlockSpec((nc, 2 * ns), lambda s_: (0, s_)),
        out_shape=jax.ShapeDtypeStruct(s.shape, F32),
        compiler_params=_params("parallel"),
        name="s5_chunk_scan",
    )(s, l16)


def _s5_output_kernel(x_ref, xp_ref, gr_ref, cp_ref, d_ref, y_ref):
    t_len, w = S5_CHUNK, S5_SLAB
    xc = _chunk_rows(x_ref)
    xp = xp_ref[...].astype(BF16)
    d = d_ref[...]
    for t in range(t_len):
        y = _dot(xc[:, :(t + 1) * w], gr_ref[(t_len - 1 - t) * w:, :])
        y = y + _dot(xp, cp_ref[t])
        y_ref[:, t, :] = y + d * x_ref[:, t, :]


def _s5_output(x3, xp, gr, cp, d, *, rows):
    nc = x3.shape[0]
    t_len, w, ns = S5_CHUNK, S5_SLAB, S5_SLAB_STATES
    once = dict(pipeline_mode=pl.Buffered(1))
    return pl.pallas_call(
        _s5_output_kernel,
        grid=(S5_SLABS, nc // rows),
        in_specs=[pl.BlockSpec((rows, t_len, w), lambda s, r: (r, 0, s)),
                  pl.BlockSpec((rows, 2 * ns), lambda s, r: (r, s)),
                  pl.BlockSpec((None, t_len * w, w), lambda s, r: (s, 0, 0), **once),
                  pl.BlockSpec((None, t_len, 2 * ns, w), lambda s, r: (s, 0, 0, 0), **once),
                  pl.BlockSpec((None, 1, w), lambda s, r: (s, 0, 0))],
        out_specs=pl.BlockSpec((rows, t_len, w), lambda s, r: (r, 0, s)),
        out_shape=jax.ShapeDtypeStruct(x3.shape, F32),
        compiler_params=_params("parallel", "arbitrary"),
        name="s5_chunk_output",
    )(x3, xp, gr, cp, d)


def _s5(s5_in, lam_re, lam_im, log_step, b_re, b_im, c_re, c_im, d, *, batch):
    m, width = s5_in.shape
    sl, sg, p, grp = S5_SLABS, S5_SLAB_GROUPS, S5_STATE, S5_GROUP
    ns = S5_SLAB_STATES
    lam = jnp.stack([lam_re, lam_im], axis=0).astype(F32).reshape(2, sl, ns).transpose(1, 0, 2)
    ls = jnp.repeat(log_step.astype(F32), p).reshape(sl, 1, ns)
    eye = jnp.eye(sg, dtype=F32)
    b = jnp.stack([b_re, b_im], axis=0).astype(F32).reshape(2, sl, sg, p, grp)
    bdb = jnp.einsum('rsgpi,gh->srgihp', b, eye).reshape(sl, 2, sg * grp, ns)
    c = jnp.stack([c_re, c_im], axis=0).astype(F32).reshape(2, sl, sg, grp, p)
    bdc = jnp.einsum('rsgop,gh->srhpgo', c, eye).reshape(sl, 2, ns, sg * grp)
    bp, cp, gr, l16 = _s5_matrices(lam, ls, bdb, bdc)
    x3 = s5_in.reshape(m // S5_CHUNK, S5_CHUNK, width)
    states = _s5_states(x3, bp, rows=512)
    xprev = _s5_chunk_scan(states, l16, batch=batch)
    y3 = _s5_output(x3, xprev, gr, cp, d.astype(F32).reshape(sl, 1, sg * grp), rows=256)
    return y3.reshape(m, width)


def _glu_kernel(y_ref, w_ref, b_ref, o_ref):
    g = jax.nn.gelu(y_ref[...])
    o_ref[...] = (g * jax.nn.sigmoid(_dot(g.astype(BF16), w_ref[...]) + b_ref[...])).astype(o_ref.dtype)


def _glu(y, w, b, *, bm=512):
    m, n = y.shape
    return pl.pallas_call(
        _glu_kernel,
        grid=(m // bm,),
        in_specs=[pl.BlockSpec((bm, n), lambda i: (i, 0)),
                  pl.BlockSpec((n, n), lambda i: (0, 0)),
                  pl.BlockSpec((1, n), lambda i: (0, 0))],
        out_specs=pl.BlockSpec((bm, n), lambda i: (i, 0)),
        out_shape=jax.ShapeDtypeStruct((m, n), BF16),
        compiler_params=_params("parallel"),
        name="s5_glu",
    )(y, w, b)


def _merge_kernel(u_ref, a_ref, s_ref, wgf_ref, wgs_ref, wpf_ref, wps_ref, bgf_ref, bgs_ref, o_ref):
    u = u_ref[...]
    gate_fox = jax.nn.sigmoid(_dot(u, wgf_ref[...]) + bgf_ref[...])
    gate_s5 = jax.nn.sigmoid(_dot(u, wgs_ref[...]) + bgs_ref[...])
    o_ref[...] = (gate_fox * _dot(a_ref[...], wpf_ref[...])
                  + gate_s5 * _dot(s_ref[...], wps_ref[...])).astype(o_ref.dtype)


def _merge(u, attn, ssm, wgf, wgs, wpf, wps, bgf, bgs, *, bm=512, bn=512):
    m, d = u.shape
    n = wgf.shape[1]
    act = lambda a: pl.BlockSpec((bm, a.shape[1]), lambda i, j: (i, 0))
    wgt = lambda w: pl.BlockSpec((w.shape[0], bn), lambda i, j: (0, j))
    return pl.pallas_call(
        _merge_kernel,
        grid=(m // bm, n // bn),
        in_specs=[act(u), act(attn), act(ssm), wgt(wgf), wgt(wgs), wgt(wpf), wgt(wps),
                  pl.BlockSpec((1, bn), lambda i, j: (0, j)),
                  pl.BlockSpec((1, bn), lambda i, j: (0, j))],
        out_specs=pl.BlockSpec((bm, bn), lambda i, j: (i, j)),
        out_shape=jax.ShapeDtypeStruct((m, n), BF16),
        compiler_params=_params("parallel", "arbitrary"),
        name="gated_merge",
    )(u, attn, ssm, wgf, wgs, wpf, wps, bgf, bgs)


NORM_LANES = 128


def _out_proj_kernel(a_ref, w_ref, x_ref, g_ref, h_ref, hg_ref, r_ref, ssq_ref, *, d_model):
    j = pl.program_id(1)

    @pl.when(j == 0)
    def _():
        ssq_ref[...] = jnp.zeros_like(ssq_ref)

    h = x_ref[...] + _dot(a_ref[...], w_ref[...])
    h_ref[...] = h
    hg_ref[...] = (h * g_ref[...]).astype(hg_ref.dtype)
    ssq_ref[...] += jnp.sum(h * h, axis=-1, keepdims=True)

    @pl.when(j == pl.num_programs(1) - 1)
    def _():
        r_ref[...] = jnp.broadcast_to(lax.rsqrt(ssq_ref[...] * (1.0 / d_model) + RMS_EPS), r_ref.shape)


def _out_proj(a, w, x, g, *, bm, bn):
    m, k = a.shape
    n = w.shape[1]
    return pl.pallas_call(
        functools.partial(_out_proj_kernel, d_model=n),
        grid=(m // bm, n // bn),
        in_specs=[pl.BlockSpec((bm, k), lambda i, j: (i, 0)),
                  pl.BlockSpec((k, bn), lambda i, j: (0, j)),
                  pl.BlockSpec((bm, bn), lambda i, j: (i, j)),
                  pl.BlockSpec((1, bn), lambda i, j: (0, j))],
        out_specs=[pl.BlockSpec((bm, bn), lambda i, j: (i, j)),
                   pl.BlockSpec((bm, bn), lambda i, j: (i, j)),
                   pl.BlockSpec((bm, NORM_LANES), lambda i, j: (i, 0))],
        out_shape=[jax.ShapeDtypeStruct((m, n), F32),
                   jax.ShapeDtypeStruct((m, n), BF16),
                   jax.ShapeDtypeStruct((m, NORM_LANES), F32)],
        scratch_shapes=[pltpu.VMEM((bm, 1), F32)],
        compiler_params=_params("parallel", "arbitrary"),
        name="out_proj_residual",
    )(a, w, x, g)


def _round_weights_once(w_ref, wb_ref):
    @pl.when(pl.program_id(1) == 0)
    def _():
        wb_ref[...] = w_ref[...].astype(BF16)


def _swiglu_kernel(a_ref, r_ref, wg_ref, wu_ref, o_ref, wgb_ref, wub_ref):
    _round_weights_once(wg_ref, wgb_ref)
    _round_weights_once(wu_ref, wub_ref)
    a = a_ref[...]
    r = jnp.concatenate([r_ref[...]] * (o_ref.shape[1] // NORM_LANES), axis=1)
    gate = _dot(a, wgb_ref[...]) * r
    up = _dot(a, wub_ref[...]) * r
    o_ref[...] = (jax.nn.silu(gate) * up).astype(o_ref.dtype)


def _swiglu_up(a, r, w_gate_up, d_ff, *, bm, bn):
    m, k = a.shape
    nj = d_ff // bn
    return pl.pallas_call(
        _swiglu_kernel,
        grid=(nj, m // bm),
        in_specs=[pl.BlockSpec((bm, k), lambda j, i: (i, 0)),
                  pl.BlockSpec((bm, NORM_LANES), lambda j, i: (i, 0)),
                  pl.BlockSpec((k, bn), lambda j, i: (0, j)),
                  pl.BlockSpec((k, bn), lambda j, i: (0, j + nj))],
        out_specs=pl.BlockSpec((bm, bn), lambda j, i: (i, j)),
        out_shape=jax.ShapeDtypeStruct((m, d_ff), BF16),
        scratch_shapes=[pltpu.VMEM((k, bn), BF16)] * 2,
        compiler_params=_params("parallel", "arbitrary"),
        name="swiglu_up",
    )(a, r, w_gate_up, w_gate_up)


def _down_kernel(a_ref, w_ref, r_ref, o_ref):
    o_ref[...] = r_ref[...] + _dot(a_ref[...], w_ref[...])


def _down_residual(a, w, res, *, bm, bn):
    m, k = a.shape
    n = w.shape[1]
    return pl.pallas_call(
        _down_kernel,
        grid=(m // bm, n // bn),
        in_specs=[pl.BlockSpec((bm, k), lambda i, j: (i, 0)),
                  pl.BlockSpec((k, bn), lambda i, j: (0, j)),
                  pl.BlockSpec((bm, bn), lambda i, j: (i, j))],
        out_specs=pl.BlockSpec((bm, bn), lambda i, j: (i, j)),
        out_shape=jax.ShapeDtypeStruct((m, n), F32),
        compiler_params=_params("parallel", "arbitrary"),
        name="down_proj_residual",
    )(a, w, res)


def _layer(x, batch, seq, g_mix, w_in, b_fgate, b_gates, q_norm, k_norm,
           lam_re, lam_im, log_step, b_re, b_im, c_re, c_im, s5_d,
           w_glu, b_glu, w_proj_fox, w_proj_s5, w_out, g_ffn, w_gate_up, w_down):
    m, d_model = x.shape
    fox_w = FOX_HEADS * FOX_HEAD_DIM
    s5_w = S5_GROUP * S5_GROUPS
    col_k, col_v = fox_w, 2 * fox_w
    col_f = 3 * fox_w
    col_s5 = col_f + FOX_HEADS
    col_g = col_s5 + s5_w
    d_ff = w_down.shape[0]
    w_in_b = w_in.astype(BF16)
    w_f, w_s5 = w_in_b[:, col_f:col_s5], w_in_b[:, col_s5:col_g]
    w_gate_fox, w_gate_s5 = w_in_b[:, col_g:col_g + d_model], w_in_b[:, col_g + d_model:col_g + 2 * d_model]

    u = _rmsnorm(x, g_mix)

    head_scale = jnp.concatenate([
        jnp.tile(q_norm.astype(F32), FOX_HEADS) * (LOG2_E / math.sqrt(FOX_HEAD_DIM)),
        jnp.tile(k_norm.astype(F32), FOX_HEADS),
        jnp.ones((fox_w,), F32)]).reshape(1, 3 * fox_w)
    qkv = _qkv_proj(u, w_in_b, head_scale, bm=1024, bn=1024)
    s5_in = _matmul(u, w_s5, F32, bm=1024, bn=1024, name="s5_in_proj")

    blk = 512
    f_t = _forget_cumsum(u, w_f.T, b_fgate.astype(F32).reshape(FOX_HEADS, 1),
                         batch, seq)
    f_rows = f_t.reshape(batch, FOX_HEADS, seq // blk, 1, blk)
    qk_bound = (1.02 * LOG2_E * math.sqrt(FOX_HEAD_DIM)
                * jnp.max(jnp.abs(q_norm.astype(F32))) * jnp.max(jnp.abs(k_norm.astype(F32))))
    attn = _attention(qkv.reshape(batch, seq, 3 * fox_w), f_rows, qk_bound, blk=blk).reshape(m, fox_w)

    y5 = _s5(s5_in, lam_re, lam_im, log_step, b_re, b_im, c_re, c_im, s5_d, batch=batch)
    ssm = _glu(y5, w_glu.astype(BF16), b_glu.astype(F32).reshape(1, s5_w))

    b_gates = b_gates.astype(F32).reshape(1, 2 * d_model)
    merged = _merge(u, attn, ssm, w_gate_fox, w_gate_s5,
                    w_proj_fox.astype(BF16), w_proj_s5.astype(BF16),
                    b_gates[:, :d_model], b_gates[:, d_model:])
    h, hg, h_inv_rms = _out_proj(merged, w_out.astype(BF16), x, g_ffn.astype(F32).reshape(1, d_model),
                                 bm=1024, bn=512)
    act = _swiglu_up(hg, h_inv_rms, w_gate_up, d_ff, bm=1024, bn=256)
    return _down_residual(act, w_down.astype(BF16), h, bm=512, bn=512)


def kernel(x, g_mix, w_in, b_fgate, b_gates, q_norm, k_norm, s5_lambda_re, s5_lambda_im, s5_log_step, s5_b_re, s5_b_im, s5_c_re, s5_c_im, s5_d, w_glu, b_glu, w_proj_fox, w_proj_s5, w_out, g_ffn, w_gate_up, w_down):
    batch, seq, d_model = x.shape
    h = x.reshape(batch * seq, d_model)
    for l in range(g_mix.shape[0]):
        h = _layer(h, batch, seq, g_mix[l], w_in[l], b_fgate[l], b_gates[l], q_norm[l], k_norm[l],
                   s5_lambda_re[l], s5_lambda_im[l], s5_log_step[l], s5_b_re[l], s5_b_im[l],
                   s5_c_re[l], s5_c_im[l], s5_d[l], w_glu[l], b_glu[l],
                   w_proj_fox[l], w_proj_s5[l], w_out[l], g_ffn[l], w_gate_up[l], w_down[l])
    return h.reshape(batch, seq, d_model)
```

```python
import functools
import math

import jax
import jax.numpy as jnp
from jax import lax
from jax.experimental import pallas as pl
from jax.experimental.pallas import tpu as pltpu

F32 = jnp.float32
BF16 = jnp.bfloat16

FOX_HEADS = 16
FOX_HEAD_DIM = 128
S5_GROUP = 16
S5_GROUPS = 64
S5_STATE = 64
S5_CHUNK = 16
RMS_EPS = 1e-6
MASK_VALUE = -1e30
LOG2_E = math.log2(math.e)

VMEM_LIMIT_BYTES = 56 * 1024 * 1024


def _params(*semantics):
    return pltpu.CompilerParams(dimension_semantics=semantics,
                                vmem_limit_bytes=VMEM_LIMIT_BYTES)


def _dot(a, b):
    return jnp.dot(a, b, preferred_element_type=F32)


def _rmsnorm_kernel(x_ref, g_ref, o_ref):
    x = x_ref[...]
    ms = jnp.mean(x * x, axis=-1, keepdims=True)
    o_ref[...] = (x * lax.rsqrt(ms + RMS_EPS) * g_ref[...]).astype(o_ref.dtype)


def _rmsnorm(x, g, *, bm=256):
    m, d = x.shape
    return pl.pallas_call(
        _rmsnorm_kernel,
        grid=(m // bm,),
        in_specs=[pl.BlockSpec((bm, d), lambda i: (i, 0)),
                  pl.BlockSpec((1, d), lambda i: (0, 0))],
        out_specs=pl.BlockSpec((bm, d), lambda i: (i, 0)),
        out_shape=jax.ShapeDtypeStruct((m, d), BF16),
        compiler_params=_params("parallel"),
        name="rmsnorm",
    )(x, g.reshape(1, d).astype(F32))


def _qkv_kernel(a_ref, w_ref, s_ref, o_ref, *, qk_tiles):
    acc = _dot(a_ref[...], w_ref[...])
    is_qk = pl.program_id(1) < qk_tiles
    for h in range(acc.shape[1] // FOX_HEAD_DIM):
        cols = slice(h * FOX_HEAD_DIM, (h + 1) * FOX_HEAD_DIM)
        blk = acc[:, cols]
        ms = jnp.mean(blk * blk, axis=-1, keepdims=True)
        inv = jnp.where(is_qk, lax.rsqrt(ms + RMS_EPS), 1.0)
        o_ref[:, cols] = (blk * inv * s_ref[:, cols]).astype(o_ref.dtype)


def _qkv_proj(a, w_in, scale, *, bm, bn):
    m, k = a.shape
    n = scale.shape[1]
    qk_tiles = 2 * FOX_HEADS * FOX_HEAD_DIM // bn
    return pl.pallas_call(
        functools.partial(_qkv_kernel, qk_tiles=qk_tiles),
        grid=(m // bm, n // bn),
        in_specs=[pl.BlockSpec((bm, k), lambda i, j: (i, 0)),
                  pl.BlockSpec((k, bn), lambda i, j: (0, j)),
                  pl.BlockSpec((1, bn), lambda i, j: (0, j))],
        out_specs=pl.BlockSpec((bm, bn), lambda i, j: (i, j)),
        out_shape=jax.ShapeDtypeStruct((m, n), BF16),
        compiler_params=_params("parallel", "arbitrary"),
        name="qkv_proj",
    )(a, w_in, scale)


def _s5_fgate_kernel(u_ref, ws_ref, wt_ref, b_ref, s_ref, f_ref, carry_ref):
    @pl.when(pl.program_id(1) == 0)
    def _():
        carry_ref[...] = jnp.zeros_like(carry_ref)

    u = u_ref[...]
    s_ref[...] = _dot(u, ws_ref[...])
    z = lax.dot_general(wt_ref[...], u, (((1,), (1,)), ((), ())),
                        preferred_element_type=F32) + b_ref[...]
    x = jnp.minimum(z, 0.0) - jnp.log1p(jnp.exp(-jnp.abs(z)))
    bs = x.shape[1]
    lane = lax.broadcasted_iota(jnp.int32, x.shape, 1)
    shift = 1
    while shift < bs:
        x = x + jnp.where(lane >= shift, pltpu.roll(x, shift, 1), 0.0)
        shift *= 2
    x = x + carry_ref[:, 0:1]
    f_ref[...] = x * LOG2_E
    carry_ref[...] = jnp.broadcast_to(x[:, bs - 1:bs], carry_ref.shape)


def _s5_fgate_proj(u, w_s5, wt, bias, batch, seq, *, bs=1024):
    h, d = wt.shape
    n = w_s5.shape[1]
    ns = seq // bs
    return pl.pallas_call(
        _s5_fgate_kernel,
        grid=(batch, ns),
        in_specs=[pl.BlockSpec((bs, d), lambda b, j: (b * ns + j, 0)),
                  pl.BlockSpec((d, n), lambda b, j: (0, 0)),
                  pl.BlockSpec((h, d), lambda b, j: (0, 0)),
                  pl.BlockSpec((h, 1), lambda b, j: (0, 0))],
        out_specs=[pl.BlockSpec((bs, n), lambda b, j: (b * ns + j, 0)),
                   pl.BlockSpec((None, h, bs), lambda b, j: (b, 0, j))],
        out_shape=[jax.ShapeDtypeStruct((batch * seq, n), F32),
                   jax.ShapeDtypeStruct((batch, h, seq), F32)],
        scratch_shapes=[pltpu.VMEM((h, 128), F32)],
        compiler_params=_params("parallel", "arbitrary"),
        name="s5_in_fgate_proj",
    )(u, w_s5, wt, bias)


ATTN_HEADS_PER_STEP = 4


def _attn_kernel(q_ref, k_ref, v_ref, f_ref, o_ref, *, blk):
    i = pl.program_id(2)
    dh = FOX_HEAD_DIM
    heads = range(ATTN_HEADS_PER_STEP)
    qs = [q_ref[:, h * dh:(h + 1) * dh] for h in heads]

    def block(j, carry, masked):
        start = pl.multiple_of(j * blk, blk)
        out = []
        for h in heads:
            m, l, acc = carry[h]
            ks = k_ref[pl.ds(start, blk), h * dh:(h + 1) * dh]
            vs = v_ref[pl.ds(start, blk), h * dh:(h + 1) * dh]
            s = lax.dot_general(qs[h], ks, (((1,), (1,)), ((), ())), preferred_element_type=F32)
            s = s - f_ref[h, j]
            if masked:
                row = lax.broadcasted_iota(jnp.int32, s.shape, 0)
                col = lax.broadcasted_iota(jnp.int32, s.shape, 1)
                s = jnp.where(col <= row, s, MASK_VALUE)
            m_new = jnp.maximum(m, jnp.max(s, axis=-1, keepdims=True))
            alpha = jnp.exp2(m - m_new)
            p = jnp.exp2(s - m_new)
            l = alpha * l + jnp.sum(p, axis=-1, keepdims=True)
            acc = alpha * acc + _dot(p.astype(BF16), vs)
            out.append((m_new, l, acc))
        return tuple(out)

    init = tuple((jnp.full((blk, 1), MASK_VALUE, F32), jnp.zeros((blk, 1), F32),
                  jnp.zeros((blk, dh), F32)) for _ in heads)
    carry = lax.fori_loop(0, i, lambda j, c: block(j, c, False), init)
    carry = block(i, carry, True)
    for h in heads:
        _, l, acc = carry[h]
        o_ref[:, h * dh:(h + 1) * dh] = (acc / l).astype(o_ref.dtype)


def _attn_shift_kernel(c_ref, q_ref, k_ref, v_ref, f_ref, o_ref, acc_ref, shift_ref, p_ref, *, blk):
    i = pl.program_id(2)
    dh = FOX_HEAD_DIM
    heads = range(ATTN_HEADS_PER_STEP)
    qs = [q_ref[:, h * dh:(h + 1) * dh] for h in heads]
    ones = jnp.ones((blk, dh), BF16)

    def setup():
        for h in heads:
            col = jnp.broadcast_to(f_ref[h, i], (8, blk)).T[:, 0:1] - c_ref[0]
            shift_ref[h] = jnp.broadcast_to(col, (blk, dh))
        acc_ref[...] = jnp.zeros_like(acc_ref)

    def probabilities(h, j, masked):
        start = pl.multiple_of(j * blk, blk)
        ks = k_ref[pl.ds(start, blk), h * dh:(h + 1) * dh]
        s = lax.dot_general(qs[h], ks, (((1,), (1,)), ((), ())), preferred_element_type=F32)
        s = (s + jnp.concatenate([shift_ref[h]] * (blk // dh), axis=1)) - f_ref[h, j]
        if masked:
            row = lax.broadcasted_iota(jnp.int32, s.shape, 0)
            col = lax.broadcasted_iota(jnp.int32, s.shape, 1)
            s = jnp.where(col <= row, s, MASK_VALUE)
        return jnp.exp2(s).astype(BF16)

    def values(h, j):
        start = pl.multiple_of(j * blk, blk)
        return jnp.concatenate([v_ref[pl.ds(start, blk), h * dh:(h + 1) * dh], ones], axis=1)

    setup()

    @pl.when(i == 0)
    def _():
        p_ref[...] = jnp.zeros_like(p_ref)

    @pl.when(i > 0)
    def _():
        for h in heads:
            p_ref[h] = probabilities(h, 0, False)

    @pl.loop(1, i)
    def _(j):
        for h in heads:
            acc_ref[h] += _dot(p_ref[h], values(h, j - 1))
        for h in heads:
            p_ref[h] = probabilities(h, j, False)

    prev = jnp.maximum(i - 1, 0)
    for h in heads:
        acc_ref[h] += _dot(p_ref[h], values(h, prev)) + _dot(probabilities(h, i, True), values(h, i))
    for h in heads:
        acc = acc_ref[h]
        o_ref[:, h * dh:(h + 1) * dh] = (acc[:, :dh] / acc[:, dh:]).astype(o_ref.dtype)


ATTN_SHIFT_LIMIT = 56.0


def _attention(qkv, f_rows, qk_bound, *, blk=512):
    batch, seq, _ = qkv.shape
    nblk = seq // blk
    hps = ATTN_HEADS_PER_STEP
    wide = hps * FOX_HEAD_DIM
    third = FOX_HEADS // hps
    specs = [pl.BlockSpec((None, blk, wide), lambda b, h, i: (b, i, h)),
             pl.BlockSpec((None, seq, wide), lambda b, h, i: (b, 0, h + third)),
             pl.BlockSpec((None, seq, wide), lambda b, h, i: (b, 0, h + 2 * third)),
             pl.BlockSpec((None, hps, nblk, 1, blk), lambda b, h, i: (b, h, 0, 0, 0))]
    common = dict(
        grid=(batch, FOX_HEADS // hps, nblk),
        out_specs=pl.BlockSpec((None, blk, wide), lambda b, h, i: (b, i, h)),
        out_shape=jax.ShapeDtypeStruct((batch, seq, FOX_HEADS * FOX_HEAD_DIM), BF16),
        compiler_params=_params("parallel", "parallel", "arbitrary"))

    def shifted(qkv, f_rows, c):
        return pl.pallas_call(
            functools.partial(_attn_shift_kernel, blk=blk),
            in_specs=[pl.BlockSpec(memory_space=pltpu.SMEM)] + specs,
            scratch_shapes=[pltpu.VMEM((hps, blk, 2 * FOX_HEAD_DIM), F32),
                            pltpu.VMEM((hps, blk, FOX_HEAD_DIM), F32),
                            pltpu.VMEM((hps, blk, blk), BF16)],
            name="fox_attention_shift", **common)(c.reshape(1), qkv, qkv, qkv, f_rows)

    def online(qkv, f_rows, c):
        return pl.pallas_call(functools.partial(_attn_kernel, blk=blk), in_specs=specs,
                              name="fox_attention_online", **common)(qkv, qkv, qkv, f_rows)

    return lax.cond(qk_bound <= ATTN_SHIFT_LIMIT, shifted, online, qkv, f_rows, qk_bound)


S5_SLAB_GROUPS = 16
S5_SLAB = S5_SLAB_GROUPS * S5_GROUP
S5_SLAB_STATES = S5_SLAB_GROUPS * S5_STATE
S5_SLABS = S5_GROUPS // S5_SLAB_GROUPS


def _split_bf16(a):
    hi = a.astype(BF16)
    return hi, (a - hi.astype(F32)).astype(BF16)


def _dot_split(a, b_hi, b_lo):
    a_hi, a_lo = _split_bf16(a)
    return _dot(a_hi, b_hi) + _dot(a_hi, b_lo) + _dot(a_lo, b_hi)


def _lam_pow(k, lr, li, dt):
    mag = jnp.exp(k * (lr * dt))
    ang = k * (li * dt)
    return mag * jnp.cos(ang), mag * jnp.sin(ang)


def _s5_matrices_kernel(lam_ref, ls_ref, bdb_ref, bdc_ref, bp_ref, cp_ref, gr_ref, l16_ref,
                        bb_ref, chi_ref, clo_ref):
    t = pl.program_id(1)
    ns = S5_SLAB_STATES
    lr, li = lam_ref[0:1, :], lam_ref[1:2, :]
    dt = jnp.exp(ls_ref[...])

    @pl.when(t == 0)
    def _():
        lbr, lbi = _lam_pow(1.0, lr, li, dt)
        den = lr * lr + li * li
        nr = lbr - 1.0
        fr = (nr * lr + lbi * li) / den
        fi = (lbi * lr - nr * li) / den
        b_re, b_im = bdb_ref[0], bdb_ref[1]
        bb_ref[0] = fr * b_re - fi * b_im
        bb_ref[1] = fr * b_im + fi * b_re
        for part in range(2):
            chi_ref[part], clo_ref[part] = _split_bf16(bdc_ref[part])

    bb_re, bb_im = bb_ref[0], bb_ref[1]
    pr, pi = _lam_pow((S5_CHUNK - 1 - t).astype(F32), lr, li, dt)
    bp_re = bb_re * pr - bb_im * pi
    bp_im = bb_re * pi + bb_im * pr
    bp_ref[:, :ns] = bp_re.astype(BF16)
    bp_ref[:, ns:] = bp_im.astype(BF16)

    qr, qi = _lam_pow((t + 1).astype(F32), lr, li, dt)
    qr = jnp.broadcast_to(qr, (8, ns)).T[:, 0:1]
    qi = jnp.broadcast_to(qi, (8, ns)).T[:, 0:1]
    c_re, c_im = bdc_ref[0], bdc_ref[1]
    cp_ref[:ns, :] = (c_re * qr - c_im * qi).astype(BF16)
    cp_ref[ns:, :] = (-(c_re * qi + c_im * qr)).astype(BF16)

    gr_ref[...] = (_dot_split(bp_re, chi_ref[0], clo_ref[0])
                   - _dot_split(bp_im, chi_ref[1], clo_ref[1])).astype(BF16)

    l16r, l16i = _lam_pow(float(S5_CHUNK), lr, li, dt)
    l16_ref[0:1, :] = l16r
    l16_ref[1:2, :] = l16i


def _s5_matrices(lam, ls, bdb, bdc):
    t_len, w, ns = S5_CHUNK, S5_SLAB, S5_SLAB_STATES
    return pl.pallas_call(
        _s5_matrices_kernel,
        grid=(S5_SLABS, t_len),
        in_specs=[pl.BlockSpec((None, 2, ns), lambda s, t: (s, 0, 0)),
                  pl.BlockSpec((None, 1, ns), lambda s, t: (s, 0, 0)),
                  pl.BlockSpec((None, 2, w, ns), lambda s, t: (s, 0, 0, 0)),
                  pl.BlockSpec((None, 2, ns, w), lambda s, t: (s, 0, 0, 0))],
        out_specs=[pl.BlockSpec((None, w, 2 * ns), lambda s, t: (s, t, 0)),
                   pl.BlockSpec((None, None, 2 * ns, w), lambda s, t: (s, t, 0, 0)),
                   pl.BlockSpec((None, w, w), lambda s, t: (s, t, 0)),
                   pl.BlockSpec((None, 2, ns), lambda s, t: (s, 0, 0))],
        out_shape=[jax.ShapeDtypeStruct((S5_SLABS, t_len * w, 2 * ns), BF16),
                   jax.ShapeDtypeStruct((S5_SLABS, t_len, 2 * ns, w), BF16),
                   jax.ShapeDtypeStruct((S5_SLABS, t_len * w, w), BF16),
                   jax.ShapeDtypeStruct((S5_SLABS, 2, ns), F32)],
        scratch_shapes=[pltpu.VMEM((2, w, ns), F32), pltpu.VMEM((2, ns, w), BF16), pltpu.VMEM((2, ns, w), BF16)],
        compiler_params=_params("parallel", "arbitrary"),
        name="s5_matrices",
    )(lam, ls, bdb, bdc)


def _chunk_rows(x_ref):
    return jnp.concatenate([x_ref[:, t, :] for t in range(S5_CHUNK)], axis=1).astype(BF16)


def _s5_states_kernel(x_ref, bp_ref, s_ref):
    s_ref[...] = _dot(_chunk_rows(x_ref), bp_ref[...])


def _s5_states(x3, bp, *, rows):
    nc = x3.shape[0]
    t_len, w, ns = S5_CHUNK, S5_SLAB, S5_SLAB_STATES
    return pl.pallas_call(
        _s5_states_kernel,
        grid=(S5_SLABS, nc // rows),
        in_specs=[pl.BlockSpec((rows, t_len, w), lambda s, r: (r, 0, s)),
                  pl.BlockSpec((None, t_len * w, 2 * ns), lambda s, r: (s, 0, 0),
                               pipeline_mode=pl.Buffered(1))],
        out_specs=pl.BlockSpec((rows, 2 * ns), lambda s, r: (r, s)),
        out_shape=jax.ShapeDtypeStruct((nc, S5_SLABS * 2 * ns), F32),
        compiler_params=_params("parallel", "arbitrary"),
        name="s5_chunk_states",
    )(x3, bp)


def _s5_chunk_scan_kernel(s_ref, l16_ref, xp_ref, *, batch, chunks):
    ns = S5_SLAB_STATES
    ar, ai = l16_ref[0:1, :], l16_ref[1:2, :]

    def step(c, carry):
        out = []
        for b, (xr, xi) in enumerate(carry):
            row = pl.ds(b * chunks + c, 1)
            xp_ref[row, :ns] = xr
            xp_ref[row, ns:] = xi
            sr = s_ref[row, :ns]
            si = s_ref[row, ns:]
            out.append((ar * xr - ai * xi + sr, ar * xi + ai * xr + si))
        return tuple(out)

    zero = jnp.zeros((1, ns), F32)
    lax.fori_loop(0, chunks, step, ((zero, zero),) * batch)


def _s5_chunk_scan(s, l16, *, batch):
    nc = s.shape[0]
    ns = S5_SLAB_STATES
    return pl.pallas_call(
        functools.partial(_s5_chunk_scan_kernel, batch=batch, chunks=nc // batch),
        grid=(S5_SLABS,),
        in_specs=[pl.BlockSpec((nc, 2 * ns), lambda s_: (0, s_)),
                  pl.BlockSpec((None, 2, ns), lambda s_: (s_, 0, 0))],
        out_specs=pl.BlockSpec((nc, 2 * ns), lambda s_: (0, s_)),
        out_shape=jax.ShapeDtypeStruct(s.shape, F32),
        compiler_params=_params("parallel"),
        name="s5_chunk_scan",
    )(s, l16)


def _s5_output_kernel(x_ref, xp_ref, gr_ref, cp_ref, d_ref, y_ref):
    t_len, w = S5_CHUNK, S5_SLAB
    xc = _chunk_rows(x_ref)
    xp = xp_ref[...].astype(BF16)
    d = d_ref[...]
    for t in range(t_len):
        y = _dot(xc[:, :(t + 1) * w], gr_ref[(t_len - 1 - t) * w:, :])
        y = y + _dot(xp, cp_ref[t])
        y_ref[:, t, :] = y + d * x_ref[:, t, :]


def _s5_output(x3, xp, gr, cp, d, *, rows):
    nc = x3.shape[0]
    t_len, w, ns = S5_CHUNK, S5_SLAB, S5_SLAB_STATES
    once = dict(pipeline_mode=pl.Buffered(1))
    return pl.pallas_call(
        _s5_output_kernel,
        grid=(S5_SLABS, nc // rows),
        in_specs=[pl.BlockSpec((rows, t_len, w), lambda s, r: (r, 0, s)),
                  pl.BlockSpec((rows, 2 * ns), lambda s, r: (r, s)),
                  pl.BlockSpec((None, t_len * w, w), lambda s, r: (s, 0, 0), **once),
                  pl.BlockSpec((None, t_len, 2 * ns, w), lambda s, r: (s, 0, 0, 0), **once),
                  pl.BlockSpec((None, 1, w), lambda s, r: (s, 0, 0))],
        out_specs=pl.BlockSpec((rows, t_len, w), lambda s, r: (r, 0, s)),
        out_shape=jax.ShapeDtypeStruct(x3.shape, F32),
        compiler_params=_params("parallel", "arbitrary"),
        name="s5_chunk_output",
    )(x3, xp, gr, cp, d)


def _s5(s5_in, lam_re, lam_im, log_step, b_re, b_im, c_re, c_im, d, *, batch):
    m, width = s5_in.shape
    sl, sg, p, grp = S5_SLABS, S5_SLAB_GROUPS, S5_STATE, S5_GROUP
    ns = S5_SLAB_STATES
    lam = jnp.stack([lam_re, lam_im], axis=0).astype(F32).reshape(2, sl, ns).transpose(1, 0, 2)
    ls = jnp.repeat(log_step.astype(F32), p).reshape(sl, 1, ns)
    eye = jnp.eye(sg, dtype=F32)
    b = jnp.stack([b_re, b_im], axis=0).astype(F32).reshape(2, sl, sg, p, grp)
    bdb = jnp.einsum('rsgpi,gh->srgihp', b, eye).reshape(sl, 2, sg * grp, ns)
    c = jnp.stack([c_re, c_im], axis=0).astype(F32).reshape(2, sl, sg, grp, p)
    bdc = jnp.einsum('rsgop,gh->srhpgo', c, eye).reshape(sl, 2, ns, sg * grp)
    bp, cp, gr, l16 = _s5_matrices(lam, ls, bdb, bdc)
    x3 = s5_in.reshape(m // S5_CHUNK, S5_CHUNK, width)
    states = _s5_states(x3, bp, rows=512)
    xprev = _s5_chunk_scan(states, l16, batch=batch)
    y3 = _s5_output(x3, xprev, gr, cp, d.astype(F32).reshape(sl, 1, sg * grp), rows=256)
    return y3.reshape(m, width)


def _glu_kernel(y_ref, w_ref, b_ref, o_ref):
    g = jax.nn.gelu(y_ref[...])
    o_ref[...] = (g * jax.nn.sigmoid(_dot(g.astype(BF16), w_ref[...]) + b_ref[...])).astype(o_ref.dtype)


def _glu(y, w, b, *, bm=512):
    m, n = y.shape
    return pl.pallas_call(
        _glu_kernel,
        grid=(m // bm,),
        in_specs=[pl.BlockSpec((bm, n), lambda i: (i, 0)),
                  pl.BlockSpec((n, n), lambda i: (0, 0)),
                  pl.BlockSpec((1, n), lambda i: (0, 0))],
        out_specs=pl.BlockSpec((bm, n), lambda i: (i, 0)),
        out_shape=jax.ShapeDtypeStruct((m, n), BF16),
        compiler_params=_params("parallel"),
        name="s5_glu",
    )(y, w, b)


def _merge_kernel(u_ref, a_ref, s_ref, wgf_ref, wgs_ref, wpf_ref, wps_ref, bgf_ref, bgs_ref, o_ref):
    u = u_ref[...]
    gate_fox = jax.nn.sigmoid(_dot(u, wgf_ref[...]) + bgf_ref[...])
    gate_s5 = jax.nn.sigmoid(_dot(u, wgs_ref[...]) + bgs_ref[...])
    o_ref[...] = (gate_fox * _dot(a_ref[...], wpf_ref[...])
                  + gate_s5 * _dot(s_ref[...], wps_ref[...])).astype(o_ref.dtype)


def _merge(u, attn, ssm, wgf, wgs, wpf, wps, bgf, bgs, *, bm=512, bn=512):
    m, d = u.shape
    n = wgf.shape[1]
    act = lambda a: pl.BlockSpec((bm, a.shape[1]), lambda i, j: (i, 0))
    wgt = lambda w: pl.BlockSpec((w.shape[0], bn), lambda i, j: (0, j))
    return pl.pallas_call(
        _merge_kernel,
        grid=(m // bm, n // bn),
        in_specs=[act(u), act(attn), act(ssm), wgt(wgf), wgt(wgs), wgt(wpf), wgt(wps),
                  pl.BlockSpec((1, bn), lambda i, j: (0, j)),
                  pl.BlockSpec((1, bn), lambda i, j: (0, j))],
        out_specs=pl.BlockSpec((bm, bn), lambda i, j: (i, j)),
        out_shape=jax.ShapeDtypeStruct((m, n), BF16),
        compiler_params=_params("parallel", "arbitrary"),
        name="gated_merge",
    )(u, attn, ssm, wgf, wgs, wpf, wps, bgf, bgs)


NORM_LANES = 128


def _out_proj_kernel(a_ref, w_ref, x_ref, g_ref, h_ref, hg_ref, r_ref, ssq_ref, *, d_model):
    j = pl.program_id(1)

    @pl.when(j == 0)
    def _():
        ssq_ref[...] = jnp.zeros_like(ssq_ref)

    h = x_ref[...] + _dot(a_ref[...], w_ref[...])
    h_ref[...] = h
    hg_ref[...] = (h * g_ref[...]).astype(hg_ref.dtype)
    ssq_ref[...] += jnp.sum(h * h, axis=-1, keepdims=True)

    @pl.when(j == pl.num_programs(1) - 1)
    def _():
        r_ref[...] = jnp.broadcast_to(lax.rsqrt(ssq_ref[...] * (1.0 / d_model) + RMS_EPS), r_ref.shape)


def _out_proj(a, w, x, g, *, bm, bn):
    m, k = a.shape
    n = w.shape[1]
    return pl.pallas_call(
        functools.partial(_out_proj_kernel, d_model=n),
        grid=(m // bm, n // bn),
        in_specs=[pl.BlockSpec((bm, k), lambda i, j: (i, 0)),
                  pl.BlockSpec((k, bn), lambda i, j: (0, j)),
                  pl.BlockSpec((bm, bn), lambda i, j: (i, j)),
                  pl.BlockSpec((1, bn), lambda i, j: (0, j))],
        out_specs=[pl.BlockSpec((bm, bn), lambda i, j: (i, j)),
                   pl.BlockSpec((bm, bn), lambda i, j: (i, j)),
                   pl.BlockSpec((bm, NORM_LANES), lambda i, j: (i, 0))],
        out_shape=[jax.ShapeDtypeStruct((m, n), F32),
                   jax.ShapeDtypeStruct((m, n), BF16),
                   jax.ShapeDtypeStruct((m, NORM_LANES), F32)],
        scratch_shapes=[pltpu.VMEM((bm, 1), F32)],
        compiler_params=_params("parallel", "arbitrary"),
        name="out_proj_residual",
    )(a, w, x, g)


def _round_weights_once(w_ref, wb_ref):
    @pl.when(pl.program_id(1) == 0)
    def _():
        wb_ref[...] = w_ref[...].astype(BF16)


def _swiglu_kernel(a_ref, r_ref, wg_ref, wu_ref, o_ref, wgb_ref, wub_ref):
    _round_weights_once(wg_ref, wgb_ref)
    _round_weights_once(wu_ref, wub_ref)
    a = a_ref[...]
    r = jnp.concatenate([r_ref[...]] * (o_ref.shape[1] // NORM_LANES), axis=1)
    gate = _dot(a, wgb_ref[...]) * r
    up = _dot(a, wub_ref[...]) * r
    o_ref[...] = (jax.nn.silu(gate) * up).astype(o_ref.dtype)


def _swiglu_up(a, r, w_gate_up, d_ff, *, bm, bn):
    m, k = a.shape
    nj = d_ff // bn
    return pl.pallas_call(
        _swiglu_kernel,
        grid=(nj, m // bm),
        in_specs=[pl.BlockSpec((bm, k), lambda j, i: (i, 0)),
                  pl.BlockSpec((bm, NORM_LANES), lambda j, i: (i, 0)),
                  pl.BlockSpec((k, bn), lambda j, i: (0, j)),
                  pl.BlockSpec((k, bn), lambda j, i: (0, j + nj))],
        out_specs=pl.BlockSpec((bm, bn), lambda j, i: (i, j)),
        out_shape=jax.ShapeDtypeStruct((m, d_ff), BF16),
        scratch_shapes=[pltpu.VMEM((k, bn), BF16)] * 2,
        compiler_params=_params("parallel", "arbitrary"),
        name="swiglu_up",
    )(a, r, w_gate_up, w_gate_up)


def _down_kernel(a_ref, w_ref, r_ref, o_ref):
    o_ref[...] = r_ref[...] + _dot(a_ref[...], w_ref[...])


def _down_residual(a, w, res, *, bm, bn):
    m, k = a.shape
    n = w.shape[1]
    return pl.pallas_call(
        _down_kernel,
        grid=(m // bm, n // bn),
        in_specs=[pl.BlockSpec((bm, k), lambda i, j: (i, 0)),
                  pl.BlockSpec((k, bn), lambda i, j: (0, j)),
                  pl.BlockSpec((bm, bn), lambda i, j: (i, j))],
        out_specs=pl.BlockSpec((bm, bn), lambda i, j: (i, j)),
        out_shape=jax.ShapeDtypeStruct((m, n), F32),
        compiler_params=_params("parallel", "arbitrary"),
        name="down_proj_residual",
    )(a, w, res)


def _layer(x, batch, seq, g_mix, w_in, b_fgate, b_gates, q_norm, k_norm,
           lam_re, lam_im, log_step, b_re, b_im, c_re, c_im, s5_d,
           w_glu, b_glu, w_proj_fox, w_proj_s5, w_out, g_ffn, w_gate_up, w_down):
    m, d_model = x.shape
    fox_w = FOX_HEADS * FOX_HEAD_DIM
    s5_w = S5_GROUP * S5_GROUPS
    col_k, col_v = fox_w, 2 * fox_w
    col_f = 3 * fox_w
    col_s5 = col_f + FOX_HEADS
    col_g = col_s5 + s5_w
    d_ff = w_down.shape[0]
    w_in_b = w_in.astype(BF16)
    w_f, w_s5 = w_in_b[:, col_f:col_s5], w_in_b[:, col_s5:col_g]
    w_gate_fox, w_gate_s5 = w_in_b[:, col_g:col_g + d_model], w_in_b[:, col_g + d_model:col_g + 2 * d_model]

    u = _rmsnorm(x, g_mix)

    head_scale = jnp.concatenate([
        jnp.tile(q_norm.astype(F32), FOX_HEADS) * (LOG2_E / math.sqrt(FOX_HEAD_DIM)),
        jnp.tile(k_norm.astype(F32), FOX_HEADS),
        jnp.ones((fox_w,), F32)]).reshape(1, 3 * fox_w)
    qkv = _qkv_proj(u, w_in_b, head_scale, bm=1024, bn=1024)
    s5_in, f_t = _s5_fgate_proj(u, w_s5, w_f.T, b_fgate.astype(F32).reshape(FOX_HEADS, 1), batch, seq)

    blk = 512
    f_rows = f_t.reshape(batch, FOX_HEADS, seq // blk, 1, blk)
    qk_bound = (1.02 * LOG2_E * math.sqrt(FOX_HEAD_DIM)
                * jnp.max(jnp.abs(q_norm.astype(F32))) * jnp.max(jnp.abs(k_norm.astype(F32))))
    attn = _attention(qkv.reshape(batch, seq, 3 * fox_w), f_rows, qk_bound, blk=blk).reshape(m, fox_w)

    y5 = _s5(s5_in, lam_re, lam_im, log_step, b_re, b_im, c_re, c_im, s5_d, batch=batch)
    ssm = _glu(y5, w_glu.astype(BF16), b_glu.astype(F32).reshape(1, s5_w))

    b_gates = b_gates.astype(F32).reshape(1, 2 * d_model)
    merged = _merge(u, attn, ssm, w_gate_fox, w_gate_s5,
                    w_proj_fox.astype(BF16), w_proj_s5.astype(BF16),
                    b_gates[:, :d_model], b_gates[:, d_model:])
    h, hg, h_inv_rms = _out_proj(merged, w_out.astype(BF16), x, g_ffn.astype(F32).reshape(1, d_model),
                                 bm=1024, bn=512)
    act = _swiglu_up(hg, h_inv_rms, w_gate_up, d_ff, bm=1024, bn=256)
    return _down_residual(act, w_down.astype(BF16), h, bm=512, bn=512)


def kernel(x, g_mix, w_in, b_fgate, b_gates, q_norm, k_norm, s5_lambda_re, s5_lambda_im, s5_log_step, s5_b_re, s5_b_im, s5_c_re, s5_c_im, s5_d, w_glu, b_glu, w_proj_fox, w_proj_s5, w_out, g_ffn, w_gate_up, w_down):
    batch, seq, d_model = x.shape
    h = x.reshape(batch * seq, d_model)
    for l in range(g_mix.shape[0]):
        h = _layer(h, batch, seq, g_mix[l], w_in[l], b_fgate[l], b_gates[l], q_norm[l], k_norm[l],
                   s5_lambda_re[l], s5_lambda_im[l], s5_log_step[l], s5_b_re[l], s5_b_im[l],
                   s5_c_re[l], s5_c_im[l], s5_d[l], w_glu[l], b_glu[l],
                   w_proj_fox[l], w_proj_s5[l], w_out[l], g_ffn[l], w_gate_up[l], w_down[l])
    return h.reshape(batch, seq, d_model)
```

```python
import functools
import math

import jax
import jax.numpy as jnp
from jax import lax
from jax.experimental import pallas as pl
from jax.experimental.pallas import tpu as pltpu

F32 = jnp.float32
BF16 = jnp.bfloat16

FOX_HEADS = 16
FOX_HEAD_DIM = 128
S5_GROUP = 16
S5_GROUPS = 64
S5_STATE = 64
S5_CHUNK = 16
RMS_EPS = 1e-6
MASK_VALUE = -1e30
LOG2_E = math.log2(math.e)

VMEM_LIMIT_BYTES = 56 * 1024 * 1024


def _params(*semantics):
    return pltpu.CompilerParams(dimension_semantics=semantics,
                                vmem_limit_bytes=VMEM_LIMIT_BYTES)


def _dot(a, b):
    return jnp.dot(a, b, preferred_element_type=F32)


def _cast_job(w, steps, step_index):
    rows = w.shape[0] // steps
    assert rows * steps == w.shape[0] and rows % 16 == 0, (w.shape, steps)
    return (pl.BlockSpec((rows, w.shape[1]), lambda *g: (step_index(*g), 0)),
            jax.ShapeDtypeStruct(w.shape, BF16))


def _run_cast_jobs(src_refs, dst_refs):
    for src, dst in zip(src_refs, dst_refs):
        dst[...] = src[...].astype(dst.dtype)


def _rmsnorm_kernel(x_ref, g_ref, o_ref):
    x = x_ref[...]
    ms = jnp.mean(x * x, axis=-1, keepdims=True)
    o_ref[...] = (x * lax.rsqrt(ms + RMS_EPS) * g_ref[...]).astype(o_ref.dtype)


def _rmsnorm(x, g, *, bm=256):
    m, d = x.shape
    return pl.pallas_call(
        _rmsnorm_kernel,
        grid=(m // bm,),
        in_specs=[pl.BlockSpec((bm, d), lambda i: (i, 0)),
                  pl.BlockSpec((1, d), lambda i: (0, 0))],
        out_specs=pl.BlockSpec((bm, d), lambda i: (i, 0)),
        out_shape=jax.ShapeDtypeStruct((m, d), BF16),
        compiler_params=_params("parallel"),
        name="rmsnorm",
    )(x, g.reshape(1, d).astype(F32))


def _qkv_kernel(a_ref, w_ref, s_ref, o_ref, *, qk_tiles):
    acc = _dot(a_ref[...], w_ref[...])
    is_qk = pl.program_id(1) < qk_tiles
    for h in range(acc.shape[1] // FOX_HEAD_DIM):
        cols = slice(h * FOX_HEAD_DIM, (h + 1) * FOX_HEAD_DIM)
        blk = acc[:, cols]
        ms = jnp.mean(blk * blk, axis=-1, keepdims=True)
        inv = jnp.where(is_qk, lax.rsqrt(ms + RMS_EPS), 1.0)
        o_ref[:, cols] = (blk * inv * s_ref[:, cols]).astype(o_ref.dtype)


def _qkv_proj(a, w_in, scale, *, bm, bn):
    m, k = a.shape
    n = scale.shape[1]
    qk_tiles = 2 * FOX_HEADS * FOX_HEAD_DIM // bn
    return pl.pallas_call(
        functools.partial(_qkv_kernel, qk_tiles=qk_tiles),
        grid=(m // bm, n // bn),
        in_specs=[pl.BlockSpec((bm, k), lambda i, j: (i, 0)),
                  pl.BlockSpec((k, bn), lambda i, j: (0, j)),
                  pl.BlockSpec((1, bn), lambda i, j: (0, j))],
        out_specs=pl.BlockSpec((bm, bn), lambda i, j: (i, j)),
        out_shape=jax.ShapeDtypeStruct((m, n), BF16),
        compiler_params=_params("parallel", "arbitrary"),
        name="qkv_proj",
    )(a, w_in, scale)


def _s5_fgate_kernel(u_ref, ws_ref, wt_ref, b_ref, wa_ref, wb_ref, s_ref, f_ref, wa_out_ref, wb_out_ref,
                     carry_ref):
    @pl.when(pl.program_id(1) == 0)
    def _():
        carry_ref[...] = jnp.zeros_like(carry_ref)

    _run_cast_jobs((wa_ref, wb_ref), (wa_out_ref, wb_out_ref))

    u = u_ref[...]
    s_ref[...] = _dot(u, ws_ref[...])
    z = lax.dot_general(wt_ref[...], u, (((1,), (1,)), ((), ())),
                        preferred_element_type=F32) + b_ref[...]
    x = jnp.minimum(z, 0.0) - jnp.log1p(jnp.exp(-jnp.abs(z)))
    bs = x.shape[1]
    lane = lax.broadcasted_iota(jnp.int32, x.shape, 1)
    shift = 1
    while shift < bs:
        x = x + jnp.where(lane >= shift, pltpu.roll(x, shift, 1), 0.0)
        shift *= 2
    x = x + carry_ref[:, 0:1]
    f_ref[...] = x * LOG2_E
    carry_ref[...] = jnp.broadcast_to(x[:, bs - 1:bs], carry_ref.shape)


def _s5_fgate_proj(u, w_in_b, s5_col, n, wt, bias, batch, seq, cast_a, cast_b, *, bs=1024):
    h, d = wt.shape
    ns = seq // bs
    step = lambda b, j: b * ns + j
    (spec_a, shape_a), (spec_b, shape_b) = (_cast_job(w, batch * ns, step) for w in (cast_a, cast_b))
    return pl.pallas_call(
        _s5_fgate_kernel,
        grid=(batch, ns),
        in_specs=[pl.BlockSpec((bs, d), lambda b, j: (b * ns + j, 0)),
                  pl.BlockSpec((d, n), lambda b, j: (0, s5_col // n)),
                  pl.BlockSpec((h, d), lambda b, j: (0, 0)),
                  pl.BlockSpec((h, 1), lambda b, j: (0, 0)),
                  spec_a, spec_b],
        out_specs=[pl.BlockSpec((bs, n), lambda b, j: (b * ns + j, 0)),
                   pl.BlockSpec((None, h, bs), lambda b, j: (b, 0, j)),
                   spec_a, spec_b],
        out_shape=[jax.ShapeDtypeStruct((batch * seq, n), F32),
                   jax.ShapeDtypeStruct((batch, h, seq), F32),
                   shape_a, shape_b],
        scratch_shapes=[pltpu.VMEM((h, 128), F32)],
        compiler_params=_params("parallel", "arbitrary"),
        name="s5_in_fgate_proj",
    )(u, w_in_b, wt, bias, cast_a, cast_b)


ATTN_HEADS_PER_STEP = 4


def _attn_kernel(q_ref, k_ref, v_ref, f_ref, o_ref, *, blk):
    i = pl.program_id(2)
    dh = FOX_HEAD_DIM
    heads = range(ATTN_HEADS_PER_STEP)
    qs = [q_ref[:, h * dh:(h + 1) * dh] for h in heads]

    def block(j, carry, masked):
        start = pl.multiple_of(j * blk, blk)
        out = []
        for h in heads:
            m, l, acc = carry[h]
            ks = k_ref[pl.ds(start, blk), h * dh:(h + 1) * dh]
            vs = v_ref[pl.ds(start, blk), h * dh:(h + 1) * dh]
            s = lax.dot_general(qs[h], ks, (((1,), (1,)), ((), ())), preferred_element_type=F32)
            s = s - f_ref[h, j]
            if masked:
                row = lax.broadcasted_iota(jnp.int32, s.shape, 0)
                col = lax.broadcasted_iota(jnp.int32, s.shape, 1)
                s = jnp.where(col <= row, s, MASK_VALUE)
            m_new = jnp.maximum(m, jnp.max(s, axis=-1, keepdims=True))
            alpha = jnp.exp2(m - m_new)
            p = jnp.exp2(s - m_new)
            l = alpha * l + jnp.sum(p, axis=-1, keepdims=True)
            acc = alpha * acc + _dot(p.astype(BF16), vs)
            out.append((m_new, l, acc))
        return tuple(out)

    init = tuple((jnp.full((blk, 1), MASK_VALUE, F32), jnp.zeros((blk, 1), F32),
                  jnp.zeros((blk, dh), F32)) for _ in heads)
    carry = lax.fori_loop(0, i, lambda j, c: block(j, c, False), init)
    carry = block(i, carry, True)
    for h in heads:
        _, l, acc = carry[h]
        o_ref[:, h * dh:(h + 1) * dh] = (acc / l).astype(o_ref.dtype)


def _attn_shift_kernel(c_ref, q_ref, k_ref, v_ref, f_ref, o_ref, acc_ref, shift_ref, p_ref, *, blk):
    i = pl.program_id(2)
    dh = FOX_HEAD_DIM
    heads = range(ATTN_HEADS_PER_STEP)
    qs = [q_ref[:, h * dh:(h + 1) * dh] for h in heads]
    ones = jnp.ones((blk, dh), BF16)

    def setup():
        for h in heads:
            col = jnp.broadcast_to(f_ref[h, i], (8, blk)).T[:, 0:1] - c_ref[0]
            shift_ref[h] = jnp.broadcast_to(col, (blk, dh))
        acc_ref[...] = jnp.zeros_like(acc_ref)

    def probabilities(h, j, masked):
        start = pl.multiple_of(j * blk, blk)
        ks = k_ref[pl.ds(start, blk), h * dh:(h + 1) * dh]
        s = lax.dot_general(qs[h], ks, (((1,), (1,)), ((), ())), preferred_element_type=F32)
        s = (s + jnp.concatenate([shift_ref[h]] * (blk // dh), axis=1)) - f_ref[h, j]
        if masked:
            row = lax.broadcasted_iota(jnp.int32, s.shape, 0)
            col = lax.broadcasted_iota(jnp.int32, s.shape, 1)
            s = jnp.where(col <= row, s, MASK_VALUE)
        return jnp.exp2(s).astype(BF16)

    def values(h, j):
        start = pl.multiple_of(j * blk, blk)
        return jnp.concatenate([v_ref[pl.ds(start, blk), h * dh:(h + 1) * dh], ones], axis=1)

    setup()

    @pl.when(i == 0)
    def _():
        p_ref[...] = jnp.zeros_like(p_ref)

    @pl.when(i > 0)
    def _():
        for h in heads:
            p_ref[h] = probabilities(h, 0, False)

    @pl.loop(1, i)
    def _(j):
        for h in heads:
            acc_ref[h] += _dot(p_ref[h], values(h, j - 1))
        for h in heads:
            p_ref[h] = probabilities(h, j, False)

    prev = jnp.maximum(i - 1, 0)
    for h in heads:
        acc_ref[h] += _dot(p_ref[h], values(h, prev)) + _dot(probabilities(h, i, True), values(h, i))
    for h in heads:
        acc = acc_ref[h]
        o_ref[:, h * dh:(h + 1) * dh] = (acc[:, :dh] / acc[:, dh:]).astype(o_ref.dtype)


ATTN_SHIFT_LIMIT = 56.0


def _attention(qkv, f_rows, qk_bound, *, blk=512):
    batch, seq, _ = qkv.shape
    nblk = seq // blk
    hps = ATTN_HEADS_PER_STEP
    wide = hps * FOX_HEAD_DIM
    third = FOX_HEADS // hps
    specs = [pl.BlockSpec((None, blk, wide), lambda b, h, i: (b, i, h)),
             pl.BlockSpec((None, seq, wide), lambda b, h, i: (b, 0, h + third)),
             pl.BlockSpec((None, seq, wide), lambda b, h, i: (b, 0, h + 2 * third)),
             pl.BlockSpec((None, hps, nblk, 1, blk), lambda b, h, i: (b, h, 0, 0, 0))]
    common = dict(
        grid=(batch, FOX_HEADS // hps, nblk),
        out_specs=pl.BlockSpec((None, blk, wide), lambda b, h, i: (b, i, h)),
        out_shape=jax.ShapeDtypeStruct((batch, seq, FOX_HEADS * FOX_HEAD_DIM), BF16),
        compiler_params=_params("parallel", "parallel", "arbitrary"))

    def shifted(qkv, f_rows, c):
        return pl.pallas_call(
            functools.partial(_attn_shift_kernel, blk=blk),
            in_specs=[pl.BlockSpec(memory_space=pltpu.SMEM)] + specs,
            scratch_shapes=[pltpu.VMEM((hps, blk, 2 * FOX_HEAD_DIM), F32),
                            pltpu.VMEM((hps, blk, FOX_HEAD_DIM), F32),
                            pltpu.VMEM((hps, blk, blk), BF16)],
            name="fox_attention_shift", **common)(c.reshape(1), qkv, qkv, qkv, f_rows)

    def online(qkv, f_rows, c):
        return pl.pallas_call(functools.partial(_attn_kernel, blk=blk), in_specs=specs,
                              name="fox_attention_online", **common)(qkv, qkv, qkv, f_rows)

    return lax.cond(qk_bound <= ATTN_SHIFT_LIMIT, shifted, online, qkv, f_rows, qk_bound)


S5_SLAB_GROUPS = 16
S5_SLAB = S5_SLAB_GROUPS * S5_GROUP
S5_SLAB_STATES = S5_SLAB_GROUPS * S5_STATE
S5_SLABS = S5_GROUPS // S5_SLAB_GROUPS


def _split_bf16(a):
    hi = a.astype(BF16)
    return hi, (a - hi.astype(F32)).astype(BF16)


def _dot_split(a, b_hi, b_lo):
    a_hi, a_lo = _split_bf16(a)
    return _dot(a_hi, b_hi) + _dot(a_hi, b_lo) + _dot(a_lo, b_hi)


def _lam_pow(k, lr, li, dt):
    mag = jnp.exp(k * (lr * dt))
    ang = k * (li * dt)
    return mag * jnp.cos(ang), mag * jnp.sin(ang)


def _s5_matrices_kernel(lam_ref, ls_ref, bdb_ref, bdc_ref, bp_ref, cp_ref, gr_ref, l16_ref,
                        bb_ref, chi_ref, clo_ref):
    t = pl.program_id(1)
    ns = S5_SLAB_STATES
    lr, li = lam_ref[0:1, :], lam_ref[1:2, :]
    dt = jnp.exp(ls_ref[...])

    @pl.when(t == 0)
    def _():
        lbr, lbi = _lam_pow(1.0, lr, li, dt)
        den = lr * lr + li * li
        nr = lbr - 1.0
        fr = (nr * lr + lbi * li) / den
        fi = (lbi * lr - nr * li) / den
        b_re, b_im = bdb_ref[0], bdb_ref[1]
        bb_ref[0] = fr * b_re - fi * b_im
        bb_ref[1] = fr * b_im + fi * b_re
        for part in range(2):
            chi_ref[part], clo_ref[part] = _split_bf16(bdc_ref[part])

    bb_re, bb_im = bb_ref[0], bb_ref[1]
    pr, pi = _lam_pow((S5_CHUNK - 1 - t).astype(F32), lr, li, dt)
    bp_re = bb_re * pr - bb_im * pi
    bp_im = bb_re * pi + bb_im * pr
    bp_ref[:, :ns] = bp_re.astype(BF16)
    bp_ref[:, ns:] = bp_im.astype(BF16)

    qr, qi = _lam_pow((t + 1).astype(F32), lr, li, dt)
    qr = jnp.broadcast_to(qr, (8, ns)).T[:, 0:1]
    qi = jnp.broadcast_to(qi, (8, ns)).T[:, 0:1]
    c_re, c_im = bdc_ref[0], bdc_ref[1]
    cp_ref[:ns, :] = (c_re * qr - c_im * qi).astype(BF16)
    cp_ref[ns:, :] = (-(c_re * qi + c_im * qr)).astype(BF16)

    gr_ref[...] = (_dot_split(bp_re, chi_ref[0], clo_ref[0])
                   - _dot_split(bp_im, chi_ref[1], clo_ref[1])).astype(BF16)

    l16r, l16i = _lam_pow(float(S5_CHUNK), lr, li, dt)
    l16_ref[0:1, :] = l16r
    l16_ref[1:2, :] = l16i


def _s5_matrices(lam, ls, bdb, bdc):
    t_len, w, ns = S5_CHUNK, S5_SLAB, S5_SLAB_STATES
    return pl.pallas_call(
        _s5_matrices_kernel,
        grid=(S5_SLABS, t_len),
        in_specs=[pl.BlockSpec((None, 2, ns), lambda s, t: (s, 0, 0)),
                  pl.BlockSpec((None, 1, ns), lambda s, t: (s, 0, 0)),
                  pl.BlockSpec((None, 2, w, ns), lambda s, t: (s, 0, 0, 0)),
                  pl.BlockSpec((None, 2, ns, w), lambda s, t: (s, 0, 0, 0))],
        out_specs=[pl.BlockSpec((None, w, 2 * ns), lambda s, t: (s, t, 0)),
                   pl.BlockSpec((None, None, 2 * ns, w), lambda s, t: (s, t, 0, 0)),
                   pl.BlockSpec((None, w, w), lambda s, t: (s, t, 0)),
                   pl.BlockSpec((None, 2, ns), lambda s, t: (s, 0, 0))],
        out_shape=[jax.ShapeDtypeStruct((S5_SLABS, t_len * w, 2 * ns), BF16),
                   jax.ShapeDtypeStruct((S5_SLABS, t_len, 2 * ns, w), BF16),
                   jax.ShapeDtypeStruct((S5_SLABS, t_len * w, w), BF16),
                   jax.ShapeDtypeStruct((S5_SLABS, 2, ns), F32)],
        scratch_shapes=[pltpu.VMEM((2, w, ns), F32), pltpu.VMEM((2, ns, w), BF16), pltpu.VMEM((2, ns, w), BF16)],
        compiler_params=_params("parallel", "arbitrary"),
        name="s5_matrices",
    )(lam, ls, bdb, bdc)


def _chunk_rows(x_ref):
    return jnp.concatenate([x_ref[:, t, :] for t in range(S5_CHUNK)], axis=1).astype(BF16)


def _s5_states_kernel(x_ref, bp_ref, s_ref):
    s_ref[...] = _dot(_chunk_rows(x_ref), bp_ref[...])


def _s5_states(x3, bp, *, rows):
    nc = x3.shape[0]
    t_len, w, ns = S5_CHUNK, S5_SLAB, S5_SLAB_STATES
    return pl.pallas_call(
        _s5_states_kernel,
        grid=(S5_SLABS, nc // rows),
        in_specs=[pl.BlockSpec((rows, t_len, w), lambda s, r: (r, 0, s)),
                  pl.BlockSpec((None, t_len * w, 2 * ns), lambda s, r: (s, 0, 0),
                               pipeline_mode=pl.Buffered(1))],
        out_specs=pl.BlockSpec((rows, 2 * ns), lambda s, r: (r, s)),
        out_shape=jax.ShapeDtypeStruct((nc, S5_SLABS * 2 * ns), F32),
        compiler_params=_params("parallel", "arbitrary"),
        name="s5_chunk_states",
    )(x3, bp)


def _s5_chunk_scan_kernel(s_ref, l16_ref, xp_ref, *, batch, chunks):
    ns = S5_SLAB_STATES
    ar, ai = l16_ref[0:1, :], l16_ref[1:2, :]

    def step(c, carry):
        out = []
        for b, (xr, xi) in enumerate(carry):
            row = pl.ds(b * chunks + c, 1)
            xp_ref[row, :ns] = xr
            xp_ref[row, ns:] = xi
            sr = s_ref[row, :ns]
            si = s_ref[row, ns:]
            out.append((ar * xr - ai * xi + sr, ar * xi + ai * xr + si))
        return tuple(out)

    zero = jnp.zeros((1, ns), F32)
    lax.fori_loop(0, chunks, step, ((zero, zero),) * batch)


def _s5_chunk_scan(s, l16, *, batch):
    nc = s.shape[0]
    ns = S5_SLAB_STATES
    return pl.pallas_call(
        functools.partial(_s5_chunk_scan_kernel, batch=batch, chunks=nc // batch),
        grid=(S5_SLABS,),
        in_specs=[pl.BlockSpec((nc, 2 * ns), lambda s_: (0, s_)),
                  pl.BlockSpec((None, 2, ns), lambda s_: (s_, 0, 0))],
        out_specs=pl.BlockSpec((nc, 2 * ns), lambda s_: (0, s_)),
        out_shape=jax.ShapeDtypeStruct(s.shape, F32),
        compiler_params=_params("parallel"),
        name="s5_chunk_scan",
    )(s, l16)


def _s5_output_kernel(x_ref, xp_ref, gr_ref, cp_ref, d_ref, y_ref):
    t_len, w = S5_CHUNK, S5_SLAB
    xc = _chunk_rows(x_ref)
    xp = xp_ref[...].astype(BF16)
    d = d_ref[...]
    for t in range(t_len):
        y = _dot(xc[:, :(t + 1) * w], gr_ref[(t_len - 1 - t) * w:, :])
        y = y + _dot(xp, cp_ref[t])
        y_ref[:, t, :] = y + d * x_ref[:, t, :]


def _s5_output(x3, xp, gr, cp, d, *, rows):
    nc = x3.shape[0]
    t_len, w, ns = S5_CHUNK, S5_SLAB, S5_SLAB_STATES
    once = dict(pipeline_mode=pl.Buffered(1))
    return pl.pallas_call(
        _s5_output_kernel,
        grid=(S5_SLABS, nc // rows),
        in_specs=[pl.BlockSpec((rows, t_len, w), lambda s, r: (r, 0, s)),
                  pl.BlockSpec((rows, 2 * ns), lambda s, r: (r, s)),
                  pl.BlockSpec((None, t_len * w, w), lambda s, r: (s, 0, 0), **once),
                  pl.BlockSpec((None, t_len, 2 * ns, w), lambda s, r: (s, 0, 0, 0), **once),
                  pl.BlockSpec((None, 1, w), lambda s, r: (s, 0, 0))],
        out_specs=pl.BlockSpec((rows, t_len, w), lambda s, r: (r, 0, s)),
        out_shape=jax.ShapeDtypeStruct(x3.shape, F32),
        compiler_params=_params("parallel", "arbitrary"),
        name="s5_chunk_output",
    )(x3, xp, gr, cp, d)


def _s5(s5_in, lam_re, lam_im, log_step, b_re, b_im, c_re, c_im, d, *, batch):
    m, width = s5_in.shape
    sl, sg, p, grp = S5_SLABS, S5_SLAB_GROUPS, S5_STATE, S5_GROUP
    ns = S5_SLAB_STATES
    lam = jnp.stack([lam_re, lam_im], axis=0).astype(F32).reshape(2, sl, ns).transpose(1, 0, 2)
    ls = jnp.repeat(log_step.astype(F32), p).reshape(sl, 1, ns)
    eye = jnp.eye(sg, dtype=F32)
    b = jnp.stack([b_re, b_im], axis=0).astype(F32).reshape(2, sl, sg, p, grp)
    bdb = jnp.einsum('rsgpi,gh->srgihp', b, eye).reshape(sl, 2, sg * grp, ns)
    c = jnp.stack([c_re, c_im], axis=0).astype(F32).reshape(2, sl, sg, grp, p)
    bdc = jnp.einsum('rsgop,gh->srhpgo', c, eye).reshape(sl, 2, ns, sg * grp)
    bp, cp, gr, l16 = _s5_matrices(lam, ls, bdb, bdc)
    x3 = s5_in.reshape(m // S5_CHUNK, S5_CHUNK, width)
    states = _s5_states(x3, bp, rows=512)
    xprev = _s5_chunk_scan(states, l16, batch=batch)
    y3 = _s5_output(x3, xprev, gr, cp, d.astype(F32).reshape(sl, 1, sg * grp), rows=256)
    return y3.reshape(m, width)


def _glu_kernel(y_ref, w_ref, b_ref, o_ref):
    g = jax.nn.gelu(y_ref[...])
    o_ref[...] = (g * jax.nn.sigmoid(_dot(g.astype(BF16), w_ref[...]) + b_ref[...])).astype(o_ref.dtype)


def _glu(y, w, b, *, bm=512):
    m, n = y.shape
    return pl.pallas_call(
        _glu_kernel,
        grid=(m // bm,),
        in_specs=[pl.BlockSpec((bm, n), lambda i: (i, 0)),
                  pl.BlockSpec((n, n), lambda i: (0, 0)),
                  pl.BlockSpec((1, n), lambda i: (0, 0))],
        out_specs=pl.BlockSpec((bm, n), lambda i: (i, 0)),
        out_shape=jax.ShapeDtypeStruct((m, n), BF16),
        compiler_params=_params("parallel"),
        name="s5_glu",
    )(y, w, b)


def _merge_kernel(u_ref, a_ref, s_ref, wgf_ref, wgs_ref, wpf_ref, wps_ref, bgf_ref, bgs_ref, wc_ref,
                  o_ref, wc_out_ref):
    _run_cast_jobs((wc_ref,), (wc_out_ref,))
    u = u_ref[...]
    gate_fox = jax.nn.sigmoid(_dot(u, wgf_ref[...]) + bgf_ref[...])
    gate_s5 = jax.nn.sigmoid(_dot(u, wgs_ref[...]) + bgs_ref[...])
    o_ref[...] = (gate_fox * _dot(a_ref[...], wpf_ref[...])
                  + gate_s5 * _dot(s_ref[...], wps_ref[...])).astype(o_ref.dtype)


def _merge(u, attn, ssm, w_in_b, gate_fox_col, gate_s5_col, wpf, wps, bgf, bgs, cast_w, *, bm=512, bn=512):
    m, d = u.shape
    n = wpf.shape[1]
    nn = n // bn
    act = lambda a: pl.BlockSpec((bm, a.shape[1]), lambda i, j: (i, 0))
    wgt = lambda w: pl.BlockSpec((w.shape[0], bn), lambda i, j: (0, j))
    gate = lambda col: pl.BlockSpec((d, bn), lambda i, j: (0, col // bn + j))
    cast_spec, cast_shape = _cast_job(cast_w, (m // bm) * nn, lambda i, j: i * nn + j)
    return pl.pallas_call(
        _merge_kernel,
        grid=(m // bm, nn),
        in_specs=[act(u), act(attn), act(ssm), gate(gate_fox_col), gate(gate_s5_col), wgt(wpf), wgt(wps),
                  pl.BlockSpec((1, bn), lambda i, j: (0, j)),
                  pl.BlockSpec((1, bn), lambda i, j: (0, j)),
                  cast_spec],
        out_specs=[pl.BlockSpec((bm, bn), lambda i, j: (i, j)), cast_spec],
        out_shape=[jax.ShapeDtypeStruct((m, n), BF16), cast_shape],
        compiler_params=_params("parallel", "arbitrary"),
        name="gated_merge",
    )(u, attn, ssm, w_in_b, w_in_b, wpf, wps, bgf, bgs, cast_w)


NORM_LANES = 128


def _out_proj_kernel(a_ref, w_ref, x_ref, g_ref, h_ref, hg_ref, r_ref, ssq_ref, *, d_model):
    j = pl.program_id(1)

    @pl.when(j == 0)
    def _():
        ssq_ref[...] = jnp.zeros_like(ssq_ref)

    h = x_ref[...] + _dot(a_ref[...], w_ref[...])
    h_ref[...] = h
    hg_ref[...] = (h * g_ref[...]).astype(hg_ref.dtype)
    ssq_ref[...] += jnp.sum(h * h, axis=-1, keepdims=True)

    @pl.when(j == pl.num_programs(1) - 1)
    def _():
        r_ref[...] = jnp.broadcast_to(lax.rsqrt(ssq_ref[...] * (1.0 / d_model) + RMS_EPS), r_ref.shape)


def _out_proj(a, w, x, g, *, bm, bn):
    m, k = a.shape
    n = w.shape[1]
    return pl.pallas_call(
        functools.partial(_out_proj_kernel, d_model=n),
        grid=(m // bm, n // bn),
        in_specs=[pl.BlockSpec((bm, k), lambda i, j: (i, 0)),
                  pl.BlockSpec((k, bn), lambda i, j: (0, j)),
                  pl.BlockSpec((bm, bn), lambda i, j: (i, j)),
                  pl.BlockSpec((1, bn), lambda i, j: (0, j))],
        out_specs=[pl.BlockSpec((bm, bn), lambda i, j: (i, j)),
                   pl.BlockSpec((bm, bn), lambda i, j: (i, j)),
                   pl.BlockSpec((bm, NORM_LANES), lambda i, j: (i, 0))],
        out_shape=[jax.ShapeDtypeStruct((m, n), F32),
                   jax.ShapeDtypeStruct((m, n), BF16),
                   jax.ShapeDtypeStruct((m, NORM_LANES), F32)],
        scratch_shapes=[pltpu.VMEM((bm, 1), F32)],
        compiler_params=_params("parallel", "arbitrary"),
        name="out_proj_residual",
    )(a, w, x, g)


def _round_weights_once(w_ref, wb_ref):
    @pl.when(pl.program_id(1) == 0)
    def _():
        wb_ref[...] = w_ref[...].astype(BF16)


def _swiglu_kernel(a_ref, r_ref, wg_ref, wu_ref, wc_ref, o_ref, wc_out_ref, wgb_ref, wub_ref):
    _round_weights_once(wg_ref, wgb_ref)
    _round_weights_once(wu_ref, wub_ref)
    _run_cast_jobs((wc_ref,), (wc_out_ref,))
    a = a_ref[...]
    r = jnp.concatenate([r_ref[...]] * (o_ref.shape[1] // NORM_LANES), axis=1)
    gate = _dot(a, wgb_ref[...]) * r
    up = _dot(a, wub_ref[...]) * r
    o_ref[...] = (jax.nn.silu(gate) * up).astype(o_ref.dtype)


def _swiglu_up(a, r, w_gate_up, d_ff, cast_w, *, bm, bn):
    m, k = a.shape
    nj, ni = d_ff // bn, m // bm
    cast_spec, cast_shape = _cast_job(cast_w, nj * ni, lambda j, i: j * ni + i)
    return pl.pallas_call(
        _swiglu_kernel,
        grid=(nj, ni),
        in_specs=[pl.BlockSpec((bm, k), lambda j, i: (i, 0)),
                  pl.BlockSpec((bm, NORM_LANES), lambda j, i: (i, 0)),
                  pl.BlockSpec((k, bn), lambda j, i: (0, j)),
                  pl.BlockSpec((k, bn), lambda j, i: (0, j + nj)),
                  cast_spec],
        out_specs=[pl.BlockSpec((bm, bn), lambda j, i: (i, j)), cast_spec],
        out_shape=[jax.ShapeDtypeStruct((m, d_ff), BF16), cast_shape],
        scratch_shapes=[pltpu.VMEM((k, bn), BF16)] * 2,
        compiler_params=_params("parallel", "arbitrary"),
        name="swiglu_up",
    )(a, r, w_gate_up, w_gate_up, cast_w)


def _down_kernel(a_ref, w_ref, r_ref, o_ref):
    o_ref[...] = r_ref[...] + _dot(a_ref[...], w_ref[...])


def _down_residual(a, w, res, *, bm, bn):
    m, k = a.shape
    n = w.shape[1]
    return pl.pallas_call(
        _down_kernel,
        grid=(m // bm, n // bn),
        in_specs=[pl.BlockSpec((bm, k), lambda i, j: (i, 0)),
                  pl.BlockSpec((k, bn), lambda i, j: (0, j)),
                  pl.BlockSpec((bm, bn), lambda i, j: (i, j))],
        out_specs=pl.BlockSpec((bm, bn), lambda i, j: (i, j)),
        out_shape=jax.ShapeDtypeStruct((m, n), F32),
        compiler_params=_params("parallel", "arbitrary"),
        name="down_proj_residual",
    )(a, w, res)


def _layer(x, batch, seq, g_mix, w_in, b_fgate, b_gates, q_norm, k_norm,
           lam_re, lam_im, log_step, b_re, b_im, c_re, c_im, s5_d,
           w_glu, b_glu, w_proj_fox, w_proj_s5, w_out, g_ffn, w_gate_up, w_down):
    m, d_model = x.shape
    fox_w = FOX_HEADS * FOX_HEAD_DIM
    s5_w = S5_GROUP * S5_GROUPS
    col_k, col_v = fox_w, 2 * fox_w
    col_f = 3 * fox_w
    col_s5 = col_f + FOX_HEADS
    col_g = col_s5 + s5_w
    d_ff = w_down.shape[0]
    pad_s5 = -col_s5 % s5_w
    w_in_b = jnp.concatenate([w_in[:, :col_s5], jnp.zeros((d_model, pad_s5), w_in.dtype), w_in[:, col_s5:]],
                             axis=1).astype(BF16)
    col_s5_b, col_g_b = col_s5 + pad_s5, col_g + pad_s5
    w_f_t = w_in[:, col_f:col_s5].T.astype(BF16)

    u = _rmsnorm(x, g_mix)

    head_scale = jnp.concatenate([
        jnp.tile(q_norm.astype(F32), FOX_HEADS) * (LOG2_E / math.sqrt(FOX_HEAD_DIM)),
        jnp.tile(k_norm.astype(F32), FOX_HEADS),
        jnp.ones((fox_w,), F32)]).reshape(1, 3 * fox_w)
    qkv = _qkv_proj(u, w_in_b, head_scale, bm=1024, bn=1024)
    s5_in, f_t, w_proj_fox_b, w_proj_s5_b = _s5_fgate_proj(
        u, w_in_b, col_s5_b, s5_w, w_f_t, b_fgate.astype(F32).reshape(FOX_HEADS, 1), batch, seq,
        w_proj_fox, w_proj_s5)

    blk = 512
    f_rows = f_t.reshape(batch, FOX_HEADS, seq // blk, 1, blk)
    qk_bound = (1.02 * LOG2_E * math.sqrt(FOX_HEAD_DIM)
                * jnp.max(jnp.abs(q_norm.astype(F32))) * jnp.max(jnp.abs(k_norm.astype(F32))))
    attn = _attention(qkv.reshape(batch, seq, 3 * fox_w), f_rows, qk_bound, blk=blk).reshape(m, fox_w)

    y5 = _s5(s5_in, lam_re, lam_im, log_step, b_re, b_im, c_re, c_im, s5_d, batch=batch)
    ssm = _glu(y5, w_glu.astype(BF16), b_glu.astype(F32).reshape(1, s5_w))

    b_gates = b_gates.astype(F32).reshape(1, 2 * d_model)
    merged, w_out_b = _merge(u, attn, ssm, w_in_b, col_g_b, col_g_b + d_model, w_proj_fox_b, w_proj_s5_b,
                             b_gates[:, :d_model], b_gates[:, d_model:], w_out)
    h, hg, h_inv_rms = _out_proj(merged, w_out_b, x, g_ffn.astype(F32).reshape(1, d_model), bm=1024, bn=512)
    act, w_down_b = _swiglu_up(hg, h_inv_rms, w_gate_up, d_ff, w_down, bm=1024, bn=256)
    return _down_residual(act, w_down_b, h, bm=512, bn=512)


def kernel(x, g_mix, w_in, b_fgate, b_gates, q_norm, k_norm, s5_lambda_re, s5_lambda_im, s5_log_step, s5_b_re, s5_b_im, s5_c_re, s5_c_im, s5_d, w_glu, b_glu, w_proj_fox, w_proj_s5, w_out, g_ffn, w_gate_up, w_down):
    batch, seq, d_model = x.shape
    h = x.reshape(batch * seq, d_model)
    for l in range(g_mix.shape[0]):
        h = _layer(h, batch, seq, g_mix[l], w_in[l], b_fgate[l], b_gates[l], q_norm[l], k_norm[l],
                   s5_lambda_re[l], s5_lambda_im[l], s5_log_step[l], s5_b_re[l], s5_b_im[l],
                   s5_c_re[l], s5_c_im[l], s5_d[l], w_glu[l], b_glu[l],
                   w_proj_fox[l], w_proj_s5[l], w_out[l], g_ffn[l], w_gate_up[l], w_down[l])
    return h.reshape(batch, seq, d_model)
```

```python
import functools
import math

import jax
import jax.numpy as jnp
from jax import lax
from jax.experimental import pallas as pl
from jax.experimental.pallas import tpu as pltpu

F32 = jnp.float32
BF16 = jnp.bfloat16

FOX_HEADS = 16
FOX_HEAD_DIM = 128
S5_GROUP = 16
S5_GROUPS = 64
S5_STATE = 64
S5_CHUNK = 16
RMS_EPS = 1e-6
MASK_VALUE = -1e30
LOG2_E = math.log2(math.e)

VMEM_LIMIT_BYTES = 56 * 1024 * 1024


def _params(*semantics):
    return pltpu.CompilerParams(dimension_semantics=semantics,
                                vmem_limit_bytes=VMEM_LIMIT_BYTES)


def _dot(a, b):
    return jnp.dot(a, b, preferred_element_type=F32)


def _cast_job(w, steps, step_index):
    rows = w.shape[0] // steps
    assert rows * steps == w.shape[0] and rows % 16 == 0, (w.shape, steps)
    return (pl.BlockSpec((rows, w.shape[1]), lambda *g: (step_index(*g), 0)),
            jax.ShapeDtypeStruct(w.shape, BF16))


def _run_cast_jobs(src_refs, dst_refs):
    for src, dst in zip(src_refs, dst_refs):
        dst[...] = src[...].astype(dst.dtype)


def _rmsnorm_kernel(x_ref, g_ref, o_ref):
    x = x_ref[...]
    ms = jnp.mean(x * x, axis=-1, keepdims=True)
    o_ref[...] = (x * lax.rsqrt(ms + RMS_EPS) * g_ref[...]).astype(o_ref.dtype)


def _rmsnorm(x, g, *, bm=256):
    m, d = x.shape
    return pl.pallas_call(
        _rmsnorm_kernel,
        grid=(m // bm,),
        in_specs=[pl.BlockSpec((bm, d), lambda i: (i, 0)),
                  pl.BlockSpec((1, d), lambda i: (0, 0))],
        out_specs=pl.BlockSpec((bm, d), lambda i: (i, 0)),
        out_shape=jax.ShapeDtypeStruct((m, d), BF16),
        compiler_params=_params("parallel"),
        name="rmsnorm",
    )(x, g.reshape(1, d).astype(F32))


def _qkv_kernel(a_ref, w_ref, s_ref, o_ref, *, qk_tiles):
    acc = _dot(a_ref[...], w_ref[...])
    is_qk = pl.program_id(1) < qk_tiles
    for h in range(acc.shape[1] // FOX_HEAD_DIM):
        cols = slice(h * FOX_HEAD_DIM, (h + 1) * FOX_HEAD_DIM)
        blk = acc[:, cols]
        ms = jnp.mean(blk * blk, axis=-1, keepdims=True)
        inv = jnp.where(is_qk, lax.rsqrt(ms + RMS_EPS), 1.0)
        o_ref[:, cols] = (blk * inv * s_ref[:, cols]).astype(o_ref.dtype)


def _qkv_proj(a, w_in, scale, *, bm, bn):
    m, k = a.shape
    n = scale.shape[1]
    qk_tiles = 2 * FOX_HEADS * FOX_HEAD_DIM // bn
    return pl.pallas_call(
        functools.partial(_qkv_kernel, qk_tiles=qk_tiles),
        grid=(m // bm, n // bn),
        in_specs=[pl.BlockSpec((bm, k), lambda i, j: (i, 0)),
                  pl.BlockSpec((k, bn), lambda i, j: (0, j)),
                  pl.BlockSpec((1, bn), lambda i, j: (0, j))],
        out_specs=pl.BlockSpec((bm, bn), lambda i, j: (i, j)),
        out_shape=jax.ShapeDtypeStruct((m, n), BF16),
        compiler_params=_params("parallel", "arbitrary"),
        name="qkv_proj",
    )(a, w_in, scale)


def _s5_fgate_kernel(u_ref, ws_ref, wt_ref, b_ref, wa_ref, wb_ref, s_ref, f_ref, wa_out_ref, wb_out_ref,
                     carry_ref):
    @pl.when(pl.program_id(1) == 0)
    def _():
        carry_ref[...] = jnp.zeros_like(carry_ref)

    _run_cast_jobs((wa_ref, wb_ref), (wa_out_ref, wb_out_ref))

    u = u_ref[...]
    s_ref[...] = _dot(u, ws_ref[...])
    z = lax.dot_general(wt_ref[...], u, (((1,), (1,)), ((), ())),
                        preferred_element_type=F32) + b_ref[...]
    x = jnp.minimum(z, 0.0) - jnp.log1p(jnp.exp(-jnp.abs(z)))
    bs = x.shape[1]
    lane = lax.broadcasted_iota(jnp.int32, x.shape, 1)
    shift = 1
    while shift < bs:
        x = x + jnp.where(lane >= shift, pltpu.roll(x, shift, 1), 0.0)
        shift *= 2
    x = x + carry_ref[:, 0:1]
    f_ref[...] = x * LOG2_E
    carry_ref[...] = jnp.broadcast_to(x[:, bs - 1:bs], carry_ref.shape)


def _s5_fgate_proj(u, w_s5, wt, bias, batch, seq, cast_a, cast_b, *, bs=1024):
    h, d = wt.shape
    n = w_s5.shape[1]
    ns = seq // bs
    step = lambda b, j: b * ns + j
    (spec_a, shape_a), (spec_b, shape_b) = (_cast_job(w, batch * ns, step) for w in (cast_a, cast_b))
    return pl.pallas_call(
        _s5_fgate_kernel,
        grid=(batch, ns),
        in_specs=[pl.BlockSpec((bs, d), lambda b, j: (b * ns + j, 0)),
                  pl.BlockSpec((d, n), lambda b, j: (0, 0)),
                  pl.BlockSpec((h, d), lambda b, j: (0, 0)),
                  pl.BlockSpec((h, 1), lambda b, j: (0, 0)),
                  spec_a, spec_b],
        out_specs=[pl.BlockSpec((bs, n), lambda b, j: (b * ns + j, 0)),
                   pl.BlockSpec((None, h, bs), lambda b, j: (b, 0, j)),
                   spec_a, spec_b],
        out_shape=[jax.ShapeDtypeStruct((batch * seq, n), F32),
                   jax.ShapeDtypeStruct((batch, h, seq), F32),
                   shape_a, shape_b],
        scratch_shapes=[pltpu.VMEM((h, 128), F32)],
        compiler_params=_params("parallel", "arbitrary"),
        name="s5_in_fgate_proj",
    )(u, w_s5, wt, bias, cast_a, cast_b)


ATTN_HEADS_PER_STEP = 4


def _attn_kernel(q_ref, k_ref, v_ref, f_ref, o_ref, *, blk):
    i = pl.program_id(2)
    dh = FOX_HEAD_DIM
    heads = range(ATTN_HEADS_PER_STEP)
    qs = [q_ref[:, h * dh:(h + 1) * dh] for h in heads]

    def block(j, carry, masked):
        start = pl.multiple_of(j * blk, blk)
        out = []
        for h in heads:
            m, l, acc = carry[h]
            ks = k_ref[pl.ds(start, blk), h * dh:(h + 1) * dh]
            vs = v_ref[pl.ds(start, blk), h * dh:(h + 1) * dh]
            s = lax.dot_general(qs[h], ks, (((1,), (1,)), ((), ())), preferred_element_type=F32)
            s = s - f_ref[h, j]
            if masked:
                row = lax.broadcasted_iota(jnp.int32, s.shape, 0)
                col = lax.broadcasted_iota(jnp.int32, s.shape, 1)
                s = jnp.where(col <= row, s, MASK_VALUE)
            m_new = jnp.maximum(m, jnp.max(s, axis=-1, keepdims=True))
            alpha = jnp.exp2(m - m_new)
            p = jnp.exp2(s - m_new)
            l = alpha * l + jnp.sum(p, axis=-1, keepdims=True)
            acc = alpha * acc + _dot(p.astype(BF16), vs)
            out.append((m_new, l, acc))
        return tuple(out)

    init = tuple((jnp.full((blk, 1), MASK_VALUE, F32), jnp.zeros((blk, 1), F32),
                  jnp.zeros((blk, dh), F32)) for _ in heads)
    carry = lax.fori_loop(0, i, lambda j, c: block(j, c, False), init)
    carry = block(i, carry, True)
    for h in heads:
        _, l, acc = carry[h]
        o_ref[:, h * dh:(h + 1) * dh] = (acc / l).astype(o_ref.dtype)


def _attn_shift_kernel(c_ref, q_ref, k_ref, v_ref, f_ref, o_ref, acc_ref, shift_ref, p_ref, *, blk):
    i = pl.program_id(2)
    dh = FOX_HEAD_DIM
    heads = range(ATTN_HEADS_PER_STEP)
    qs = [q_ref[:, h * dh:(h + 1) * dh] for h in heads]
    ones = jnp.ones((blk, dh), BF16)

    def setup():
        for h in heads:
            col = jnp.broadcast_to(f_ref[h, i], (8, blk)).T[:, 0:1] - c_ref[0]
            shift_ref[h] = jnp.broadcast_to(col, (blk, dh))
        acc_ref[...] = jnp.zeros_like(acc_ref)

    def probabilities(h, j, masked):
        start = pl.multiple_of(j * blk, blk)
        ks = k_ref[pl.ds(start, blk), h * dh:(h + 1) * dh]
        s = lax.dot_general(qs[h], ks, (((1,), (1,)), ((), ())), preferred_element_type=F32)
        s = (s + jnp.concatenate([shift_ref[h]] * (blk // dh), axis=1)) - f_ref[h, j]
        if masked:
            row = lax.broadcasted_iota(jnp.int32, s.shape, 0)
            col = lax.broadcasted_iota(jnp.int32, s.shape, 1)
            s = jnp.where(col <= row, s, MASK_VALUE)
        return jnp.exp2(s).astype(BF16)

    def values(h, j):
        start = pl.multiple_of(j * blk, blk)
        return jnp.concatenate([v_ref[pl.ds(start, blk), h * dh:(h + 1) * dh], ones], axis=1)

    setup()

    @pl.when(i == 0)
    def _():
        p_ref[...] = jnp.zeros_like(p_ref)

    @pl.when(i > 0)
    def _():
        for h in heads:
            p_ref[h] = probabilities(h, 0, False)

    @pl.loop(1, i)
    def _(j):
        for h in heads:
            acc_ref[h] += _dot(p_ref[h], values(h, j - 1))
        for h in heads:
            p_ref[h] = probabilities(h, j, False)

    prev = jnp.maximum(i - 1, 0)
    for h in heads:
        acc_ref[h] += _dot(p_ref[h], values(h, prev)) + _dot(probabilities(h, i, True), values(h, i))
    for h in heads:
        acc = acc_ref[h]
        o_ref[:, h * dh:(h + 1) * dh] = (acc[:, :dh] / acc[:, dh:]).astype(o_ref.dtype)


ATTN_SHIFT_LIMIT = 56.0


def _attention(qkv, f_rows, qk_bound, *, blk=512):
    batch, seq, _ = qkv.shape
    nblk = seq // blk
    hps = ATTN_HEADS_PER_STEP
    wide = hps * FOX_HEAD_DIM
    third = FOX_HEADS // hps
    specs = [pl.BlockSpec((None, blk, wide), lambda b, h, i: (b, i, h)),
             pl.BlockSpec((None, seq, wide), lambda b, h, i: (b, 0, h + third)),
             pl.BlockSpec((None, seq, wide), lambda b, h, i: (b, 0, h + 2 * third)),
             pl.BlockSpec((None, hps, nblk, 1, blk), lambda b, h, i: (b, h, 0, 0, 0))]
    common = dict(
        grid=(batch, FOX_HEADS // hps, nblk),
        out_specs=pl.BlockSpec((None, blk, wide), lambda b, h, i: (b, i, h)),
        out_shape=jax.ShapeDtypeStruct((batch, seq, FOX_HEADS * FOX_HEAD_DIM), BF16),
        compiler_params=_params("parallel", "parallel", "arbitrary"))

    def shifted(qkv, f_rows, c):
        return pl.pallas_call(
            functools.partial(_attn_shift_kernel, blk=blk),
            in_specs=[pl.BlockSpec(memory_space=pltpu.SMEM)] + specs,
            scratch_shapes=[pltpu.VMEM((hps, blk, 2 * FOX_HEAD_DIM), F32),
                            pltpu.VMEM((hps, blk, FOX_HEAD_DIM), F32),
                            pltpu.VMEM((hps, blk, blk), BF16)],
            name="fox_attention_shift", **common)(c.reshape(1), qkv, qkv, qkv, f_rows)

    def online(qkv, f_rows, c):
        return pl.pallas_call(functools.partial(_attn_kernel, blk=blk), in_specs=specs,
                              name="fox_attention_online", **common)(qkv, qkv, qkv, f_rows)

    return lax.cond(qk_bound <= ATTN_SHIFT_LIMIT, shifted, online, qkv, f_rows, qk_bound)


S5_SLAB_GROUPS = 16
S5_SLAB = S5_SLAB_GROUPS * S5_GROUP
S5_SLAB_STATES = S5_SLAB_GROUPS * S5_STATE
S5_SLABS = S5_GROUPS // S5_SLAB_GROUPS


def _split_bf16(a):
    hi = a.astype(BF16)
    return hi, (a - hi.astype(F32)).astype(BF16)


def _dot_split(a, b_hi, b_lo):
    a_hi, a_lo = _split_bf16(a)
    return _dot(a_hi, b_hi) + _dot(a_hi, b_lo) + _dot(a_lo, b_hi)


def _lam_pow(k, lr, li, dt):
    mag = jnp.exp(k * (lr * dt))
    ang = k * (li * dt)
    return mag * jnp.cos(ang), mag * jnp.sin(ang)


def _s5_matrices_kernel(lam_ref, ls_ref, bdb_ref, bdc_ref, bp_ref, cp_ref, gr_ref, l16_ref,
                        bb_ref, chi_ref, clo_ref):
    t = pl.program_id(1)
    ns = S5_SLAB_STATES
    lr, li = lam_ref[0:1, :], lam_ref[1:2, :]
    dt = jnp.exp(ls_ref[...])

    @pl.when(t == 0)
    def _():
        lbr, lbi = _lam_pow(1.0, lr, li, dt)
        den = lr * lr + li * li
        nr = lbr - 1.0
        fr = (nr * lr + lbi * li) / den
        fi = (lbi * lr - nr * li) / den
        b_re, b_im = bdb_ref[0], bdb_ref[1]
        bb_ref[0] = fr * b_re - fi * b_im
        bb_ref[1] = fr * b_im + fi * b_re
        for part in range(2):
            chi_ref[part], clo_ref[part] = _split_bf16(bdc_ref[part])

    bb_re, bb_im = bb_ref[0], bb_ref[1]
    pr, pi = _lam_pow((S5_CHUNK - 1 - t).astype(F32), lr, li, dt)
    bp_re = bb_re * pr - bb_im * pi
    bp_im = bb_re * pi + bb_im * pr
    bp_ref[:, :ns] = bp_re.astype(BF16)
    bp_ref[:, ns:] = bp_im.astype(BF16)

    qr, qi = _lam_pow((t + 1).astype(F32), lr, li, dt)
    qr = jnp.broadcast_to(qr, (8, ns)).T[:, 0:1]
    qi = jnp.broadcast_to(qi, (8, ns)).T[:, 0:1]
    c_re, c_im = bdc_ref[0], bdc_ref[1]
    cp_ref[:ns, :] = (c_re * qr - c_im * qi).astype(BF16)
    cp_ref[ns:, :] = (-(c_re * qi + c_im * qr)).astype(BF16)

    gr_ref[...] = (_dot_split(bp_re, chi_ref[0], clo_ref[0])
                   - _dot_split(bp_im, chi_ref[1], clo_ref[1])).astype(BF16)

    l16r, l16i = _lam_pow(float(S5_CHUNK), lr, li, dt)
    l16_ref[0:1, :] = l16r
    l16_ref[1:2, :] = l16i


def _s5_matrices(lam, ls, bdb, bdc):
    t_len, w, ns = S5_CHUNK, S5_SLAB, S5_SLAB_STATES
    return pl.pallas_call(
        _s5_matrices_kernel,
        grid=(S5_SLABS, t_len),
        in_specs=[pl.BlockSpec((None, 2, ns), lambda s, t: (s, 0, 0)),
                  pl.BlockSpec((None, 1, ns), lambda s, t: (s, 0, 0)),
                  pl.BlockSpec((None, 2, w, ns), lambda s, t: (s, 0, 0, 0)),
                  pl.BlockSpec((None, 2, ns, w), lambda s, t: (s, 0, 0, 0))],
        out_specs=[pl.BlockSpec((None, w, 2 * ns), lambda s, t: (s, t, 0)),
                   pl.BlockSpec((None, None, 2 * ns, w), lambda s, t: (s, t, 0, 0)),
                   pl.BlockSpec((None, w, w), lambda s, t: (s, t, 0)),
                   pl.BlockSpec((None, 2, ns), lambda s, t: (s, 0, 0))],
        out_shape=[jax.ShapeDtypeStruct((S5_SLABS, t_len * w, 2 * ns), BF16),
                   jax.ShapeDtypeStruct((S5_SLABS, t_len, 2 * ns, w), BF16),
                   jax.ShapeDtypeStruct((S5_SLABS, t_len * w, w), BF16),
                   jax.ShapeDtypeStruct((S5_SLABS, 2, ns), F32)],
        scratch_shapes=[pltpu.VMEM((2, w, ns), F32), pltpu.VMEM((2, ns, w), BF16), pltpu.VMEM((2, ns, w), BF16)],
        compiler_params=_params("parallel", "arbitrary"),
        name="s5_matrices",
    )(lam, ls, bdb, bdc)


def _chunk_rows(x_ref):
    return jnp.concatenate([x_ref[:, t, :] for t in range(S5_CHUNK)], axis=1).astype(BF16)


def _s5_states_kernel(x_ref, bp_ref, s_ref):
    s_ref[...] = _dot(_chunk_rows(x_ref), bp_ref[...])


def _s5_states(x3, bp, *, rows):
    nc = x3.shape[0]
    t_len, w, ns = S5_CHUNK, S5_SLAB, S5_SLAB_STATES
    return pl.pallas_call(
        _s5_states_kernel,
        grid=(S5_SLABS, nc // rows),
        in_specs=[pl.BlockSpec((rows, t_len, w), lambda s, r: (r, 0, s)),
                  pl.BlockSpec((None, t_len * w, 2 * ns), lambda s, r: (s, 0, 0),
                               pipeline_mode=pl.Buffered(1))],
        out_specs=pl.BlockSpec((rows, 2 * ns), lambda s, r: (r, s)),
        out_shape=jax.ShapeDtypeStruct((nc, S5_SLABS * 2 * ns), F32),
        compiler_params=_params("parallel", "arbitrary"),
        name="s5_chunk_states",
    )(x3, bp)


def _s5_chunk_scan_kernel(s_ref, l16_ref, xp_ref, *, batch, chunks):
    ns = S5_SLAB_STATES
    ar, ai = l16_ref[0:1, :], l16_ref[1:2, :]

    def step(c, carry):
        out = []
        for b, (xr, xi) in enumerate(carry):
            row = pl.ds(b * chunks + c, 1)
            xp_ref[row, :ns] = xr
            xp_ref[row, ns:] = xi
            sr = s_ref[row, :ns]
            si = s_ref[row, ns:]
            out.append((ar * xr - ai * xi + sr, ar * xi + ai * xr + si))
        return tuple(out)

    zero = jnp.zeros((1, ns), F32)
    lax.fori_loop(0, chunks, step, ((zero, zero),) * batch)


def _s5_chunk_scan(s, l16, *, batch):
    nc = s.shape[0]
    ns = S5_SLAB_STATES
    return pl.pallas_call(
        functools.partial(_s5_chunk_scan_kernel, batch=batch, chunks=nc // batch),
        grid=(S5_SLABS,),
        in_specs=[pl.BlockSpec((nc, 2 * ns), lambda s_: (0, s_)),
                  pl.BlockSpec((None, 2, ns), lambda s_: (s_, 0, 0))],
        out_specs=pl.BlockSpec((nc, 2 * ns), lambda s_: (0, s_)),
        out_shape=jax.ShapeDtypeStruct(s.shape, F32),
        compiler_params=_params("parallel"),
        name="s5_chunk_scan",
    )(s, l16)


def _s5_output_kernel(x_ref, xp_ref, gr_ref, cp_ref, d_ref, y_ref):
    t_len, w = S5_CHUNK, S5_SLAB
    xc = _chunk_rows(x_ref)
    xp = xp_ref[...].astype(BF16)
    d = d_ref[...]
    for t in range(t_len):
        y = _dot(xc[:, :(t + 1) * w], gr_ref[(t_len - 1 - t) * w:, :])
        y = y + _dot(xp, cp_ref[t])
        y_ref[:, t, :] = y + d * x_ref[:, t, :]


def _s5_output(x3, xp, gr, cp, d, *, rows):
    nc = x3.shape[0]
    t_len, w, ns = S5_CHUNK, S5_SLAB, S5_SLAB_STATES
    once = dict(pipeline_mode=pl.Buffered(1))
    return pl.pallas_call(
        _s5_output_kernel,
        grid=(S5_SLABS, nc // rows),
        in_specs=[pl.BlockSpec((rows, t_len, w), lambda s, r: (r, 0, s)),
                  pl.BlockSpec((rows, 2 * ns), lambda s, r: (r, s)),
                  pl.BlockSpec((None, t_len * w, w), lambda s, r: (s, 0, 0), **once),
                  pl.BlockSpec((None, t_len, 2 * ns, w), lambda s, r: (s, 0, 0, 0), **once),
                  pl.BlockSpec((None, 1, w), lambda s, r: (s, 0, 0))],
        out_specs=pl.BlockSpec((rows, t_len, w), lambda s, r: (r, 0, s)),
        out_shape=jax.ShapeDtypeStruct(x3.shape, F32),
        compiler_params=_params("parallel", "arbitrary"),
        name="s5_chunk_output",
    )(x3, xp, gr, cp, d)


def _s5(s5_in, lam_re, lam_im, log_step, b_re, b_im, c_re, c_im, d, *, batch):
    m, width = s5_in.shape
    sl, sg, p, grp = S5_SLABS, S5_SLAB_GROUPS, S5_STATE, S5_GROUP
    ns = S5_SLAB_STATES
    lam = jnp.stack([lam_re, lam_im], axis=0).astype(F32).reshape(2, sl, ns).transpose(1, 0, 2)
    ls = jnp.repeat(log_step.astype(F32), p).reshape(sl, 1, ns)
    eye = jnp.eye(sg, dtype=F32)
    b = jnp.stack([b_re, b_im], axis=0).astype(F32).reshape(2, sl, sg, p, grp)
    bdb = jnp.einsum('rsgpi,gh->srgihp', b, eye).reshape(sl, 2, sg * grp, ns)
    c = jnp.stack([c_re, c_im], axis=0).astype(F32).reshape(2, sl, sg, grp, p)
    bdc = jnp.einsum('rsgop,gh->srhpgo', c, eye).reshape(sl, 2, ns, sg * grp)
    bp, cp, gr, l16 = _s5_matrices(lam, ls, bdb, bdc)
    x3 = s5_in.reshape(m // S5_CHUNK, S5_CHUNK, width)
    states = _s5_states(x3, bp, rows=512)
    xprev = _s5_chunk_scan(states, l16, batch=batch)
    y3 = _s5_output(x3, xprev, gr, cp, d.astype(F32).reshape(sl, 1, sg * grp), rows=256)
    return y3.reshape(m, width)


def _glu_kernel(y_ref, w_ref, b_ref, o_ref):
    g = jax.nn.gelu(y_ref[...])
    o_ref[...] = (g * jax.nn.sigmoid(_dot(g.astype(BF16), w_ref[...]) + b_ref[...])).astype(o_ref.dtype)


def _glu(y, w, b, *, bm=512):
    m, n = y.shape
    return pl.pallas_call(
        _glu_kernel,
        grid=(m // bm,),
        in_specs=[pl.BlockSpec((bm, n), lambda i: (i, 0)),
                  pl.BlockSpec((n, n), lambda i: (0, 0)),
                  pl.BlockSpec((1, n), lambda i: (0, 0))],
        out_specs=pl.BlockSpec((bm, n), lambda i: (i, 0)),
        out_shape=jax.ShapeDtypeStruct((m, n), BF16),
        compiler_params=_params("parallel"),
        name="s5_glu",
    )(y, w, b)


def _merge_kernel(u_ref, a_ref, s_ref, wgf_ref, wgs_ref, wpf_ref, wps_ref, bgf_ref, bgs_ref, wc_ref,
                  o_ref, wc_out_ref):
    _run_cast_jobs((wc_ref,), (wc_out_ref,))
    u = u_ref[...]
    gate_fox = jax.nn.sigmoid(_dot(u, wgf_ref[...]) + bgf_ref[...])
    gate_s5 = jax.nn.sigmoid(_dot(u, wgs_ref[...]) + bgs_ref[...])
    o_ref[...] = (gate_fox * _dot(a_ref[...], wpf_ref[...])
                  + gate_s5 * _dot(s_ref[...], wps_ref[...])).astype(o_ref.dtype)


def _merge(u, attn, ssm, wgf, wgs, wpf, wps, bgf, bgs, cast_w, *, bm=512, bn=512):
    m, d = u.shape
    n = wpf.shape[1]
    nn = n // bn
    act = lambda a: pl.BlockSpec((bm, a.shape[1]), lambda i, j: (i, 0))
    wgt = lambda w: pl.BlockSpec((w.shape[0], bn), lambda i, j: (0, j))
    cast_spec, cast_shape = _cast_job(cast_w, (m // bm) * nn, lambda i, j: i * nn + j)
    return pl.pallas_call(
        _merge_kernel,
        grid=(m // bm, nn),
        in_specs=[act(u), act(attn), act(ssm), wgt(wgf), wgt(wgs), wgt(wpf), wgt(wps),
                  pl.BlockSpec((1, bn), lambda i, j: (0, j)),
                  pl.BlockSpec((1, bn), lambda i, j: (0, j)),
                  cast_spec],
        out_specs=[pl.BlockSpec((bm, bn), lambda i, j: (i, j)), cast_spec],
        out_shape=[jax.ShapeDtypeStruct((m, n), BF16), cast_shape],
        compiler_params=_params("parallel", "arbitrary"),
        name="gated_merge",
    )(u, attn, ssm, wgf, wgs, wpf, wps, bgf, bgs, cast_w)


NORM_LANES = 128


def _out_proj_kernel(a_ref, w_ref, x_ref, g_ref, h_ref, hg_ref, r_ref, ssq_ref, *, d_model):
    j = pl.program_id(1)

    @pl.when(j == 0)
    def _():
        ssq_ref[...] = jnp.zeros_like(ssq_ref)

    h = x_ref[...] + _dot(a_ref[...], w_ref[...])
    h_ref[...] = h
    hg_ref[...] = (h * g_ref[...]).astype(hg_ref.dtype)
    ssq_ref[...] += jnp.sum(h * h, axis=-1, keepdims=True)

    @pl.when(j == pl.num_programs(1) - 1)
    def _():
        r_ref[...] = jnp.broadcast_to(lax.rsqrt(ssq_ref[...] * (1.0 / d_model) + RMS_EPS), r_ref.shape)


def _out_proj(a, w, x, g, *, bm, bn):
    m, k = a.shape
    n = w.shape[1]
    return pl.pallas_call(
        functools.partial(_out_proj_kernel, d_model=n),
        grid=(m // bm, n // bn),
        in_specs=[pl.BlockSpec((bm, k), lambda i, j: (i, 0)),
                  pl.BlockSpec((k, bn), lambda i, j: (0, j)),
                  pl.BlockSpec((bm, bn), lambda i, j: (i, j)),
                  pl.BlockSpec((1, bn), lambda i, j: (0, j))],
        out_specs=[pl.BlockSpec((bm, bn), lambda i, j: (i, j)),
                   pl.BlockSpec((bm, bn), lambda i, j: (i, j)),
                   pl.BlockSpec((bm, NORM_LANES), lambda i, j: (i, 0))],
        out_shape=[jax.ShapeDtypeStruct((m, n), F32),
                   jax.ShapeDtypeStruct((m, n), BF16),
                   jax.ShapeDtypeStruct((m, NORM_LANES), F32)],
        scratch_shapes=[pltpu.VMEM((bm, 1), F32)],
        compiler_params=_params("parallel", "arbitrary"),
        name="out_proj_residual",
    )(a, w, x, g)


def _round_weights_once(w_ref, wb_ref):
    @pl.when(pl.program_id(1) == 0)
    def _():
        wb_ref[...] = w_ref[...].astype(BF16)


def _swiglu_kernel(a_ref, r_ref, wg_ref, wu_ref, wc_ref, o_ref, wc_out_ref, wgb_ref, wub_ref):
    _round_weights_once(wg_ref, wgb_ref)
    _round_weights_once(wu_ref, wub_ref)
    _run_cast_jobs((wc_ref,), (wc_out_ref,))
    a = a_ref[...]
    r = jnp.concatenate([r_ref[...]] * (o_ref.shape[1] // NORM_LANES), axis=1)
    gate = _dot(a, wgb_ref[...]) * r
    up = _dot(a, wub_ref[...]) * r
    o_ref[...] = (jax.nn.silu(gate) * up).astype(o_ref.dtype)


def _swiglu_up(a, r, w_gate_up, d_ff, cast_w, *, bm, bn):
    m, k = a.shape
    nj, ni = d_ff // bn, m // bm
    cast_spec, cast_shape = _cast_job(cast_w, nj * ni, lambda j, i: j * ni + i)
    return pl.pallas_call(
        _swiglu_kernel,
        grid=(nj, ni),
        in_specs=[pl.BlockSpec((bm, k), lambda j, i: (i, 0)),
                  pl.BlockSpec((bm, NORM_LANES), lambda j, i: (i, 0)),
                  pl.BlockSpec((k, bn), lambda j, i: (0, j)),
                  pl.BlockSpec((k, bn), lambda j, i: (0, j + nj)),
                  cast_spec],
        out_specs=[pl.BlockSpec((bm, bn), lambda j, i: (i, j)), cast_spec],
        out_shape=[jax.ShapeDtypeStruct((m, d_ff), BF16), cast_shape],
        scratch_shapes=[pltpu.VMEM((k, bn), BF16)] * 2,
        compiler_params=_params("parallel", "arbitrary"),
        name="swiglu_up",
    )(a, r, w_gate_up, w_gate_up, cast_w)


def _down_kernel(a_ref, w_ref, r_ref, o_ref):
    o_ref[...] = r_ref[...] + _dot(a_ref[...], w_ref[...])


def _down_residual(a, w, res, *, bm, bn):
    m, k = a.shape
    n = w.shape[1]
    return pl.pallas_call(
        _down_kernel,
        grid=(m // bm, n // bn),
        in_specs=[pl.BlockSpec((bm, k), lambda i, j: (i, 0)),
                  pl.BlockSpec((k, bn), lambda i, j: (0, j)),
                  pl.BlockSpec((bm, bn), lambda i, j: (i, j))],
        out_specs=pl.BlockSpec((bm, bn), lambda i, j: (i, j)),
        out_shape=jax.ShapeDtypeStruct((m, n), F32),
        compiler_params=_params("parallel", "arbitrary"),
        name="down_proj_residual",
    )(a, w, res)


def _layer(x, batch, seq, g_mix, w_in, b_fgate, b_gates, q_norm, k_norm,
           lam_re, lam_im, log_step, b_re, b_im, c_re, c_im, s5_d,
           w_glu, b_glu, w_proj_fox, w_proj_s5, w_out, g_ffn, w_gate_up, w_down):
    m, d_model = x.shape
    fox_w = FOX_HEADS * FOX_HEAD_DIM
    s5_w = S5_GROUP * S5_GROUPS
    col_k, col_v = fox_w, 2 * fox_w
    col_f = 3 * fox_w
    col_s5 = col_f + FOX_HEADS
    col_g = col_s5 + s5_w
    d_ff = w_down.shape[0]
    w_in_b = w_in.astype(BF16)
    w_f, w_s5 = w_in_b[:, col_f:col_s5], w_in_b[:, col_s5:col_g]
    w_gate_fox, w_gate_s5 = w_in_b[:, col_g:col_g + d_model], w_in_b[:, col_g + d_model:col_g + 2 * d_model]

    u = _rmsnorm(x, g_mix)

    head_scale = jnp.concatenate([
        jnp.tile(q_norm.astype(F32), FOX_HEADS) * (LOG2_E / math.sqrt(FOX_HEAD_DIM)),
        jnp.tile(k_norm.astype(F32), FOX_HEADS),
        jnp.ones((fox_w,), F32)]).reshape(1, 3 * fox_w)
    qkv = _qkv_proj(u, w_in_b, head_scale, bm=1024, bn=1024)
    s5_in, f_t, w_proj_fox_b, w_proj_s5_b = _s5_fgate_proj(
        u, w_s5, w_f.T, b_fgate.astype(F32).reshape(FOX_HEADS, 1), batch, seq, w_proj_fox, w_proj_s5)

    blk = 512
    f_rows = f_t.reshape(batch, FOX_HEADS, seq // blk, 1, blk)
    qk_bound = (1.02 * LOG2_E * math.sqrt(FOX_HEAD_DIM)
                * jnp.max(jnp.abs(q_norm.astype(F32))) * jnp.max(jnp.abs(k_norm.astype(F32))))
    attn = _attention(qkv.reshape(batch, seq, 3 * fox_w), f_rows, qk_bound, blk=blk).reshape(m, fox_w)

    y5 = _s5(s5_in, lam_re, lam_im, log_step, b_re, b_im, c_re, c_im, s5_d, batch=batch)
    ssm = _glu(y5, w_glu.astype(BF16), b_glu.astype(F32).reshape(1, s5_w))

    b_gates = b_gates.astype(F32).reshape(1, 2 * d_model)
    merged, w_out_b = _merge(u, attn, ssm, w_gate_fox, w_gate_s5, w_proj_fox_b, w_proj_s5_b,
                             b_gates[:, :d_model], b_gates[:, d_model:], w_out)
    h, hg, h_inv_rms = _out_proj(merged, w_out_b, x, g_ffn.astype(F32).reshape(1, d_model), bm=1024, bn=512)
    act, w_down_b = _swiglu_up(hg, h_inv_rms, w_gate_up, d_ff, w_down, bm=1024, bn=256)
    return _down_residual(act, w_down_b, h, bm=512, bn=512)


def kernel(x, g_mix, w_in, b_fgate, b_gates, q_norm, k_norm, s5_lambda_re, s5_lambda_im, s5_log_step, s5_b_re, s5_b_im, s5_c_re, s5_c_im, s5_d, w_glu, b_glu, w_proj_fox, w_proj_s5, w_out, g_ffn, w_gate_up, w_down):
    batch, seq, d_model = x.shape
    h = x.reshape(batch * seq, d_model)
    for l in range(g_mix.shape[0]):
        h = _layer(h, batch, seq, g_mix[l], w_in[l], b_fgate[l], b_gates[l], q_norm[l], k_norm[l],
                   s5_lambda_re[l], s5_lambda_im[l], s5_log_step[l], s5_b_re[l], s5_b_im[l],
                   s5_c_re[l], s5_c_im[l], s5_d[l], w_glu[l], b_glu[l],
                   w_proj_fox[l], w_proj_s5[l], w_out[l], g_ffn[l], w_gate_up[l], w_down[l])
    return h.reshape(batch, seq, d_model)
```

```python
import functools
import math

import jax
import jax.numpy as jnp
from jax import lax
from jax.experimental import pallas as pl
from jax.experimental.pallas import tpu as pltpu

F32 = jnp.float32
BF16 = jnp.bfloat16

FOX_HEADS = 16
FOX_HEAD_DIM = 128
S5_GROUP = 16
S5_GROUPS = 64
S5_STATE = 64
S5_CHUNK = 16
RMS_EPS = 1e-6
MASK_VALUE = -1e30
LOG2_E = math.log2(math.e)

VMEM_LIMIT_BYTES = 56 * 1024 * 1024


def _params(*semantics):
    return pltpu.CompilerParams(dimension_semantics=semantics,
                                vmem_limit_bytes=VMEM_LIMIT_BYTES)


def _dot(a, b):
    return jnp.dot(a, b, preferred_element_type=F32)


def _cast_job(w, steps, step_index):
    rows = w.shape[0] // steps
    assert rows * steps == w.shape[0] and rows % 16 == 0, (w.shape, steps)
    return (pl.BlockSpec((rows, w.shape[1]), lambda *g: (step_index(*g), 0)),
            jax.ShapeDtypeStruct(w.shape, BF16))


def _run_cast_jobs(src_refs, dst_refs):
    for src, dst in zip(src_refs, dst_refs):
        dst[...] = src[...].astype(dst.dtype)


def _rmsnorm_kernel(x_ref, g_ref, o_ref):
    x = x_ref[...]
    ms = jnp.mean(x * x, axis=-1, keepdims=True)
    o_ref[...] = (x * lax.rsqrt(ms + RMS_EPS) * g_ref[...]).astype(o_ref.dtype)


def _rmsnorm(x, g, *, bm=256):
    m, d = x.shape
    return pl.pallas_call(
        _rmsnorm_kernel,
        grid=(m // bm,),
        in_specs=[pl.BlockSpec((bm, d), lambda i: (i, 0)),
                  pl.BlockSpec((1, d), lambda i: (0, 0))],
        out_specs=pl.BlockSpec((bm, d), lambda i: (i, 0)),
        out_shape=jax.ShapeDtypeStruct((m, d), BF16),
        compiler_params=_params("parallel"),
        name="rmsnorm",
    )(x, g.reshape(1, d).astype(F32))


def _qkv_kernel(a_ref, w_ref, s_ref, o_ref, *, qk_tiles):
    acc = _dot(a_ref[...], w_ref[...])
    is_qk = pl.program_id(1) < qk_tiles
    for h in range(acc.shape[1] // FOX_HEAD_DIM):
        cols = slice(h * FOX_HEAD_DIM, (h + 1) * FOX_HEAD_DIM)
        blk = acc[:, cols]
        ms = jnp.mean(blk * blk, axis=-1, keepdims=True)
        inv = jnp.where(is_qk, lax.rsqrt(ms + RMS_EPS), 1.0)
        o_ref[:, cols] = (blk * inv * s_ref[:, cols]).astype(o_ref.dtype)


def _qkv_proj(a, w_in, scale, *, bm, bn):
    m, k = a.shape
    n = scale.shape[1]
    qk_tiles = 2 * FOX_HEADS * FOX_HEAD_DIM // bn
    return pl.pallas_call(
        functools.partial(_qkv_kernel, qk_tiles=qk_tiles),
        grid=(m // bm, n // bn),
        in_specs=[pl.BlockSpec((bm, k), lambda i, j: (i, 0)),
                  pl.BlockSpec((k, bn), lambda i, j: (0, j)),
                  pl.BlockSpec((1, bn), lambda i, j: (0, j))],
        out_specs=pl.BlockSpec((bm, bn), lambda i, j: (i, j)),
        out_shape=jax.ShapeDtypeStruct((m, n), BF16),
        compiler_params=_params("parallel", "arbitrary"),
        name="qkv_proj",
    )(a, w_in, scale)


def _s5_fgate_kernel(u_ref, ws_ref, wt_ref, b_ref, wa_ref, wb_ref, s_ref, f_ref, wa_out_ref, wb_out_ref,
                     carry_ref):
    @pl.when(pl.program_id(1) == 0)
    def _():
        carry_ref[...] = jnp.zeros_like(carry_ref)

    _run_cast_jobs((wa_ref, wb_ref), (wa_out_ref, wb_out_ref))

    u = u_ref[...]
    s_ref[...] = _dot(u, ws_ref[...])
    z = lax.dot_general(wt_ref[...], u, (((1,), (1,)), ((), ())),
                        preferred_element_type=F32) + b_ref[...]
    x = jnp.minimum(z, 0.0) - jnp.log1p(jnp.exp(-jnp.abs(z)))
    bs = x.shape[1]
    lane = lax.broadcasted_iota(jnp.int32, x.shape, 1)
    shift = 1
    while shift < bs:
        x = x + jnp.where(lane >= shift, pltpu.roll(x, shift, 1), 0.0)
        shift *= 2
    x = x + carry_ref[:, 0:1]
    f_ref[...] = x * LOG2_E
    carry_ref[...] = jnp.broadcast_to(x[:, bs - 1:bs], carry_ref.shape)


def _s5_fgate_proj(u, w_s5, wt, bias, batch, seq, cast_a, cast_b, *, bs=1024):
    h, d = wt.shape
    n = w_s5.shape[1]
    ns = seq // bs
    step = lambda b, j: b * ns + j
    (spec_a, shape_a), (spec_b, shape_b) = (_cast_job(w, batch * ns, step) for w in (cast_a, cast_b))
    return pl.pallas_call(
        _s5_fgate_kernel,
        grid=(batch, ns),
        in_specs=[pl.BlockSpec((bs, d), lambda b, j: (b * ns + j, 0)),
                  pl.BlockSpec((d, n), lambda b, j: (0, 0)),
                  pl.BlockSpec((h, d), lambda b, j: (0, 0)),
                  pl.BlockSpec((h, 1), lambda b, j: (0, 0)),
                  spec_a, spec_b],
        out_specs=[pl.BlockSpec((bs, n), lambda b, j: (b * ns + j, 0)),
                   pl.BlockSpec((None, h, bs), lambda b, j: (b, 0, j)),
                   spec_a, spec_b],
        out_shape=[jax.ShapeDtypeStruct((batch * seq, n), F32),
                   jax.ShapeDtypeStruct((batch, h, seq), F32),
                   shape_a, shape_b],
        scratch_shapes=[pltpu.VMEM((h, 128), F32)],
        compiler_params=_params("parallel", "arbitrary"),
        name="s5_in_fgate_proj",
    )(u, w_s5, wt, bias, cast_a, cast_b)


ATTN_HEADS_PER_STEP = 4


def _attn_kernel(q_ref, k_ref, v_ref, f_ref, o_ref, *, blk):
    i = pl.program_id(2)
    dh = FOX_HEAD_DIM
    heads = range(ATTN_HEADS_PER_STEP)
    qs = [q_ref[:, h * dh:(h + 1) * dh] for h in heads]

    def block(j, carry, masked):
        start = pl.multiple_of(j * blk, blk)
        out = []
        for h in heads:
            m, l, acc = carry[h]
            ks = k_ref[pl.ds(start, blk), h * dh:(h + 1) * dh]
            vs = v_ref[pl.ds(start, blk), h * dh:(h + 1) * dh]
            s = lax.dot_general(qs[h], ks, (((1,), (1,)), ((), ())), preferred_element_type=F32)
            s = s - f_ref[h, j]
            if masked:
                row = lax.broadcasted_iota(jnp.int32, s.shape, 0)
                col = lax.broadcasted_iota(jnp.int32, s.shape, 1)
                s = jnp.where(col <= row, s, MASK_VALUE)
            m_new = jnp.maximum(m, jnp.max(s, axis=-1, keepdims=True))
            alpha = jnp.exp2(m - m_new)
            p = jnp.exp2(s - m_new)
            l = alpha * l + jnp.sum(p, axis=-1, keepdims=True)
            acc = alpha * acc + _dot(p.astype(BF16), vs)
            out.append((m_new, l, acc))
        return tuple(out)

    init = tuple((jnp.full((blk, 1), MASK_VALUE, F32), jnp.zeros((blk, 1), F32),
                  jnp.zeros((blk, dh), F32)) for _ in heads)
    carry = lax.fori_loop(0, i, lambda j, c: block(j, c, False), init)
    carry = block(i, carry, True)
    for h in heads:
        _, l, acc = carry[h]
        o_ref[:, h * dh:(h + 1) * dh] = (acc / l).astype(o_ref.dtype)


def _attn_shift_kernel(c_ref, q_ref, k_ref, v_ref, f_ref, o_ref, acc_ref, shift_ref, p_ref, *, blk):
    i = pl.program_id(2)
    dh = FOX_HEAD_DIM
    heads = range(ATTN_HEADS_PER_STEP)
    qs = [q_ref[:, h * dh:(h + 1) * dh] for h in heads]
    ones = jnp.ones((blk, dh), BF16)

    def setup():
        for h in heads:
            col = jnp.broadcast_to(f_ref[h, i], (8, blk)).T[:, 0:1] - c_ref[0]
            shift_ref[h] = jnp.broadcast_to(col, (blk, dh))
        acc_ref[...] = jnp.zeros_like(acc_ref)

    def probabilities(h, j, masked):
        start = pl.multiple_of(j * blk, blk)
        ks = k_ref[pl.ds(start, blk), h * dh:(h + 1) * dh]
        s = lax.dot_general(qs[h], ks, (((1,), (1,)), ((), ())), preferred_element_type=F32)
        s = (s + jnp.concatenate([shift_ref[h]] * (blk // dh), axis=1)) - f_ref[h, j]
        if masked:
            row = lax.broadcasted_iota(jnp.int32, s.shape, 0)
            col = lax.broadcasted_iota(jnp.int32, s.shape, 1)
            s = jnp.where(col <= row, s, MASK_VALUE)
        return jnp.exp2(s).astype(BF16)

    def values(h, j):
        start = pl.multiple_of(j * blk, blk)
        return jnp.concatenate([v_ref[pl.ds(start, blk), h * dh:(h + 1) * dh], ones], axis=1)

    setup()

    @pl.when(i == 0)
    def _():
        p_ref[...] = jnp.zeros_like(p_ref)

    @pl.when(i > 0)
    def _():
        for h in heads:
            p_ref[h] = probabilities(h, 0, False)

    @pl.loop(1, i)
    def _(j):
        for h in heads:
            acc_ref[h] += _dot(p_ref[h], values(h, j - 1))
        for h in heads:
            p_ref[h] = probabilities(h, j, False)

    prev = jnp.maximum(i - 1, 0)
    for h in heads:
        acc_ref[h] += _dot(p_ref[h], values(h, prev)) + _dot(probabilities(h, i, True), values(h, i))
    for h in heads:
        acc = acc_ref[h]
        o_ref[:, h * dh:(h + 1) * dh] = (acc[:, :dh] / acc[:, dh:]).astype(o_ref.dtype)


ATTN_SHIFT_LIMIT = 56.0


def _attention(qkv, f_rows, qk_bound, *, blk=512):
    batch, seq, _ = qkv.shape
    nblk = seq // blk
    hps = ATTN_HEADS_PER_STEP
    wide = hps * FOX_HEAD_DIM
    third = FOX_HEADS // hps
    specs = [pl.BlockSpec((None, blk, wide), lambda b, h, i: (b, i, h)),
             pl.BlockSpec((None, seq, wide), lambda b, h, i: (b, 0, h + third)),
             pl.BlockSpec((None, seq, wide), lambda b, h, i: (b, 0, h + 2 * third)),
             pl.BlockSpec((None, hps, nblk, 1, blk), lambda b, h, i: (b, h, 0, 0, 0))]
    common = dict(
        grid=(batch, FOX_HEADS // hps, nblk),
        out_specs=pl.BlockSpec((None, blk, wide), lambda b, h, i: (b, i, h)),
        out_shape=jax.ShapeDtypeStruct((batch, seq, FOX_HEADS * FOX_HEAD_DIM), BF16),
        compiler_params=_params("parallel", "parallel", "arbitrary"))

    def shifted(qkv, f_rows, c):
        return pl.pallas_call(
            functools.partial(_attn_shift_kernel, blk=blk),
            in_specs=[pl.BlockSpec(memory_space=pltpu.SMEM)] + specs,
            scratch_shapes=[pltpu.VMEM((hps, blk, 2 * FOX_HEAD_DIM), F32),
                            pltpu.VMEM((hps, blk, FOX_HEAD_DIM), F32),
                            pltpu.VMEM((hps, blk, blk), BF16)],
            name="fox_attention_shift", **common)(c.reshape(1), qkv, qkv, qkv, f_rows)

    def online(qkv, f_rows, c):
        return pl.pallas_call(functools.partial(_attn_kernel, blk=blk), in_specs=specs,
                              name="fox_attention_online", **common)(qkv, qkv, qkv, f_rows)

    return lax.cond(qk_bound <= ATTN_SHIFT_LIMIT, shifted, online, qkv, f_rows, qk_bound)


S5_SLAB_GROUPS = 16
S5_SLAB = S5_SLAB_GROUPS * S5_GROUP
S5_SLAB_STATES = S5_SLAB_GROUPS * S5_STATE
S5_SLABS = S5_GROUPS // S5_SLAB_GROUPS


def _split_bf16(a):
    hi = a.astype(BF16)
    return hi, (a - hi.astype(F32)).astype(BF16)


def _dot_split(a, b_hi, b_lo):
    a_hi, a_lo = _split_bf16(a)
    return _dot(a_hi, b_hi) + _dot(a_hi, b_lo) + _dot(a_lo, b_hi)


def _lam_pow(k, lr, li, dt):
    mag = jnp.exp(k * (lr * dt))
    ang = k * (li * dt)
    return mag * jnp.cos(ang), mag * jnp.sin(ang)


def _s5_matrices_kernel(lam_ref, ls_ref, bdb_ref, bdc_ref, bp_ref, cp_ref, gr_ref, l16_ref,
                        bb_ref, chi_ref, clo_ref):
    t = pl.program_id(1)
    ns = S5_SLAB_STATES
    lr, li = lam_ref[0:1, :], lam_ref[1:2, :]
    dt = jnp.exp(ls_ref[...])

    @pl.when(t == 0)
    def _():
        lbr, lbi = _lam_pow(1.0, lr, li, dt)
        den = lr * lr + li * li
        nr = lbr - 1.0
        fr = (nr * lr + lbi * li) / den
        fi = (lbi * lr - nr * li) / den
        b_re, b_im = bdb_ref[0], bdb_ref[1]
        bb_ref[0] = fr * b_re - fi * b_im
        bb_ref[1] = fr * b_im + fi * b_re
        for part in range(2):
            chi_ref[part], clo_ref[part] = _split_bf16(bdc_ref[part])

    bb_re, bb_im = bb_ref[0], bb_ref[1]
    pr, pi = _lam_pow((S5_CHUNK - 1 - t).astype(F32), lr, li, dt)
    bp_re = bb_re * pr - bb_im * pi
    bp_im = bb_re * pi + bb_im * pr
    bp_ref[:, :ns] = bp_re.astype(BF16)
    bp_ref[:, ns:] = bp_im.astype(BF16)

    qr, qi = _lam_pow((t + 1).astype(F32), lr, li, dt)
    qr = jnp.broadcast_to(qr, (8, ns)).T[:, 0:1]
    qi = jnp.broadcast_to(qi, (8, ns)).T[:, 0:1]
    c_re, c_im = bdc_ref[0], bdc_ref[1]
    cp_ref[:ns, :] = (c_re * qr - c_im * qi).astype(BF16)
    cp_ref[ns:, :] = (-(c_re * qi + c_im * qr)).astype(BF16)

    gr_ref[...] = (_dot_split(bp_re, chi_ref[0], clo_ref[0])
                   - _dot_split(bp_im, chi_ref[1], clo_ref[1])).astype(BF16)

    l16r, l16i = _lam_pow(float(S5_CHUNK), lr, li, dt)
    l16_ref[0:1, :] = l16r
    l16_ref[1:2, :] = l16i


def _s5_matrices(lam, ls, bdb, bdc):
    t_len, w, ns = S5_CHUNK, S5_SLAB, S5_SLAB_STATES
    return pl.pallas_call(
        _s5_matrices_kernel,
        grid=(S5_SLABS, t_len),
        in_specs=[pl.BlockSpec((None, 2, ns), lambda s, t: (s, 0, 0)),
                  pl.BlockSpec((None, 1, ns), lambda s, t: (s, 0, 0)),
                  pl.BlockSpec((None, 2, w, ns), lambda s, t: (s, 0, 0, 0)),
                  pl.BlockSpec((None, 2, ns, w), lambda s, t: (s, 0, 0, 0))],
        out_specs=[pl.BlockSpec((None, w, 2 * ns), lambda s, t: (s, t, 0)),
                   pl.BlockSpec((None, None, 2 * ns, w), lambda s, t: (s, t, 0, 0)),
                   pl.BlockSpec((None, w, w), lambda s, t: (s, t, 0)),
                   pl.BlockSpec((None, 2, ns), lambda s, t: (s, 0, 0))],
        out_shape=[jax.ShapeDtypeStruct((S5_SLABS, t_len * w, 2 * ns), BF16),
                   jax.ShapeDtypeStruct((S5_SLABS, t_len, 2 * ns, w), BF16),
                   jax.ShapeDtypeStruct((S5_SLABS, t_len * w, w), BF16),
                   jax.ShapeDtypeStruct((S5_SLABS, 2, ns), F32)],
        scratch_shapes=[pltpu.VMEM((2, w, ns), F32), pltpu.VMEM((2, ns, w), BF16), pltpu.VMEM((2, ns, w), BF16)],
        compiler_params=_params("parallel", "arbitrary"),
        name="s5_matrices",
    )(lam, ls, bdb, bdc)


def _chunk_rows(x_ref):
    return jnp.concatenate([x_ref[:, t, :] for t in range(S5_CHUNK)], axis=1).astype(BF16)


def _s5_states_kernel(x_ref, bp_ref, s_ref):
    s_ref[...] = _dot(_chunk_rows(x_ref), bp_ref[...])


def _s5_states(x3, bp, *, rows):
    nc = x3.shape[0]
    t_len, w, ns = S5_CHUNK, S5_SLAB, S5_SLAB_STATES
    return pl.pallas_call(
        _s5_states_kernel,
        grid=(S5_SLABS, nc // rows),
        in_specs=[pl.BlockSpec((rows, t_len, w), lambda s, r: (r, 0, s)),
                  pl.BlockSpec((None, t_len * w, 2 * ns), lambda s, r: (s, 0, 0),
                               pipeline_mode=pl.Buffered(1))],
        out_specs=pl.BlockSpec((rows, 2 * ns), lambda s, r: (r, s)),
        out_shape=jax.ShapeDtypeStruct((nc, S5_SLABS * 2 * ns), F32),
        compiler_params=_params("parallel", "arbitrary"),
        name="s5_chunk_states",
    )(x3, bp)


def _s5_chunk_scan_kernel(s_ref, l16_ref, xp_ref, *, batch, chunks):
    ns = S5_SLAB_STATES
    ar, ai = l16_ref[0:1, :], l16_ref[1:2, :]

    def step(c, carry):
        out = []
        for b, (xr, xi) in enumerate(carry):
            row = pl.ds(b * chunks + c, 1)
            xp_ref[row, :ns] = xr
            xp_ref[row, ns:] = xi
            sr = s_ref[row, :ns]
            si = s_ref[row, ns:]
            out.append((ar * xr - ai * xi + sr, ar * xi + ai * xr + si))
        return tuple(out)

    zero = jnp.zeros((1, ns), F32)
    lax.fori_loop(0, chunks, step, ((zero, zero),) * batch)


def _s5_chunk_scan(s, l16, *, batch):
    nc = s.shape[0]
    ns = S5_SLAB_STATES
    return pl.pallas_call(
        functools.partial(_s5_chunk_scan_kernel, batch=batch, chunks=nc // batch),
        grid=(S5_SLABS,),
        in_specs=[pl.BlockSpec((nc, 2 * ns), lambda s_: (0, s_)),
                  pl.BlockSpec((None, 2, ns), lambda s_: (s_, 0, 0))],
        out_specs=pl.BlockSpec((nc, 2 * ns), lambda s_: (0, s_)),
        out_shape=jax.ShapeDtypeStruct(s.shape, F32),
        compiler_params=_params("parallel"),
        name="s5_chunk_scan",
    )(s, l16)


def _s5_output_kernel(x_ref, xp_ref, gr_ref, cp_ref, d_ref, y_ref):
    t_len, w = S5_CHUNK, S5_SLAB
    xc = _chunk_rows(x_ref)
    xp = xp_ref[...].astype(BF16)
    d = d_ref[...]
    for t in range(t_len):
        y = _dot(xc[:, :(t + 1) * w], gr_ref[(t_len - 1 - t) * w:, :])
        y = y + _dot(xp, cp_ref[t])
        y_ref[:, t, :] = y + d * x_ref[:, t, :]


def _s5_output(x3, xp, gr, cp, d, *, rows):
    nc = x3.shape[0]
    t_len, w, ns = S5_CHUNK, S5_SLAB, S5_SLAB_STATES
    once = dict(pipeline_mode=pl.Buffered(1))
    return pl.pallas_call(
        _s5_output_kernel,
        grid=(S5_SLABS, nc // rows),
        in_specs=[pl.BlockSpec((rows, t_len, w), lambda s, r: (r, 0, s)),
                  pl.BlockSpec((rows, 2 * ns), lambda s, r: (r, s)),
                  pl.BlockSpec((None, t_len * w, w), lambda s, r: (s, 0, 0), **once),
                  pl.BlockSpec((None, t_len, 2 * ns, w), lambda s, r: (s, 0, 0, 0), **once),
                  pl.BlockSpec((None, 1, w), lambda s, r: (s, 0, 0))],
        out_specs=pl.BlockSpec((rows, t_len, w), lambda s, r: (r, 0, s)),
        out_shape=jax.ShapeDtypeStruct(x3.shape, F32),
        compiler_params=_params("parallel", "arbitrary"),
        name="s5_chunk_output",
    )(x3, xp, gr, cp, d)


def _s5(s5_in, lam_re, lam_im, log_step, b_re, b_im, c_re, c_im, d, *, batch):
    m, width = s5_in.shape
    sl, sg, p, grp = S5_SLABS, S5_SLAB_GROUPS, S5_STATE, S5_GROUP
    ns = S5_SLAB_STATES
    lam = jnp.stack([lam_re, lam_im], axis=0).astype(F32).reshape(2, sl, ns).transpose(1, 0, 2)
    ls = jnp.repeat(log_step.astype(F32), p).reshape(sl, 1, ns)
    eye = jnp.eye(sg, dtype=F32)
    b = jnp.stack([b_re, b_im], axis=0).astype(F32).reshape(2, sl, sg, p, grp)
    bdb = jnp.einsum('rsgpi,gh->srgihp', b, eye).reshape(sl, 2, sg * grp, ns)
    c = jnp.stack([c_re, c_im], axis=0).astype(F32).reshape(2, sl, sg, grp, p)
    bdc = jnp.einsum('rsgop,gh->srhpgo', c, eye).reshape(sl, 2, ns, sg * grp)
    bp, cp, gr, l16 = _s5_matrices(lam, ls, bdb, bdc)
    x3 = s5_in.reshape(m // S5_CHUNK, S5_CHUNK, width)
    states = _s5_states(x3, bp, rows=512)
    xprev = _s5_chunk_scan(states, l16, batch=batch)
    y3 = _s5_output(x3, xprev, gr, cp, d.astype(F32).reshape(sl, 1, sg * grp), rows=256)
    return y3.reshape(m, width)


def _glu_kernel(y_ref, w_ref, b_ref, o_ref):
    g = jax.nn.gelu(y_ref[...])
    o_ref[...] = (g * jax.nn.sigmoid(_dot(g.astype(BF16), w_ref[...]) + b_ref[...])).astype(o_ref.dtype)


def _glu(y, w, b, *, bm=512):
    m, n = y.shape
    return pl.pallas_call(
        _glu_kernel,
        grid=(m // bm,),
        in_specs=[pl.BlockSpec((bm, n), lambda i: (i, 0)),
                  pl.BlockSpec((n, n), lambda i: (0, 0)),
                  pl.BlockSpec((1, n), lambda i: (0, 0))],
        out_specs=pl.BlockSpec((bm, n), lambda i: (i, 0)),
        out_shape=jax.ShapeDtypeStruct((m, n), BF16),
        compiler_params=_params("parallel"),
        name="s5_glu",
    )(y, w, b)


def _merge_kernel(u_ref, a_ref, s_ref, wgf_ref, wgs_ref, wpf_ref, wps_ref, bgf_ref, bgs_ref, wc1_ref, wc2_ref,
                  o_ref, wc1_out_ref, wc2_out_ref):
    _run_cast_jobs((wc1_ref, wc2_ref), (wc1_out_ref, wc2_out_ref))
    u = u_ref[...]
    gate_fox = jax.nn.sigmoid(_dot(u, wgf_ref[...]) + bgf_ref[...])
    gate_s5 = jax.nn.sigmoid(_dot(u, wgs_ref[...]) + bgs_ref[...])
    o_ref[...] = (gate_fox * _dot(a_ref[...], wpf_ref[...])
                  + gate_s5 * _dot(s_ref[...], wps_ref[...])).astype(o_ref.dtype)


def _merge(u, attn, ssm, wgf, wgs, wpf, wps, bgf, bgs, cast_1, cast_2, *, bm=512, bn=512):
    m, d = u.shape
    n = wpf.shape[1]
    nn = n // bn
    act = lambda a: pl.BlockSpec((bm, a.shape[1]), lambda i, j: (i, 0))
    wgt = lambda w: pl.BlockSpec((w.shape[0], bn), lambda i, j: (0, j))
    step = lambda i, j: i * nn + j
    (spec_1, shape_1), (spec_2, shape_2) = (_cast_job(w, (m // bm) * nn, step) for w in (cast_1, cast_2))
    return pl.pallas_call(
        _merge_kernel,
        grid=(m // bm, nn),
        in_specs=[act(u), act(attn), act(ssm), wgt(wgf), wgt(wgs), wgt(wpf), wgt(wps),
                  pl.BlockSpec((1, bn), lambda i, j: (0, j)),
                  pl.BlockSpec((1, bn), lambda i, j: (0, j)),
                  spec_1, spec_2],
        out_specs=[pl.BlockSpec((bm, bn), lambda i, j: (i, j)), spec_1, spec_2],
        out_shape=[jax.ShapeDtypeStruct((m, n), BF16), shape_1, shape_2],
        compiler_params=_params("parallel", "arbitrary"),
        name="gated_merge",
    )(u, attn, ssm, wgf, wgs, wpf, wps, bgf, bgs, cast_1, cast_2)


NORM_LANES = 128


def _out_proj_kernel(a_ref, w_ref, x_ref, g_ref, h_ref, hg_ref, r_ref, ssq_ref, *, d_model):
    j = pl.program_id(1)

    @pl.when(j == 0)
    def _():
        ssq_ref[...] = jnp.zeros_like(ssq_ref)

    h = x_ref[...] + _dot(a_ref[...], w_ref[...])
    h_ref[...] = h
    hg_ref[...] = (h * g_ref[...]).astype(hg_ref.dtype)
    ssq_ref[...] += jnp.sum(h * h, axis=-1, keepdims=True)

    @pl.when(j == pl.num_programs(1) - 1)
    def _():
        r_ref[...] = jnp.broadcast_to(lax.rsqrt(ssq_ref[...] * (1.0 / d_model) + RMS_EPS), r_ref.shape)


def _out_proj(a, w, x, g, *, bm, bn):
    m, k = a.shape
    n = w.shape[1]
    return pl.pallas_call(
        functools.partial(_out_proj_kernel, d_model=n),
        grid=(m // bm, n // bn),
        in_specs=[pl.BlockSpec((bm, k), lambda i, j: (i, 0)),
                  pl.BlockSpec((k, bn), lambda i, j: (0, j)),
                  pl.BlockSpec((bm, bn), lambda i, j: (i, j)),
                  pl.BlockSpec((1, bn), lambda i, j: (0, j))],
        out_specs=[pl.BlockSpec((bm, bn), lambda i, j: (i, j)),
                   pl.BlockSpec((bm, bn), lambda i, j: (i, j)),
                   pl.BlockSpec((bm, NORM_LANES), lambda i, j: (i, 0))],
        out_shape=[jax.ShapeDtypeStruct((m, n), F32),
                   jax.ShapeDtypeStruct((m, n), BF16),
                   jax.ShapeDtypeStruct((m, NORM_LANES), F32)],
        scratch_shapes=[pltpu.VMEM((bm, 1), F32)],
        compiler_params=_params("parallel", "arbitrary"),
        name="out_proj_residual",
    )(a, w, x, g)


def _swiglu_kernel(a_ref, r_ref, wg_ref, wu_ref, wc_ref, o_ref, wc_out_ref):
    _run_cast_jobs((wc_ref,), (wc_out_ref,))
    a = a_ref[...]
    r = jnp.concatenate([r_ref[...]] * (o_ref.shape[1] // NORM_LANES), axis=1)
    gate = _dot(a, wg_ref[...]) * r
    up = _dot(a, wu_ref[...]) * r
    o_ref[...] = (jax.nn.silu(gate) * up).astype(o_ref.dtype)


def _swiglu_up(a, r, w_gate_up, d_ff, cast_w, *, bm, bn):
    m, k = a.shape
    ni, nj = m // bm, d_ff // bn
    cast_spec, cast_shape = _cast_job(cast_w, ni * nj, lambda i, j: i * nj + j)
    return pl.pallas_call(
        _swiglu_kernel,
        grid=(ni, nj),
        in_specs=[pl.BlockSpec((bm, k), lambda i, j: (i, 0)),
                  pl.BlockSpec((bm, NORM_LANES), lambda i, j: (i, 0)),
                  pl.BlockSpec((k, bn), lambda i, j: (0, j)),
                  pl.BlockSpec((k, bn), lambda i, j: (0, j + nj)),
                  cast_spec],
        out_specs=[pl.BlockSpec((bm, bn), lambda i, j: (i, j)), cast_spec],
        out_shape=[jax.ShapeDtypeStruct((m, d_ff), BF16), cast_shape],
        compiler_params=_params("parallel", "arbitrary"),
        name="swiglu_up",
    )(a, r, w_gate_up, w_gate_up, cast_w)


def _down_kernel(a_ref, w_ref, r_ref, o_ref):
    o_ref[...] = r_ref[...] + _dot(a_ref[...], w_ref[...])


def _down_residual(a, w, res, *, bm, bn):
    m, k = a.shape
    n = w.shape[1]
    return pl.pallas_call(
        _down_kernel,
        grid=(m // bm, n // bn),
        in_specs=[pl.BlockSpec((bm, k), lambda i, j: (i, 0)),
                  pl.BlockSpec((k, bn), lambda i, j: (0, j)),
                  pl.BlockSpec((bm, bn), lambda i, j: (i, j))],
        out_specs=pl.BlockSpec((bm, bn), lambda i, j: (i, j)),
        out_shape=jax.ShapeDtypeStruct((m, n), F32),
        compiler_params=_params("parallel", "arbitrary"),
        name="down_proj_residual",
    )(a, w, res)


def _layer(x, batch, seq, g_mix, w_in, b_fgate, b_gates, q_norm, k_norm,
           lam_re, lam_im, log_step, b_re, b_im, c_re, c_im, s5_d,
           w_glu, b_glu, w_proj_fox, w_proj_s5, w_out, g_ffn, w_gate_up, w_down):
    m, d_model = x.shape
    fox_w = FOX_HEADS * FOX_HEAD_DIM
    s5_w = S5_GROUP * S5_GROUPS
    col_k, col_v = fox_w, 2 * fox_w
    col_f = 3 * fox_w
    col_s5 = col_f + FOX_HEADS
    col_g = col_s5 + s5_w
    d_ff = w_down.shape[0]
    w_in_b = w_in.astype(BF16)
    w_f, w_s5 = w_in_b[:, col_f:col_s5], w_in_b[:, col_s5:col_g]
    w_gate_fox, w_gate_s5 = w_in_b[:, col_g:col_g + d_model], w_in_b[:, col_g + d_model:col_g + 2 * d_model]

    u = _rmsnorm(x, g_mix)

    head_scale = jnp.concatenate([
        jnp.tile(q_norm.astype(F32), FOX_HEADS) * (LOG2_E / math.sqrt(FOX_HEAD_DIM)),
        jnp.tile(k_norm.astype(F32), FOX_HEADS),
        jnp.ones((fox_w,), F32)]).reshape(1, 3 * fox_w)
    qkv = _qkv_proj(u, w_in_b, head_scale, bm=1024, bn=1024)
    s5_in, f_t, w_proj_fox_b, w_proj_s5_b = _s5_fgate_proj(
        u, w_s5, w_f.T, b_fgate.astype(F32).reshape(FOX_HEADS, 1), batch, seq, w_proj_fox, w_proj_s5)

    blk = 512
    f_rows = f_t.reshape(batch, FOX_HEADS, seq // blk, 1, blk)
    qk_bound = (1.02 * LOG2_E * math.sqrt(FOX_HEAD_DIM)
                * jnp.max(jnp.abs(q_norm.astype(F32))) * jnp.max(jnp.abs(k_norm.astype(F32))))
    attn = _attention(qkv.reshape(batch, seq, 3 * fox_w), f_rows, qk_bound, blk=blk).reshape(m, fox_w)

    y5 = _s5(s5_in, lam_re, lam_im, log_step, b_re, b_im, c_re, c_im, s5_d, batch=batch)
    ssm = _glu(y5, w_glu.astype(BF16), b_glu.astype(F32).reshape(1, s5_w))

    b_gates = b_gates.astype(F32).reshape(1, 2 * d_model)
    merged, w_out_b, w_gate_up_b = _merge(u, attn, ssm, w_gate_fox, w_gate_s5, w_proj_fox_b, w_proj_s5_b,
                                          b_gates[:, :d_model], b_gates[:, d_model:], w_out, w_gate_up)
    h, hg, h_inv_rms = _out_proj(merged, w_out_b, x, g_ffn.astype(F32).reshape(1, d_model), bm=1024, bn=512)
    act, w_down_b = _swiglu_up(hg, h_inv_rms, w_gate_up_b, d_ff, w_down, bm=1024, bn=256)
    return _down_residual(act, w_down_b, h, bm=512, bn=512)


def kernel(x, g_mix, w_in, b_fgate, b_gates, q_norm, k_norm, s5_lambda_re, s5_lambda_im, s5_log_step, s5_b_re, s5_b_im, s5_c_re, s5_c_im, s5_d, w_glu, b_glu, w_proj_fox, w_proj_s5, w_out, g_ffn, w_gate_up, w_down):
    batch, seq, d_model = x.shape
    h = x.reshape(batch * seq, d_model)
    for l in range(g_mix.shape[0]):
        h = _layer(h, batch, seq, g_mix[l], w_in[l], b_fgate[l], b_gates[l], q_norm[l], k_norm[l],
                   s5_lambda_re[l], s5_lambda_im[l], s5_log_step[l], s5_b_re[l], s5_b_im[l],
                   s5_c_re[l], s5_c_im[l], s5_d[l], w_glu[l], b_glu[l],
                   w_proj_fox[l], w_proj_s5[l], w_out[l], g_ffn[l], w_gate_up[l], w_down[l])
    return h.reshape(batch, seq, d_model)
```

```python
import functools
import math

import jax
import jax.numpy as jnp
from jax import lax
from jax.experimental import pallas as pl
from jax.experimental.pallas import tpu as pltpu

F32 = jnp.float32
BF16 = jnp.bfloat16

FOX_HEADS = 16
FOX_HEAD_DIM = 128
S5_GROUP = 16
S5_GROUPS = 64
S5_STATE = 64
S5_CHUNK = 16
RMS_EPS = 1e-6
MASK_VALUE = -1e30
LOG2_E = math.log2(math.e)

VMEM_LIMIT_BYTES = 56 * 1024 * 1024

TILES = {
    "qkv_proj": dict(bm=1024, bn=1024),
    "gated_merge": dict(bm=512, bn=512),
    "out_proj": dict(bm=1024, bn=512),
    "swiglu_up": dict(bm=1024, bn=256),
    "down_proj": dict(bm=512, bn=512),
}
RMSNORM_ROWS = 256
ATTN_BLOCK = 512
S5_IN_ROWS = 1024
S5_STATE_ROWS = 512
S5_OUT_ROWS = 256


def _params(*semantics):
    return pltpu.CompilerParams(dimension_semantics=semantics,
                                vmem_limit_bytes=VMEM_LIMIT_BYTES)


def _dot(a, b):
    return jnp.dot(a, b, preferred_element_type=F32)


def _cast_job(w, steps, step_index):
    rows = w.shape[0] // steps
    assert rows * steps == w.shape[0] and rows % 16 == 0, (w.shape, steps)
    return (pl.BlockSpec((rows, w.shape[1]), lambda *g: (step_index(*g), 0)),
            jax.ShapeDtypeStruct(w.shape, BF16))


def _run_cast_jobs(src_refs, dst_refs):
    for src, dst in zip(src_refs, dst_refs):
        dst[...] = src[...].astype(dst.dtype)


def _rmsnorm_kernel(x_ref, g_ref, o_ref):
    x = x_ref[...]
    ms = jnp.mean(x * x, axis=-1, keepdims=True)
    o_ref[...] = (x * lax.rsqrt(ms + RMS_EPS) * g_ref[...]).astype(o_ref.dtype)


def _rmsnorm(x, g, *, bm=RMSNORM_ROWS):
    m, d = x.shape
    return pl.pallas_call(
        _rmsnorm_kernel,
        grid=(m // bm,),
        in_specs=[pl.BlockSpec((bm, d), lambda i: (i, 0)),
                  pl.BlockSpec((1, d), lambda i: (0, 0))],
        out_specs=pl.BlockSpec((bm, d), lambda i: (i, 0)),
        out_shape=jax.ShapeDtypeStruct((m, d), BF16),
        compiler_params=_params("parallel"),
        name="rmsnorm",
    )(x, g.reshape(1, d).astype(F32))


def _qkv_kernel(a_ref, w_ref, s_ref, o_ref, *, qk_tiles):
    acc = _dot(a_ref[...], w_ref[...])
    is_qk = pl.program_id(1) < qk_tiles
    for h in range(acc.shape[1] // FOX_HEAD_DIM):
        cols = slice(h * FOX_HEAD_DIM, (h + 1) * FOX_HEAD_DIM)
        blk = acc[:, cols]
        ms = jnp.mean(blk * blk, axis=-1, keepdims=True)
        inv = jnp.where(is_qk, lax.rsqrt(ms + RMS_EPS), 1.0)
        o_ref[:, cols] = (blk * inv * s_ref[:, cols]).astype(o_ref.dtype)


def _qkv_proj(a, w_in, scale, *, bm, bn):
    m, k = a.shape
    n = scale.shape[1]
    qk_tiles = 2 * FOX_HEADS * FOX_HEAD_DIM // bn
    return pl.pallas_call(
        functools.partial(_qkv_kernel, qk_tiles=qk_tiles),
        grid=(m // bm, n // bn),
        in_specs=[pl.BlockSpec((bm, k), lambda i, j: (i, 0)),
                  pl.BlockSpec((k, bn), lambda i, j: (0, j)),
                  pl.BlockSpec((1, bn), lambda i, j: (0, j))],
        out_specs=pl.BlockSpec((bm, bn), lambda i, j: (i, j)),
        out_shape=jax.ShapeDtypeStruct((m, n), BF16),
        compiler_params=_params("parallel", "arbitrary"),
        name="qkv_proj",
    )(a, w_in, scale)


def _s5_fgate_kernel(u_ref, ws_ref, wt_ref, b_ref, wa_ref, wb_ref, s_ref, f_ref, wa_out_ref, wb_out_ref,
                     carry_ref):
    @pl.when(pl.program_id(1) == 0)
    def _():
        carry_ref[...] = jnp.zeros_like(carry_ref)

    _run_cast_jobs((wa_ref, wb_ref), (wa_out_ref, wb_out_ref))

    u = u_ref[...]
    s_ref[...] = _dot(u, ws_ref[...])
    z = lax.dot_general(wt_ref[...], u, (((1,), (1,)), ((), ())),
                        preferred_element_type=F32) + b_ref[...]
    x = jnp.minimum(z, 0.0) - jnp.log1p(jnp.exp(-jnp.abs(z)))
    bs = x.shape[1]
    lane = lax.broadcasted_iota(jnp.int32, x.shape, 1)
    shift = 1
    while shift < bs:
        x = x + jnp.where(lane >= shift, pltpu.roll(x, shift, 1), 0.0)
        shift *= 2
    x = x + carry_ref[:, 0:1]
    blk = f_ref.shape[-1]
    for c in range(bs // blk):
        f_ref[:, c, 0, :] = x[:, c * blk:(c + 1) * blk] * LOG2_E
    carry_ref[...] = jnp.broadcast_to(x[:, bs - 1:bs], carry_ref.shape)


def _s5_fgate_proj(u, w_s5, wt, bias, batch, seq, blk, cast_a, cast_b, *, bs):
    h, d = wt.shape
    n = w_s5.shape[1]
    ns = seq // bs
    step = lambda b, j: b * ns + j
    (spec_a, shape_a), (spec_b, shape_b) = (_cast_job(w, batch * ns, step) for w in (cast_a, cast_b))
    return pl.pallas_call(
        _s5_fgate_kernel,
        grid=(batch, ns),
        in_specs=[pl.BlockSpec((bs, d), lambda b, j: (b * ns + j, 0)),
                  pl.BlockSpec((d, n), lambda b, j: (0, 0)),
                  pl.BlockSpec((h, d), lambda b, j: (0, 0)),
                  pl.BlockSpec((h, 1), lambda b, j: (0, 0)),
                  spec_a, spec_b],
        out_specs=[pl.BlockSpec((bs, n), lambda b, j: (b * ns + j, 0)),
                   pl.BlockSpec((None, h, bs // blk, 1, blk), lambda b, j: (b, 0, j, 0, 0)),
                   spec_a, spec_b],
        out_shape=[jax.ShapeDtypeStruct((batch * seq, n), F32),
                   jax.ShapeDtypeStruct((batch, h, seq // blk, 1, blk), F32),
                   shape_a, shape_b],
        scratch_shapes=[pltpu.VMEM((h, 128), F32)],
        compiler_params=_params("parallel", "arbitrary"),
        name="s5_in_fgate_proj",
    )(u, w_s5, wt, bias, cast_a, cast_b)


ATTN_HEADS_PER_STEP = 4


def _attn_kernel(q_ref, k_ref, v_ref, f_ref, o_ref, *, blk):
    i = pl.program_id(2)
    dh = FOX_HEAD_DIM
    heads = range(ATTN_HEADS_PER_STEP)
    qs = [q_ref[:, h * dh:(h + 1) * dh] for h in heads]

    def block(j, carry, masked):
        start = pl.multiple_of(j * blk, blk)
        out = []
        for h in heads:
            m, l, acc = carry[h]
            ks = k_ref[pl.ds(start, blk), h * dh:(h + 1) * dh]
            vs = v_ref[pl.ds(start, blk), h * dh:(h + 1) * dh]
            s = lax.dot_general(qs[h], ks, (((1,), (1,)), ((), ())), preferred_element_type=F32)
            s = s - f_ref[h, j]
            if masked:
                row = lax.broadcasted_iota(jnp.int32, s.shape, 0)
                col = lax.broadcasted_iota(jnp.int32, s.shape, 1)
                s = jnp.where(col <= row, s, MASK_VALUE)
            m_new = jnp.maximum(m, jnp.max(s, axis=-1, keepdims=True))
            alpha = jnp.exp2(m - m_new)
            p = jnp.exp2(s - m_new)
            l = alpha * l + jnp.sum(p, axis=-1, keepdims=True)
            acc = alpha * acc + _dot(p.astype(BF16), vs)
            out.append((m_new, l, acc))
        return tuple(out)

    init = tuple((jnp.full((blk, 1), MASK_VALUE, F32), jnp.zeros((blk, 1), F32),
                  jnp.zeros((blk, dh), F32)) for _ in heads)
    carry = lax.fori_loop(0, i, lambda j, c: block(j, c, False), init)
    carry = block(i, carry, True)
    for h in heads:
        _, l, acc = carry[h]
        o_ref[:, h * dh:(h + 1) * dh] = (acc / l).astype(o_ref.dtype)


def _attn_shift_kernel(c_ref, q_ref, k_ref, v_ref, f_ref, o_ref, acc_ref, shift_ref, p_ref, *, blk):
    i = pl.program_id(2)
    dh = FOX_HEAD_DIM
    heads = range(ATTN_HEADS_PER_STEP)
    qs = [q_ref[:, h * dh:(h + 1) * dh] for h in heads]
    ones = jnp.ones((blk, dh), BF16)

    def setup():
        for h in heads:
            col = jnp.broadcast_to(f_ref[h, i], (8, blk)).T[:, 0:1] - c_ref[0]
            shift_ref[h] = jnp.broadcast_to(col, (blk, dh))
        acc_ref[...] = jnp.zeros_like(acc_ref)

    def probabilities(h, j, masked):
        start = pl.multiple_of(j * blk, blk)
        ks = k_ref[pl.ds(start, blk), h * dh:(h + 1) * dh]
        s = lax.dot_general(qs[h], ks, (((1,), (1,)), ((), ())), preferred_element_type=F32)
        s = (s + jnp.concatenate([shift_ref[h]] * (blk // dh), axis=1)) - f_ref[h, j]
        if masked:
            row = lax.broadcasted_iota(jnp.int32, s.shape, 0)
            col = lax.broadcasted_iota(jnp.int32, s.shape, 1)
            s = jnp.where(col <= row, s, MASK_VALUE)
        return jnp.exp2(s).astype(BF16)

    def values(h, j):
        start = pl.multiple_of(j * blk, blk)
        return jnp.concatenate([v_ref[pl.ds(start, blk), h * dh:(h + 1) * dh], ones], axis=1)

    setup()

    @pl.when(i == 0)
    def _():
        p_ref[...] = jnp.zeros_like(p_ref)

    @pl.when(i > 0)
    def _():
        for h in heads:
            p_ref[h] = probabilities(h, 0, False)

    @pl.loop(1, i)
    def _(j):
        for h in heads:
            acc_ref[h] += _dot(p_ref[h], values(h, j - 1))
        for h in heads:
            p_ref[h] = probabilities(h, j, False)

    prev = jnp.maximum(i - 1, 0)
    for h in heads:
        acc_ref[h] += _dot(p_ref[h], values(h, prev)) + _dot(probabilities(h, i, True), values(h, i))
    for h in heads:
        acc = acc_ref[h]
        o_ref[:, h * dh:(h + 1) * dh] = (acc[:, :dh] / acc[:, dh:]).astype(o_ref.dtype)


ATTN_SHIFT_LIMIT = 56.0


def _attention(qkv, f_rows, qk_bound, *, blk):
    batch, seq, _ = qkv.shape
    nblk = seq // blk
    hps = ATTN_HEADS_PER_STEP
    wide = hps * FOX_HEAD_DIM
    third = FOX_HEADS // hps
    specs = [pl.BlockSpec((None, blk, wide), lambda b, h, i: (b, i, h)),
             pl.BlockSpec((None, seq, wide), lambda b, h, i: (b, 0, h + third)),
             pl.BlockSpec((None, seq, wide), lambda b, h, i: (b, 0, h + 2 * third)),
             pl.BlockSpec((None, hps, nblk, 1, blk), lambda b, h, i: (b, h, 0, 0, 0))]
    common = dict(
        grid=(batch, FOX_HEADS // hps, nblk),
        out_specs=pl.BlockSpec((None, blk, wide), lambda b, h, i: (b, i, h)),
        out_shape=jax.ShapeDtypeStruct((batch, seq, FOX_HEADS * FOX_HEAD_DIM), BF16),
        compiler_params=_params("parallel", "parallel", "arbitrary"))

    def shifted(qkv, f_rows, c):
        return pl.pallas_call(
            functools.partial(_attn_shift_kernel, blk=blk),
            in_specs=[pl.BlockSpec(memory_space=pltpu.SMEM)] + specs,
            scratch_shapes=[pltpu.VMEM((hps, blk, 2 * FOX_HEAD_DIM), F32),
                            pltpu.VMEM((hps, blk, FOX_HEAD_DIM), F32),
                            pltpu.VMEM((hps, blk, blk), BF16)],
            name="fox_attention_shift", **common)(c.reshape(1), qkv, qkv, qkv, f_rows)

    def online(qkv, f_rows, c):
        return pl.pallas_call(functools.partial(_attn_kernel, blk=blk), in_specs=specs,
                              name="fox_attention_online", **common)(qkv, qkv, qkv, f_rows)

    return lax.cond(qk_bound <= ATTN_SHIFT_LIMIT, shifted, online, qkv, f_rows, qk_bound)


S5_SLAB_GROUPS = 16
S5_SLAB = S5_SLAB_GROUPS * S5_GROUP
S5_SLAB_STATES = S5_SLAB_GROUPS * S5_STATE
S5_SLABS = S5_GROUPS // S5_SLAB_GROUPS


def _split_bf16(a):
    hi = a.astype(BF16)
    return hi, (a - hi.astype(F32)).astype(BF16)


def _dot_split(a, b_hi, b_lo):
    a_hi, a_lo = _split_bf16(a)
    return _dot(a_hi, b_hi) + _dot(a_hi, b_lo) + _dot(a_lo, b_hi)


def _lam_pow(k, lr, li, dt):
    mag = jnp.exp(k * (lr * dt))
    ang = k * (li * dt)
    return mag * jnp.cos(ang), mag * jnp.sin(ang)


def _s5_matrices_kernel(lam_ref, ls_ref, bdb_ref, bdc_ref, bp_ref, cp_ref, gr_ref, l16_ref,
                        bb_ref, chi_ref, clo_ref):
    t = pl.program_id(1)
    ns = S5_SLAB_STATES
    lr, li = lam_ref[0:1, :], lam_ref[1:2, :]
    dt = jnp.exp(ls_ref[...])

    @pl.when(t == 0)
    def _():
        lbr, lbi = _lam_pow(1.0, lr, li, dt)
        den = lr * lr + li * li
        nr = lbr - 1.0
        fr = (nr * lr + lbi * li) / den
        fi = (lbi * lr - nr * li) / den
        b_re, b_im = bdb_ref[0], bdb_ref[1]
        bb_ref[0] = fr * b_re - fi * b_im
        bb_ref[1] = fr * b_im + fi * b_re
        for part in range(2):
            chi_ref[part], clo_ref[part] = _split_bf16(bdc_ref[part])

    bb_re, bb_im = bb_ref[0], bb_ref[1]
    pr, pi = _lam_pow((S5_CHUNK - 1 - t).astype(F32), lr, li, dt)
    bp_re = bb_re * pr - bb_im * pi
    bp_im = bb_re * pi + bb_im * pr
    bp_ref[:, :ns] = bp_re.astype(BF16)
    bp_ref[:, ns:] = bp_im.astype(BF16)

    qr, qi = _lam_pow((t + 1).astype(F32), lr, li, dt)
    qr = jnp.broadcast_to(qr, (8, ns)).T[:, 0:1]
    qi = jnp.broadcast_to(qi, (8, ns)).T[:, 0:1]
    c_re, c_im = bdc_ref[0], bdc_ref[1]
    cp_ref[:ns, :] = (c_re * qr - c_im * qi).astype(BF16)
    cp_ref[ns:, :] = (-(c_re * qi + c_im * qr)).astype(BF16)

    gr_ref[...] = (_dot_split(bp_re, chi_ref[0], clo_ref[0])
                   - _dot_split(bp_im, chi_ref[1], clo_ref[1])).astype(BF16)

    l16r, l16i = _lam_pow(float(S5_CHUNK), lr, li, dt)
    l16_ref[0:1, :] = l16r
    l16_ref[1:2, :] = l16i


def _s5_matrices(lam, ls, bdb, bdc):
    t_len, w, ns = S5_CHUNK, S5_SLAB, S5_SLAB_STATES
    return pl.pallas_call(
        _s5_matrices_kernel,
        grid=(S5_SLABS, t_len),
        in_specs=[pl.BlockSpec((None, 2, ns), lambda s, t: (s, 0, 0)),
                  pl.BlockSpec((None, 1, ns), lambda s, t: (s, 0, 0)),
                  pl.BlockSpec((None, 2, w, ns), lambda s, t: (s, 0, 0, 0)),
                  pl.BlockSpec((None, 2, ns, w), lambda s, t: (s, 0, 0, 0))],
        out_specs=[pl.BlockSpec((None, w, 2 * ns), lambda s, t: (s, t, 0)),
                   pl.BlockSpec((None, None, 2 * ns, w), lambda s, t: (s, t, 0, 0)),
                   pl.BlockSpec((None, w, w), lambda s, t: (s, t, 0)),
                   pl.BlockSpec((None, 2, ns), lambda s, t: (s, 0, 0))],
        out_shape=[jax.ShapeDtypeStruct((S5_SLABS, t_len * w, 2 * ns), BF16),
                   jax.ShapeDtypeStruct((S5_SLABS, t_len, 2 * ns, w), BF16),
                   jax.ShapeDtypeStruct((S5_SLABS, t_len * w, w), BF16),
                   jax.ShapeDtypeStruct((S5_SLABS, 2, ns), F32)],
        scratch_shapes=[pltpu.VMEM((2, w, ns), F32), pltpu.VMEM((2, ns, w), BF16), pltpu.VMEM((2, ns, w), BF16)],
        compiler_params=_params("parallel", "arbitrary"),
        name="s5_matrices",
    )(lam, ls, bdb, bdc)


def _chunk_rows(x_ref):
    return jnp.concatenate([x_ref[:, t, :] for t in range(S5_CHUNK)], axis=1).astype(BF16)


def _s5_states_kernel(x_ref, bp_ref, s_ref):
    s_ref[...] = _dot(_chunk_rows(x_ref), bp_ref[...])


def _s5_states(x3, bp, *, rows):
    nc = x3.shape[0]
    t_len, w, ns = S5_CHUNK, S5_SLAB, S5_SLAB_STATES
    return pl.pallas_call(
        _s5_states_kernel,
        grid=(S5_SLABS, nc // rows),
        in_specs=[pl.BlockSpec((rows, t_len, w), lambda s, r: (r, 0, s)),
                  pl.BlockSpec((None, t_len * w, 2 * ns), lambda s, r: (s, 0, 0),
                               pipeline_mode=pl.Buffered(1))],
        out_specs=pl.BlockSpec((rows, 2 * ns), lambda s, r: (r, s)),
        out_shape=jax.ShapeDtypeStruct((nc, S5_SLABS * 2 * ns), F32),
        compiler_params=_params("parallel", "arbitrary"),
        name="s5_chunk_states",
    )(x3, bp)


def _s5_chunk_scan_kernel(s_ref, l16_ref, xp_ref, *, batch, chunks):
    ns = S5_SLAB_STATES
    ar, ai = l16_ref[0:1, :], l16_ref[1:2, :]

    def step(c, carry):
        out = []
        for b, (xr, xi) in enumerate(carry):
            row = pl.ds(b * chunks + c, 1)
            xp_ref[row, :ns] = xr
            xp_ref[row, ns:] = xi
            sr = s_ref[row, :ns]
            si = s_ref[row, ns:]
            out.append((ar * xr - ai * xi + sr, ar * xi + ai * xr + si))
        return tuple(out)

    zero = jnp.zeros((1, ns), F32)
    lax.fori_loop(0, chunks, step, ((zero, zero),) * batch)


def _s5_chunk_scan(s, l16, *, batch):
    nc = s.shape[0]
    ns = S5_SLAB_STATES
    return pl.pallas_call(
        functools.partial(_s5_chunk_scan_kernel, batch=batch, chunks=nc // batch),
        grid=(S5_SLABS,),
        in_specs=[pl.BlockSpec((nc, 2 * ns), lambda s_: (0, s_)),
                  pl.BlockSpec((None, 2, ns), lambda s_: (s_, 0, 0))],
        out_specs=pl.BlockSpec((nc, 2 * ns), lambda s_: (0, s_)),
        out_shape=jax.ShapeDtypeStruct(s.shape, F32),
        compiler_params=_params("parallel"),
        name="s5_chunk_scan",
    )(s, l16)


def _s5_output_kernel(x_ref, xp_ref, gr_ref, cp_ref, d_ref, y_ref):
    t_len, w = S5_CHUNK, S5_SLAB
    xc = _chunk_rows(x_ref)
    xp = xp_ref[...].astype(BF16)
    d = d_ref[...]
    for t in range(t_len):
        y = _dot(xc[:, :(t + 1) * w], gr_ref[(t_len - 1 - t) * w:, :])
        y = y + _dot(xp, cp_ref[t])
        y_ref[:, t, :] = y + d * x_ref[:, t, :]


def _s5_output(x3, xp, gr, cp, d, *, rows):
    nc = x3.shape[0]
    t_len, w, ns = S5_CHUNK, S5_SLAB, S5_SLAB_STATES
    once = dict(pipeline_mode=pl.Buffered(1))
    return pl.pallas_call(
        _s5_output_kernel,
        grid=(S5_SLABS, nc // rows),
        in_specs=[pl.BlockSpec((rows, t_len, w), lambda s, r: (r, 0, s)),
                  pl.BlockSpec((rows, 2 * ns), lambda s, r: (r, s)),
                  pl.BlockSpec((None, t_len * w, w), lambda s, r: (s, 0, 0), **once),
                  pl.BlockSpec((None, t_len, 2 * ns, w), lambda s, r: (s, 0, 0, 0), **once),
                  pl.BlockSpec((None, 1, w), lambda s, r: (s, 0, 0))],
        out_specs=pl.BlockSpec((rows, t_len, w), lambda s, r: (r, 0, s)),
        out_shape=jax.ShapeDtypeStruct(x3.shape, F32),
        compiler_params=_params("parallel", "arbitrary"),
        name="s5_chunk_output",
    )(x3, xp, gr, cp, d)


def _s5(s5_in, lam_re, lam_im, log_step, b_re, b_im, c_re, c_im, d, *, batch):
    m, width = s5_in.shape
    sl, sg, p, grp = S5_SLABS, S5_SLAB_GROUPS, S5_STATE, S5_GROUP
    ns = S5_SLAB_STATES
    lam = jnp.stack([lam_re, lam_im], axis=0).astype(F32).reshape(2, sl, ns).transpose(1, 0, 2)
    ls = jnp.repeat(log_step.astype(F32), p).reshape(sl, 1, ns)
    eye = jnp.eye(sg, dtype=F32)
    b = jnp.stack([b_re, b_im], axis=0).astype(F32).reshape(2, sl, sg, p, grp)
    bdb = jnp.einsum('rsgpi,gh->srgihp', b, eye).reshape(sl, 2, sg * grp, ns)
    c = jnp.stack([c_re, c_im], axis=0).astype(F32).reshape(2, sl, sg, grp, p)
    bdc = jnp.einsum('rsgop,gh->srhpgo', c, eye).reshape(sl, 2, ns, sg * grp)
    bp, cp, gr, l16 = _s5_matrices(lam, ls, bdb, bdc)
    x3 = s5_in.reshape(m // S5_CHUNK, S5_CHUNK, width)
    states = _s5_states(x3, bp, rows=S5_STATE_ROWS)
    xprev = _s5_chunk_scan(states, l16, batch=batch)
    y3 = _s5_output(x3, xprev, gr, cp, d.astype(F32).reshape(sl, 1, sg * grp), rows=S5_OUT_ROWS)
    return y3.reshape(m, width)


def _glu_kernel(y_ref, w_ref, b_ref, o_ref):
    g = jax.nn.gelu(y_ref[...])
    o_ref[...] = (g * jax.nn.sigmoid(_dot(g.astype(BF16), w_ref[...]) + b_ref[...])).astype(o_ref.dtype)


def _glu(y, w, b, *, bm=512):
    m, n = y.shape
    return pl.pallas_call(
        _glu_kernel,
        grid=(m // bm,),
        in_specs=[pl.BlockSpec((bm, n), lambda i: (i, 0)),
                  pl.BlockSpec((n, n), lambda i: (0, 0)),
                  pl.BlockSpec((1, n), lambda i: (0, 0))],
        out_specs=pl.BlockSpec((bm, n), lambda i: (i, 0)),
        out_shape=jax.ShapeDtypeStruct((m, n), BF16),
        compiler_params=_params("parallel"),
        name="s5_glu",
    )(y, w, b)


def _merge_kernel(u_ref, a_ref, s_ref, wgf_ref, wgs_ref, wpf_ref, wps_ref, bgf_ref, bgs_ref, wc1_ref, wc2_ref,
                  o_ref, wc1_out_ref, wc2_out_ref):
    _run_cast_jobs((wc1_ref, wc2_ref), (wc1_out_ref, wc2_out_ref))
    u = u_ref[...]
    gate_fox = jax.nn.sigmoid(_dot(u, wgf_ref[...]) + bgf_ref[...])
    gate_s5 = jax.nn.sigmoid(_dot(u, wgs_ref[...]) + bgs_ref[...])
    o_ref[...] = (gate_fox * _dot(a_ref[...], wpf_ref[...])
                  + gate_s5 * _dot(s_ref[...], wps_ref[...])).astype(o_ref.dtype)


def _merge(u, attn, ssm, wgf, wgs, wpf, wps, bgf, bgs, cast_1, cast_2, *, bm, bn):
    m, d = u.shape
    n = wpf.shape[1]
    nn = n // bn
    act = lambda a: pl.BlockSpec((bm, a.shape[1]), lambda i, j: (i, 0))
    wgt = lambda w: pl.BlockSpec((w.shape[0], bn), lambda i, j: (0, j))
    step = lambda i, j: i * nn + j
    (spec_1, shape_1), (spec_2, shape_2) = (_cast_job(w, (m // bm) * nn, step) for w in (cast_1, cast_2))
    return pl.pallas_call(
        _merge_kernel,
        grid=(m // bm, nn),
        in_specs=[act(u), act(attn), act(ssm), wgt(wgf), wgt(wgs), wgt(wpf), wgt(wps),
                  pl.BlockSpec((1, bn), lambda i, j: (0, j)),
                  pl.BlockSpec((1, bn), lambda i, j: (0, j)),
                  spec_1, spec_2],
        out_specs=[pl.BlockSpec((bm, bn), lambda i, j: (i, j)), spec_1, spec_2],
        out_shape=[jax.ShapeDtypeStruct((m, n), BF16), shape_1, shape_2],
        compiler_params=_params("parallel", "arbitrary"),
        name="gated_merge",
    )(u, attn, ssm, wgf, wgs, wpf, wps, bgf, bgs, cast_1, cast_2)


NORM_LANES = 128


def _out_proj_kernel(a_ref, w_ref, x_ref, g_ref, h_ref, hg_ref, r_ref, ssq_ref, *, d_model):
    j = pl.program_id(1)

    @pl.when(j == 0)
    def _():
        ssq_ref[...] = jnp.zeros_like(ssq_ref)

    h = x_ref[...] + _dot(a_ref[...], w_ref[...])
    h_ref[...] = h
    hg_ref[...] = (h * g_ref[...]).astype(hg_ref.dtype)
    ssq_ref[...] += jnp.sum(h * h, axis=-1, keepdims=True)

    @pl.when(j == pl.num_programs(1) - 1)
    def _():
        r_ref[...] = jnp.broadcast_to(lax.rsqrt(ssq_ref[...] * (1.0 / d_model) + RMS_EPS), r_ref.shape)


def _out_proj(a, w, x, g, *, bm, bn):
    m, k = a.shape
    n = w.shape[1]
    return pl.pallas_call(
        functools.partial(_out_proj_kernel, d_model=n),
        grid=(m // bm, n // bn),
        in_specs=[pl.BlockSpec((bm, k), lambda i, j: (i, 0)),
                  pl.BlockSpec((k, bn), lambda i, j: (0, j)),
                  pl.BlockSpec((bm, bn), lambda i, j: (i, j)),
                  pl.BlockSpec((1, bn), lambda i, j: (0, j))],
        out_specs=[pl.BlockSpec((bm, bn), lambda i, j: (i, j)),
                   pl.BlockSpec((bm, bn), lambda i, j: (i, j)),
                   pl.BlockSpec((bm, NORM_LANES), lambda i, j: (i, 0))],
        out_shape=[jax.ShapeDtypeStruct((m, n), F32),
                   jax.ShapeDtypeStruct((m, n), BF16),
                   jax.ShapeDtypeStruct((m, NORM_LANES), F32)],
        scratch_shapes=[pltpu.VMEM((bm, 1), F32)],
        compiler_params=_params("parallel", "arbitrary"),
        name="out_proj_residual",
    )(a, w, x, g)


def _swiglu_kernel(a_ref, r_ref, wg_ref, wu_ref, wc_ref, o_ref, wc_out_ref):
    _run_cast_jobs((wc_ref,), (wc_out_ref,))
    a = a_ref[...]
    r = jnp.concatenate([r_ref[...]] * (o_ref.shape[1] // NORM_LANES), axis=1)
    gate = _dot(a, wg_ref[...]) * r
    up = _dot(a, wu_ref[...]) * r
    o_ref[...] = (jax.nn.silu(gate) * up).astype(o_ref.dtype)


def _swiglu_up(a, r, w_gate_up, d_ff, cast_w, *, bm, bn):
    m, k = a.shape
    ni, nj = m // bm, d_ff // bn
    cast_spec, cast_shape = _cast_job(cast_w, ni * nj, lambda i, j: i * nj + j)
    return pl.pallas_call(
        _swiglu_kernel,
        grid=(ni, nj),
        in_specs=[pl.BlockSpec((bm, k), lambda i, j: (i, 0)),
                  pl.BlockSpec((bm, NORM_LANES), lambda i, j: (i, 0)),
                  pl.BlockSpec((k, bn), lambda i, j: (0, j)),
                  pl.BlockSpec((k, bn), lambda i, j: (0, j + nj)),
                  cast_spec],
        out_specs=[pl.BlockSpec((bm, bn), lambda i, j: (i, j)), cast_spec],
        out_shape=[jax.ShapeDtypeStruct((m, d_ff), BF16), cast_shape],
        compiler_params=_params("parallel", "arbitrary"),
        name="swiglu_up",
    )(a, r, w_gate_up, w_gate_up, cast_w)


def _down_kernel(a_ref, w_ref, r_ref, o_ref):
    o_ref[...] = r_ref[...] + _dot(a_ref[...], w_ref[...])


def _down_residual(a, w, res, *, bm, bn):
    m, k = a.shape
    n = w.shape[1]
    return pl.pallas_call(
        _down_kernel,
        grid=(m // bm, n // bn),
        in_specs=[pl.BlockSpec((bm, k), lambda i, j: (i, 0)),
                  pl.BlockSpec((k, bn), lambda i, j: (0, j)),
                  pl.BlockSpec((bm, bn), lambda i, j: (i, j))],
        out_specs=pl.BlockSpec((bm, bn), lambda i, j: (i, j)),
        out_shape=jax.ShapeDtypeStruct((m, n), F32),
        compiler_params=_params("parallel", "arbitrary"),
        name="down_proj_residual",
    )(a, w, res)


def _layer(x, batch, seq, g_mix, w_in, b_fgate, b_gates, q_norm, k_norm,
           lam_re, lam_im, log_step, b_re, b_im, c_re, c_im, s5_d,
           w_glu, b_glu, w_proj_fox, w_proj_s5, w_out, g_ffn, w_gate_up, w_down):
    m, d_model = x.shape
    fox_w = FOX_HEADS * FOX_HEAD_DIM
    s5_w = S5_GROUP * S5_GROUPS
    col_k, col_v = fox_w, 2 * fox_w
    col_f = 3 * fox_w
    col_s5 = col_f + FOX_HEADS
    col_g = col_s5 + s5_w
    d_ff = w_down.shape[0]
    w_in_b = w_in.astype(BF16)
    w_f, w_s5 = w_in_b[:, col_f:col_s5], w_in_b[:, col_s5:col_g]
    w_gate_fox, w_gate_s5 = w_in_b[:, col_g:col_g + d_model], w_in_b[:, col_g + d_model:col_g + 2 * d_model]

    u = _rmsnorm(x, g_mix)

    head_scale = jnp.concatenate([
        jnp.tile(q_norm.astype(F32), FOX_HEADS) * (LOG2_E / math.sqrt(FOX_HEAD_DIM)),
        jnp.tile(k_norm.astype(F32), FOX_HEADS),
        jnp.ones((fox_w,), F32)]).reshape(1, 3 * fox_w)
    qkv = _qkv_proj(u, w_in_b, head_scale, **TILES["qkv_proj"])
    s5_in, f_rows, w_proj_fox_b, w_proj_s5_b = _s5_fgate_proj(
        u, w_s5, w_f.T, b_fgate.astype(F32).reshape(FOX_HEADS, 1), batch, seq, ATTN_BLOCK, w_proj_fox, w_proj_s5,
        bs=S5_IN_ROWS)

    qk_bound = (1.02 * LOG2_E * math.sqrt(FOX_HEAD_DIM)
                * jnp.max(jnp.abs(q_norm.astype(F32))) * jnp.max(jnp.abs(k_norm.astype(F32))))
    attn = _attention(qkv.reshape(batch, seq, 3 * fox_w), f_rows, qk_bound, blk=ATTN_BLOCK).reshape(m, fox_w)

    y5 = _s5(s5_in, lam_re, lam_im, log_step, b_re, b_im, c_re, c_im, s5_d, batch=batch)
    ssm = _glu(y5, w_glu.astype(BF16), b_glu.astype(F32).reshape(1, s5_w))

    b_gates = b_gates.astype(F32).reshape(1, 2 * d_model)
    merged, w_out_b, w_gate_up_b = _merge(u, attn, ssm, w_gate_fox, w_gate_s5, w_proj_fox_b, w_proj_s5_b,
                                          b_gates[:, :d_model], b_gates[:, d_model:], w_out, w_gate_up,
                                          **TILES["gated_merge"])
    h, hg, h_inv_rms = _out_proj(merged, w_out_b, x, g_ffn.astype(F32).reshape(1, d_model), **TILES["out_proj"])
    act, w_down_b = _swiglu_up(hg, h_inv_rms, w_gate_up_b, d_ff, w_down, **TILES["swiglu_up"])
    return _down_residual(act, w_down_b, h, **TILES["down_proj"])


def kernel(x, g_mix, w_in, b_fgate, b_gates, q_norm, k_norm, s5_lambda_re, s5_lambda_im, s5_log_step, s5_b_re, s5_b_im, s5_c_re, s5_c_im, s5_d, w_glu, b_glu, w_proj_fox, w_proj_s5, w_out, g_ffn, w_gate_up, w_down):
    batch, seq, d_model = x.shape
    h = x.reshape(batch * seq, d_model)
    for l in range(g_mix.shape[0]):
        h = _layer(h, batch, seq, g_mix[l], w_in[l], b_fgate[l], b_gates[l], q_norm[l], k_norm[l],
                   s5_lambda_re[l], s5_lambda_im[l], s5_log_step[l], s5_b_re[l], s5_b_im[l],
                   s5_c_re[l], s5_c_im[l], s5_d[l], w_glu[l], b_glu[l],
                   w_proj_fox[l], w_proj_s5[l], w_out[l], g_ffn[l], w_gate_up[l], w_down[l])
    return h.reshape(batch, seq, d_model)
```

```python
import functools
import math

import jax
import jax.numpy as jnp
from jax import lax
from jax.experimental import pallas as pl
from jax.experimental.pallas import tpu as pltpu

F32 = jnp.float32
BF16 = jnp.bfloat16

FOX_HEADS = 16
FOX_HEAD_DIM = 128
S5_GROUP = 16
S5_GROUPS = 64
S5_STATE = 64
S5_CHUNK = 16
RMS_EPS = 1e-6
MASK_VALUE = -1e30
LOG2_E = math.log2(math.e)

VMEM_LIMIT_BYTES = 56 * 1024 * 1024

TILES = {
    "qkv_proj": dict(bm=1024, bn=1024),
    "gated_merge": dict(bm=1024, bn=256),
    "out_proj": dict(bm=1024, bn=512),
    "swiglu_up": dict(bm=2048, bn=256),
    "down_proj": dict(bm=512, bn=512),
}
RMSNORM_ROWS = 256
ATTN_BLOCK = 512
S5_IN_ROWS = 1024
S5_STATE_ROWS = 512
S5_OUT_ROWS = 256


def _params(*semantics):
    return pltpu.CompilerParams(dimension_semantics=semantics,
                                vmem_limit_bytes=VMEM_LIMIT_BYTES)


def _dot(a, b):
    return jnp.dot(a, b, preferred_element_type=F32)


def _cast_job(w, steps, step_index):
    rows = w.shape[0] // steps
    assert rows * steps == w.shape[0] and rows % 16 == 0, (w.shape, steps)
    return (pl.BlockSpec((rows, w.shape[1]), lambda *g: (step_index(*g), 0)),
            jax.ShapeDtypeStruct(w.shape, BF16))


def _run_cast_jobs(src_refs, dst_refs):
    for src, dst in zip(src_refs, dst_refs):
        dst[...] = src[...].astype(dst.dtype)


def _rmsnorm_kernel(x_ref, g_ref, o_ref):
    x = x_ref[...]
    ms = jnp.mean(x * x, axis=-1, keepdims=True)
    o_ref[...] = (x * lax.rsqrt(ms + RMS_EPS) * g_ref[...]).astype(o_ref.dtype)


def _rmsnorm(x, g, *, bm=RMSNORM_ROWS):
    m, d = x.shape
    return pl.pallas_call(
        _rmsnorm_kernel,
        grid=(m // bm,),
        in_specs=[pl.BlockSpec((bm, d), lambda i: (i, 0)),
                  pl.BlockSpec((1, d), lambda i: (0, 0))],
        out_specs=pl.BlockSpec((bm, d), lambda i: (i, 0)),
        out_shape=jax.ShapeDtypeStruct((m, d), BF16),
        compiler_params=_params("parallel"),
        name="rmsnorm",
    )(x, g.reshape(1, d).astype(F32))


def _qkv_kernel(a_ref, w_ref, s_ref, o_ref, *, qk_tiles):
    acc = _dot(a_ref[...], w_ref[...])
    is_qk = pl.program_id(1) < qk_tiles
    for h in range(acc.shape[1] // FOX_HEAD_DIM):
        cols = slice(h * FOX_HEAD_DIM, (h + 1) * FOX_HEAD_DIM)
        blk = acc[:, cols]
        ms = jnp.mean(blk * blk, axis=-1, keepdims=True)
        inv = jnp.where(is_qk, lax.rsqrt(ms + RMS_EPS), 1.0)
        o_ref[:, cols] = (blk * inv * s_ref[:, cols]).astype(o_ref.dtype)


def _qkv_proj(a, w_in, scale, *, bm, bn):
    m, k = a.shape
    n = scale.shape[1]
    qk_tiles = 2 * FOX_HEADS * FOX_HEAD_DIM // bn
    return pl.pallas_call(
        functools.partial(_qkv_kernel, qk_tiles=qk_tiles),
        grid=(m // bm, n // bn),
        in_specs=[pl.BlockSpec((bm, k), lambda i, j: (i, 0)),
                  pl.BlockSpec((k, bn), lambda i, j: (0, j)),
                  pl.BlockSpec((1, bn), lambda i, j: (0, j))],
        out_specs=pl.BlockSpec((bm, bn), lambda i, j: (i, j)),
        out_shape=jax.ShapeDtypeStruct((m, n), BF16),
        compiler_params=_params("parallel", "arbitrary"),
        name="qkv_proj",
    )(a, w_in, scale)


def _s5_fgate_kernel(u_ref, ws_ref, wt_ref, b_ref, wa_ref, wb_ref, s_ref, f_ref, wa_out_ref, wb_out_ref,
                     carry_ref):
    @pl.when(pl.program_id(1) == 0)
    def _():
        carry_ref[...] = jnp.zeros_like(carry_ref)

    _run_cast_jobs((wa_ref, wb_ref), (wa_out_ref, wb_out_ref))

    u = u_ref[...]
    s_ref[...] = _dot(u, ws_ref[...])
    z = lax.dot_general(wt_ref[...], u, (((1,), (1,)), ((), ())),
                        preferred_element_type=F32) + b_ref[...]
    x = jnp.minimum(z, 0.0) - jnp.log1p(jnp.exp(-jnp.abs(z)))
    bs = x.shape[1]
    lane = lax.broadcasted_iota(jnp.int32, x.shape, 1)
    shift = 1
    while shift < bs:
        x = x + jnp.where(lane >= shift, pltpu.roll(x, shift, 1), 0.0)
        shift *= 2
    x = x + carry_ref[:, 0:1]
    blk = f_ref.shape[-1]
    for c in range(bs // blk):
        f_ref[:, c, 0, :] = x[:, c * blk:(c + 1) * blk] * LOG2_E
    carry_ref[...] = jnp.broadcast_to(x[:, bs - 1:bs], carry_ref.shape)


def _s5_fgate_proj(u, w_s5, wt, bias, batch, seq, blk, cast_a, cast_b, *, bs):
    h, d = wt.shape
    n = w_s5.shape[1]
    ns = seq // bs
    step = lambda b, j: b * ns + j
    (spec_a, shape_a), (spec_b, shape_b) = (_cast_job(w, batch * ns, step) for w in (cast_a, cast_b))
    return pl.pallas_call(
        _s5_fgate_kernel,
        grid=(batch, ns),
        in_specs=[pl.BlockSpec((bs, d), lambda b, j: (b * ns + j, 0)),
                  pl.BlockSpec((d, n), lambda b, j: (0, 0)),
                  pl.BlockSpec((h, d), lambda b, j: (0, 0)),
                  pl.BlockSpec((h, 1), lambda b, j: (0, 0)),
                  spec_a, spec_b],
        out_specs=[pl.BlockSpec((bs, n), lambda b, j: (b * ns + j, 0)),
                   pl.BlockSpec((None, h, bs // blk, 1, blk), lambda b, j: (b, 0, j, 0, 0)),
                   spec_a, spec_b],
        out_shape=[jax.ShapeDtypeStruct((batch * seq, n), F32),
                   jax.ShapeDtypeStruct((batch, h, seq // blk, 1, blk), F32),
                   shape_a, shape_b],
        scratch_shapes=[pltpu.VMEM((h, 128), F32)],
        compiler_params=_params("parallel", "arbitrary"),
        name="s5_in_fgate_proj",
    )(u, w_s5, wt, bias, cast_a, cast_b)


ATTN_HEADS_PER_STEP = 4


def _attn_kernel(q_ref, k_ref, v_ref, f_ref, o_ref, *, blk):
    i = pl.program_id(2)
    dh = FOX_HEAD_DIM
    heads = range(ATTN_HEADS_PER_STEP)
    qs = [q_ref[:, h * dh:(h + 1) * dh] for h in heads]

    def block(j, carry, masked):
        start = pl.multiple_of(j * blk, blk)
        out = []
        for h in heads:
            m, l, acc = carry[h]
            ks = k_ref[pl.ds(start, blk), h * dh:(h + 1) * dh]
            vs = v_ref[pl.ds(start, blk), h * dh:(h + 1) * dh]
            s = lax.dot_general(qs[h], ks, (((1,), (1,)), ((), ())), preferred_element_type=F32)
            s = s - f_ref[h, j]
            if masked:
                row = lax.broadcasted_iota(jnp.int32, s.shape, 0)
                col = lax.broadcasted_iota(jnp.int32, s.shape, 1)
                s = jnp.where(col <= row, s, MASK_VALUE)
            m_new = jnp.maximum(m, jnp.max(s, axis=-1, keepdims=True))
            alpha = jnp.exp2(m - m_new)
            p = jnp.exp2(s - m_new)
            l = alpha * l + jnp.sum(p, axis=-1, keepdims=True)
            acc = alpha * acc + _dot(p.astype(BF16), vs)
            out.append((m_new, l, acc))
        return tuple(out)

    init = tuple((jnp.full((blk, 1), MASK_VALUE, F32), jnp.zeros((blk, 1), F32),
                  jnp.zeros((blk, dh), F32)) for _ in heads)
    carry = lax.fori_loop(0, i, lambda j, c: block(j, c, False), init)
    carry = block(i, carry, True)
    for h in heads:
        _, l, acc = carry[h]
        o_ref[:, h * dh:(h + 1) * dh] = (acc / l).astype(o_ref.dtype)


def _attn_shift_kernel(c_ref, q_ref, k_ref, v_ref, f_ref, o_ref, acc_ref, shift_ref, p_ref, *, blk):
    i = pl.program_id(2)
    dh = FOX_HEAD_DIM
    heads = range(ATTN_HEADS_PER_STEP)
    qs = [q_ref[:, h * dh:(h + 1) * dh] for h in heads]
    ones = jnp.ones((blk, dh), BF16)

    def setup():
        for h in heads:
            col = jnp.broadcast_to(f_ref[h, i], (8, blk)).T[:, 0:1] - c_ref[0]
            shift_ref[h] = jnp.broadcast_to(col, (blk, dh))
        acc_ref[...] = jnp.zeros_like(acc_ref)

    def probabilities(h, j, masked):
        start = pl.multiple_of(j * blk, blk)
        ks = k_ref[pl.ds(start, blk), h * dh:(h + 1) * dh]
        s = lax.dot_general(qs[h], ks, (((1,), (1,)), ((), ())), preferred_element_type=F32)
        s = (s + jnp.concatenate([shift_ref[h]] * (blk // dh), axis=1)) - f_ref[h, j]
        if masked:
            row = lax.broadcasted_iota(jnp.int32, s.shape, 0)
            col = lax.broadcasted_iota(jnp.int32, s.shape, 1)
            s = jnp.where(col <= row, s, MASK_VALUE)
        return jnp.exp2(s).astype(BF16)

    def values(h, j):
        start = pl.multiple_of(j * blk, blk)
        return jnp.concatenate([v_ref[pl.ds(start, blk), h * dh:(h + 1) * dh], ones], axis=1)

    setup()

    @pl.when(i == 0)
    def _():
        p_ref[...] = jnp.zeros_like(p_ref)

    @pl.when(i > 0)
    def _():
        for h in heads:
            p_ref[h] = probabilities(h, 0, False)

    @pl.loop(1, i)
    def _(j):
        for h in heads:
            acc_ref[h] += _dot(p_ref[h], values(h, j - 1))
        for h in heads:
            p_ref[h] = probabilities(h, j, False)

    prev = jnp.maximum(i - 1, 0)
    for h in heads:
        acc_ref[h] += _dot(p_ref[h], values(h, prev)) + _dot(probabilities(h, i, True), values(h, i))
    for h in heads:
        acc = acc_ref[h]
        o_ref[:, h * dh:(h + 1) * dh] = (acc[:, :dh] / acc[:, dh:]).astype(o_ref.dtype)


ATTN_SHIFT_LIMIT = 56.0


def _attention(qkv, f_rows, qk_bound, *, blk):
    batch, seq, _ = qkv.shape
    nblk = seq // blk
    hps = ATTN_HEADS_PER_STEP
    wide = hps * FOX_HEAD_DIM
    third = FOX_HEADS // hps
    specs = [pl.BlockSpec((None, blk, wide), lambda b, h, i: (b, i, h)),
             pl.BlockSpec((None, seq, wide), lambda b, h, i: (b, 0, h + third)),
             pl.BlockSpec((None, seq, wide), lambda b, h, i: (b, 0, h + 2 * third)),
             pl.BlockSpec((None, hps, nblk, 1, blk), lambda b, h, i: (b, h, 0, 0, 0))]
    common = dict(
        grid=(batch, FOX_HEADS // hps, nblk),
        out_specs=pl.BlockSpec((None, blk, wide), lambda b, h, i: (b, i, h)),
        out_shape=jax.ShapeDtypeStruct((batch, seq, FOX_HEADS * FOX_HEAD_DIM), BF16),
        compiler_params=_params("parallel", "parallel", "arbitrary"))

    def shifted(qkv, f_rows, c):
        return pl.pallas_call(
            functools.partial(_attn_shift_kernel, blk=blk),
            in_specs=[pl.BlockSpec(memory_space=pltpu.SMEM)] + specs,
            scratch_shapes=[pltpu.VMEM((hps, blk, 2 * FOX_HEAD_DIM), F32),
                            pltpu.VMEM((hps, blk, FOX_HEAD_DIM), F32),
                            pltpu.VMEM((hps, blk, blk), BF16)],
            name="fox_attention_shift", **common)(c.reshape(1), qkv, qkv, qkv, f_rows)

    def online(qkv, f_rows, c):
        return pl.pallas_call(functools.partial(_attn_kernel, blk=blk), in_specs=specs,
                              name="fox_attention_online", **common)(qkv, qkv, qkv, f_rows)

    return lax.cond(qk_bound <= ATTN_SHIFT_LIMIT, shifted, online, qkv, f_rows, qk_bound)


S5_SLAB_GROUPS = 16
S5_SLAB = S5_SLAB_GROUPS * S5_GROUP
S5_SLAB_STATES = S5_SLAB_GROUPS * S5_STATE
S5_SLABS = S5_GROUPS // S5_SLAB_GROUPS


def _split_bf16(a):
    hi = a.astype(BF16)
    return hi, (a - hi.astype(F32)).astype(BF16)


def _dot_split(a, b_hi, b_lo):
    a_hi, a_lo = _split_bf16(a)
    return _dot(a_hi, b_hi) + _dot(a_hi, b_lo) + _dot(a_lo, b_hi)


def _lam_pow(k, lr, li, dt):
    mag = jnp.exp(k * (lr * dt))
    ang = k * (li * dt)
    return mag * jnp.cos(ang), mag * jnp.sin(ang)


def _s5_matrices_kernel(lam_ref, ls_ref, bdb_ref, bdc_ref, bp_ref, cp_ref, gr_ref, l16_ref,
                        bb_ref, chi_ref, clo_ref):
    t = pl.program_id(1)
    ns = S5_SLAB_STATES
    lr, li = lam_ref[0:1, :], lam_ref[1:2, :]
    dt = jnp.exp(ls_ref[...])

    @pl.when(t == 0)
    def _():
        lbr, lbi = _lam_pow(1.0, lr, li, dt)
        den = lr * lr + li * li
        nr = lbr - 1.0
        fr = (nr * lr + lbi * li) / den
        fi = (lbi * lr - nr * li) / den
        b_re, b_im = bdb_ref[0], bdb_ref[1]
        bb_ref[0] = fr * b_re - fi * b_im
        bb_ref[1] = fr * b_im + fi * b_re
        for part in range(2):
            chi_ref[part], clo_ref[part] = _split_bf16(bdc_ref[part])

    bb_re, bb_im = bb_ref[0], bb_ref[1]
    pr, pi = _lam_pow((S5_CHUNK - 1 - t).astype(F32), lr, li, dt)
    bp_re = bb_re * pr - bb_im * pi
    bp_im = bb_re * pi + bb_im * pr
    bp_ref[:, :ns] = bp_re.astype(BF16)
    bp_ref[:, ns:] = bp_im.astype(BF16)

    qr, qi = _lam_pow((t + 1).astype(F32), lr, li, dt)
    qr = jnp.broadcast_to(qr, (8, ns)).T[:, 0:1]
    qi = jnp.broadcast_to(qi, (8, ns)).T[:, 0:1]
    c_re, c_im = bdc_ref[0], bdc_ref[1]
    cp_ref[:ns, :] = (c_re * qr - c_im * qi).astype(BF16)
    cp_ref[ns:, :] = (-(c_re * qi + c_im * qr)).astype(BF16)

    gr_ref[...] = (_dot_split(bp_re, chi_ref[0], clo_ref[0])
                   - _dot_split(bp_im, chi_ref[1], clo_ref[1])).astype(BF16)

    l16r, l16i = _lam_pow(float(S5_CHUNK), lr, li, dt)
    l16_ref[0:1, :] = l16r
    l16_ref[1:2, :] = l16i


def _s5_matrices(lam, ls, bdb, bdc):
    t_len, w, ns = S5_CHUNK, S5_SLAB, S5_SLAB_STATES
    return pl.pallas_call(
        _s5_matrices_kernel,
        grid=(S5_SLABS, t_len),
        in_specs=[pl.BlockSpec((None, 2, ns), lambda s, t: (s, 0, 0)),
                  pl.BlockSpec((None, 1, ns), lambda s, t: (s, 0, 0)),
                  pl.BlockSpec((None, 2, w, ns), lambda s, t: (s, 0, 0, 0)),
                  pl.BlockSpec((None, 2, ns, w), lambda s, t: (s, 0, 0, 0))],
        out_specs=[pl.BlockSpec((None, w, 2 * ns), lambda s, t: (s, t, 0)),
                   pl.BlockSpec((None, None, 2 * ns, w), lambda s, t: (s, t, 0, 0)),
                   pl.BlockSpec((None, w, w), lambda s, t: (s, t, 0)),
                   pl.BlockSpec((None, 2, ns), lambda s, t: (s, 0, 0))],
        out_shape=[jax.ShapeDtypeStruct((S5_SLABS, t_len * w, 2 * ns), BF16),
                   jax.ShapeDtypeStruct((S5_SLABS, t_len, 2 * ns, w), BF16),
                   jax.ShapeDtypeStruct((S5_SLABS, t_len * w, w), BF16),
                   jax.ShapeDtypeStruct((S5_SLABS, 2, ns), F32)],
        scratch_shapes=[pltpu.VMEM((2, w, ns), F32), pltpu.VMEM((2, ns, w), BF16), pltpu.VMEM((2, ns, w), BF16)],
        compiler_params=_params("parallel", "arbitrary"),
        name="s5_matrices",
    )(lam, ls, bdb, bdc)


def _chunk_rows(x_ref):
    return jnp.concatenate([x_ref[:, t, :] for t in range(S5_CHUNK)], axis=1).astype(BF16)


def _s5_states_kernel(x_ref, bp_ref, s_ref):
    s_ref[...] = _dot(_chunk_rows(x_ref), bp_ref[...])


def _s5_states(x3, bp, *, rows):
    nc = x3.shape[0]
    t_len, w, ns = S5_CHUNK, S5_SLAB, S5_SLAB_STATES
    return pl.pallas_call(
        _s5_states_kernel,
        grid=(S5_SLABS, nc // rows),
        in_specs=[pl.BlockSpec((rows, t_len, w), lambda s, r: (r, 0, s)),
                  pl.BlockSpec((None, t_len * w, 2 * ns), lambda s, r: (s, 0, 0),
                               pipeline_mode=pl.Buffered(1))],
        out_specs=pl.BlockSpec((rows, 2 * ns), lambda s, r: (r, s)),
        out_shape=jax.ShapeDtypeStruct((nc, S5_SLABS * 2 * ns), F32),
        compiler_params=_params("parallel", "arbitrary"),
        name="s5_chunk_states",
    )(x3, bp)


def _s5_chunk_scan_kernel(s_ref, l16_ref, xp_ref, *, batch, chunks):
    ns = S5_SLAB_STATES
    ar, ai = l16_ref[0:1, :], l16_ref[1:2, :]

    def step(c, carry):
        out = []
        for b, (xr, xi) in enumerate(carry):
            row = pl.ds(b * chunks + c, 1)
            xp_ref[row, :ns] = xr
            xp_ref[row, ns:] = xi
            sr = s_ref[row, :ns]
            si = s_ref[row, ns:]
            out.append((ar * xr - ai * xi + sr, ar * xi + ai * xr + si))
        return tuple(out)

    zero = jnp.zeros((1, ns), F32)
    lax.fori_loop(0, chunks, step, ((zero, zero),) * batch)


def _s5_chunk_scan(s, l16, *, batch):
    nc = s.shape[0]
    ns = S5_SLAB_STATES
    return pl.pallas_call(
        functools.partial(_s5_chunk_scan_kernel, batch=batch, chunks=nc // batch),
        grid=(S5_SLABS,),
        in_specs=[pl.BlockSpec((nc, 2 * ns), lambda s_: (0, s_)),
                  pl.BlockSpec((None, 2, ns), lambda s_: (s_, 0, 0))],
        out_specs=pl.BlockSpec((nc, 2 * ns), lambda s_: (0, s_)),
        out_shape=jax.ShapeDtypeStruct(s.shape, F32),
        compiler_params=_params("parallel"),
        name="s5_chunk_scan",
    )(s, l16)


def _s5_output_kernel(x_ref, xp_ref, gr_ref, cp_ref, d_ref, y_ref):
    t_len, w = S5_CHUNK, S5_SLAB
    xc = _chunk_rows(x_ref)
    xp = xp_ref[...].astype(BF16)
    d = d_ref[...]
    for t in range(t_len):
        y = _dot(xc[:, :(t + 1) * w], gr_ref[(t_len - 1 - t) * w:, :])
        y = y + _dot(xp, cp_ref[t])
        y_ref[:, t, :] = y + d * x_ref[:, t, :]


def _s5_output(x3, xp, gr, cp, d, *, rows):
    nc = x3.shape[0]
    t_len, w, ns = S5_CHUNK, S5_SLAB, S5_SLAB_STATES
    once = dict(pipeline_mode=pl.Buffered(1))
    return pl.pallas_call(
        _s5_output_kernel,
        grid=(S5_SLABS, nc // rows),
        in_specs=[pl.BlockSpec((rows, t_len, w), lambda s, r: (r, 0, s)),
                  pl.BlockSpec((rows, 2 * ns), lambda s, r: (r, s)),
                  pl.BlockSpec((None, t_len * w, w), lambda s, r: (s, 0, 0), **once),
                  pl.BlockSpec((None, t_len, 2 * ns, w), lambda s, r: (s, 0, 0, 0), **once),
                  pl.BlockSpec((None, 1, w), lambda s, r: (s, 0, 0))],
        out_specs=pl.BlockSpec((rows, t_len, w), lambda s, r: (r, 0, s)),
        out_shape=jax.ShapeDtypeStruct(x3.shape, F32),
        compiler_params=_params("parallel", "arbitrary"),
        name="s5_chunk_output",
    )(x3, xp, gr, cp, d)


def _s5(s5_in, lam_re, lam_im, log_step, b_re, b_im, c_re, c_im, d, *, batch):
    m, width = s5_in.shape
    sl, sg, p, grp = S5_SLABS, S5_SLAB_GROUPS, S5_STATE, S5_GROUP
    ns = S5_SLAB_STATES
    lam = jnp.stack([lam_re, lam_im], axis=0).astype(F32).reshape(2, sl, ns).transpose(1, 0, 2)
    ls = jnp.repeat(log_step.astype(F32), p).reshape(sl, 1, ns)
    eye = jnp.eye(sg, dtype=F32)
    b = jnp.stack([b_re, b_im], axis=0).astype(F32).reshape(2, sl, sg, p, grp)
    bdb = jnp.einsum('rsgpi,gh->srgihp', b, eye).reshape(sl, 2, sg * grp, ns)
    c = jnp.stack([c_re, c_im], axis=0).astype(F32).reshape(2, sl, sg, grp, p)
    bdc = jnp.einsum('rsgop,gh->srhpgo', c, eye).reshape(sl, 2, ns, sg * grp)
    bp, cp, gr, l16 = _s5_matrices(lam, ls, bdb, bdc)
    x3 = s5_in.reshape(m // S5_CHUNK, S5_CHUNK, width)
    states = _s5_states(x3, bp, rows=S5_STATE_ROWS)
    xprev = _s5_chunk_scan(states, l16, batch=batch)
    y3 = _s5_output(x3, xprev, gr, cp, d.astype(F32).reshape(sl, 1, sg * grp), rows=S5_OUT_ROWS)
    return y3.reshape(m, width)


def _glu_kernel(y_ref, w_ref, b_ref, o_ref):
    g = jax.nn.gelu(y_ref[...])
    o_ref[...] = (g * jax.nn.sigmoid(_dot(g.astype(BF16), w_ref[...]) + b_ref[...])).astype(o_ref.dtype)


def _glu(y, w, b, *, bm=512):
    m, n = y.shape
    return pl.pallas_call(
        _glu_kernel,
        grid=(m // bm,),
        in_specs=[pl.BlockSpec((bm, n), lambda i: (i, 0)),
                  pl.BlockSpec((n, n), lambda i: (0, 0)),
                  pl.BlockSpec((1, n), lambda i: (0, 0))],
        out_specs=pl.BlockSpec((bm, n), lambda i: (i, 0)),
        out_shape=jax.ShapeDtypeStruct((m, n), BF16),
        compiler_params=_params("parallel"),
        name="s5_glu",
    )(y, w, b)


def _merge_kernel(u_ref, a_ref, s_ref, wgf_ref, wgs_ref, wpf_ref, wps_ref, bgf_ref, bgs_ref, wc1_ref, wc2_ref,
                  o_ref, wc1_out_ref, wc2_out_ref):
    _run_cast_jobs((wc1_ref, wc2_ref), (wc1_out_ref, wc2_out_ref))
    u = u_ref[...]
    gate_fox = jax.nn.sigmoid(_dot(u, wgf_ref[...]) + bgf_ref[...])
    gate_s5 = jax.nn.sigmoid(_dot(u, wgs_ref[...]) + bgs_ref[...])
    o_ref[...] = (gate_fox * _dot(a_ref[...], wpf_ref[...])
                  + gate_s5 * _dot(s_ref[...], wps_ref[...])).astype(o_ref.dtype)


def _merge(u, attn, ssm, wgf, wgs, wpf, wps, bgf, bgs, cast_1, cast_2, *, bm, bn):
    m, d = u.shape
    n = wpf.shape[1]
    nn = n // bn
    act = lambda a: pl.BlockSpec((bm, a.shape[1]), lambda i, j: (i, 0))
    wgt = lambda w: pl.BlockSpec((w.shape[0], bn), lambda i, j: (0, j))
    step = lambda i, j: i * nn + j
    (spec_1, shape_1), (spec_2, shape_2) = (_cast_job(w, (m // bm) * nn, step) for w in (cast_1, cast_2))
    return pl.pallas_call(
        _merge_kernel,
        grid=(m // bm, nn),
        in_specs=[act(u), act(attn), act(ssm), wgt(wgf), wgt(wgs), wgt(wpf), wgt(wps),
                  pl.BlockSpec((1, bn), lambda i, j: (0, j)),
                  pl.BlockSpec((1, bn), lambda i, j: (0, j)),
                  spec_1, spec_2],
        out_specs=[pl.BlockSpec((bm, bn), lambda i, j: (i, j)), spec_1, spec_2],
        out_shape=[jax.ShapeDtypeStruct((m, n), BF16), shape_1, shape_2],
        compiler_params=_params("parallel", "arbitrary"),
        name="gated_merge",
    )(u, attn, ssm, wgf, wgs, wpf, wps, bgf, bgs, cast_1, cast_2)


NORM_LANES = 128


def _out_proj_kernel(a_ref, w_ref, x_ref, g_ref, h_ref, hg_ref, r_ref, ssq_ref, *, d_model):
    j = pl.program_id(1)

    @pl.when(j == 0)
    def _():
        ssq_ref[...] = jnp.zeros_like(ssq_ref)

    h = x_ref[...] + _dot(a_ref[...], w_ref[...])
    h_ref[...] = h
    hg_ref[...] = (h * g_ref[...]).astype(hg_ref.dtype)
    ssq_ref[...] += jnp.sum(h * h, axis=-1, keepdims=True)

    @pl.when(j == pl.num_programs(1) - 1)
    def _():
        r_ref[...] = jnp.broadcast_to(lax.rsqrt(ssq_ref[...] * (1.0 / d_model) + RMS_EPS), r_ref.shape)


def _out_proj(a, w, x, g, *, bm, bn):
    m, k = a.shape
    n = w.shape[1]
    return pl.pallas_call(
        functools.partial(_out_proj_kernel, d_model=n),
        grid=(m // bm, n // bn),
        in_specs=[pl.BlockSpec((bm, k), lambda i, j: (i, 0)),
                  pl.BlockSpec((k, bn), lambda i, j: (0, j)),
                  pl.BlockSpec((bm, bn), lambda i, j: (i, j)),
                  pl.BlockSpec((1, bn), lambda i, j: (0, j))],
        out_specs=[pl.BlockSpec((bm, bn), lambda i, j: (i, j)),
                   pl.BlockSpec((bm, bn), lambda i, j: (i, j)),
                   pl.BlockSpec((bm, NORM_LANES), lambda i, j: (i, 0))],
        out_shape=[jax.ShapeDtypeStruct((m, n), F32),
                   jax.ShapeDtypeStruct((m, n), BF16),
                   jax.ShapeDtypeStruct((m, NORM_LANES), F32)],
        scratch_shapes=[pltpu.VMEM((bm, 1), F32)],
        compiler_params=_params("parallel", "arbitrary"),
        name="out_proj_residual",
    )(a, w, x, g)


def _swiglu_kernel(a_ref, r_ref, wg_ref, wu_ref, wc_ref, o_ref, wc_out_ref):
    _run_cast_jobs((wc_ref,), (wc_out_ref,))
    a = a_ref[...]
    r = jnp.concatenate([r_ref[...]] * (o_ref.shape[1] // NORM_LANES), axis=1)
    gate = _dot(a, wg_ref[...]) * r
    up = _dot(a, wu_ref[...]) * r
    o_ref[...] = (jax.nn.silu(gate) * up).astype(o_ref.dtype)


def _swiglu_up(a, r, w_gate_up, d_ff, cast_w, *, bm, bn):
    m, k = a.shape
    ni, nj = m // bm, d_ff // bn
    cast_spec, cast_shape = _cast_job(cast_w, ni * nj, lambda i, j: i * nj + j)
    return pl.pallas_call(
        _swiglu_kernel,
        grid=(ni, nj),
        in_specs=[pl.BlockSpec((bm, k), lambda i, j: (i, 0)),
                  pl.BlockSpec((bm, NORM_LANES), lambda i, j: (i, 0)),
                  pl.BlockSpec((k, bn), lambda i, j: (0, j)),
                  pl.BlockSpec((k, bn), lambda i, j: (0, j + nj)),
                  cast_spec],
        out_specs=[pl.BlockSpec((bm, bn), lambda i, j: (i, j)), cast_spec],
        out_shape=[jax.ShapeDtypeStruct((m, d_ff), BF16), cast_shape],
        compiler_params=_params("parallel", "arbitrary"),
        name="swiglu_up",
    )(a, r, w_gate_up, w_gate_up, cast_w)


def _down_kernel(a_ref, w_ref, r_ref, o_ref):
    o_ref[...] = r_ref[...] + _dot(a_ref[...], w_ref[...])


def _down_residual(a, w, res, *, bm, bn):
    m, k = a.shape
    n = w.shape[1]
    return pl.pallas_call(
        _down_kernel,
        grid=(m // bm, n // bn),
        in_specs=[pl.BlockSpec((bm, k), lambda i, j: (i, 0)),
                  pl.BlockSpec((k, bn), lambda i, j: (0, j)),
                  pl.BlockSpec((bm, bn), lambda i, j: (i, j))],
        out_specs=pl.BlockSpec((bm, bn), lambda i, j: (i, j)),
        out_shape=jax.ShapeDtypeStruct((m, n), F32),
        compiler_params=_params("parallel", "arbitrary"),
        name="down_proj_residual",
    )(a, w, res)


def _layer(x, batch, seq, g_mix, w_in, b_fgate, b_gates, q_norm, k_norm,
           lam_re, lam_im, log_step, b_re, b_im, c_re, c_im, s5_d,
           w_glu, b_glu, w_proj_fox, w_proj_s5, w_out, g_ffn, w_gate_up, w_down):
    m, d_model = x.shape
    fox_w = FOX_HEADS * FOX_HEAD_DIM
    s5_w = S5_GROUP * S5_GROUPS
    col_k, col_v = fox_w, 2 * fox_w
    col_f = 3 * fox_w
    col_s5 = col_f + FOX_HEADS
    col_g = col_s5 + s5_w
    d_ff = w_down.shape[0]
    w_in_b = w_in.astype(BF16)
    w_f, w_s5 = w_in_b[:, col_f:col_s5], w_in_b[:, col_s5:col_g]
    w_gate_fox, w_gate_s5 = w_in_b[:, col_g:col_g + d_model], w_in_b[:, col_g + d_model:col_g + 2 * d_model]

    u = _rmsnorm(x, g_mix)

    head_scale = jnp.concatenate([
        jnp.tile(q_norm.astype(F32), FOX_HEADS) * (LOG2_E / math.sqrt(FOX_HEAD_DIM)),
        jnp.tile(k_norm.astype(F32), FOX_HEADS),
        jnp.ones((fox_w,), F32)]).reshape(1, 3 * fox_w)
    qkv = _qkv_proj(u, w_in_b, head_scale, **TILES["qkv_proj"])
    s5_in, f_rows, w_proj_fox_b, w_proj_s5_b = _s5_fgate_proj(
        u, w_s5, w_f.T, b_fgate.astype(F32).reshape(FOX_HEADS, 1), batch, seq, ATTN_BLOCK, w_proj_fox, w_proj_s5,
        bs=S5_IN_ROWS)

    qk_bound = (1.02 * LOG2_E * math.sqrt(FOX_HEAD_DIM)
                * jnp.max(jnp.abs(q_norm.astype(F32))) * jnp.max(jnp.abs(k_norm.astype(F32))))
    attn = _attention(qkv.reshape(batch, seq, 3 * fox_w), f_rows, qk_bound, blk=ATTN_BLOCK).reshape(m, fox_w)

    y5 = _s5(s5_in, lam_re, lam_im, log_step, b_re, b_im, c_re, c_im, s5_d, batch=batch)
    ssm = _glu(y5, w_glu.astype(BF16), b_glu.astype(F32).reshape(1, s5_w))

    b_gates = b_gates.astype(F32).reshape(1, 2 * d_model)
    merged, w_out_b, w_gate_up_b = _merge(u, attn, ssm, w_gate_fox, w_gate_s5, w_proj_fox_b, w_proj_s5_b,
                                          b_gates[:, :d_model], b_gates[:, d_model:], w_out, w_gate_up,
                                          **TILES["gated_merge"])
    h, hg, h_inv_rms = _out_proj(merged, w_out_b, x, g_ffn.astype(F32).reshape(1, d_model), **TILES["out_proj"])
    act, w_down_b = _swiglu_up(hg, h_inv_rms, w_gate_up_b, d_ff, w_down, **TILES["swiglu_up"])
    return _down_residual(act, w_down_b, h, **TILES["down_proj"])


def kernel(x, g_mix, w_in, b_fgate, b_gates, q_norm, k_norm, s5_lambda_re, s5_lambda_im, s5_log_step, s5_b_re, s5_b_im, s5_c_re, s5_c_im, s5_d, w_glu, b_glu, w_proj_fox, w_proj_s5, w_out, g_ffn, w_gate_up, w_down):
    batch, seq, d_model = x.shape
    h = x.reshape(batch * seq, d_model)
    for l in range(g_mix.shape[0]):
        h = _layer(h, batch, seq, g_mix[l], w_in[l], b_fgate[l], b_gates[l], q_norm[l], k_norm[l],
                   s5_lambda_re[l], s5_lambda_im[l], s5_log_step[l], s5_b_re[l], s5_b_im[l],
                   s5_c_re[l], s5_c_im[l], s5_d[l], w_glu[l], b_glu[l],
                   w_proj_fox[l], w_proj_s5[l], w_out[l], g_ffn[l], w_gate_up[l], w_down[l])
    return h.reshape(batch, seq, d_model)
```

```python
import functools
import math

import jax
import jax.numpy as jnp
from jax import lax
from jax.experimental import pallas as pl
from jax.experimental.pallas import tpu as pltpu

F32 = jnp.float32
BF16 = jnp.bfloat16

FOX_HEADS = 16
FOX_HEAD_DIM = 128
S5_GROUP = 16
S5_GROUPS = 64
S5_STATE = 64
S5_CHUNK = 16
RMS_EPS = 1e-6
MASK_VALUE = -1e30
LOG2_E = math.log2(math.e)

VMEM_LIMIT_BYTES = 56 * 1024 * 1024

TILES = {
    "qkv_proj": dict(bm=1024, bn=1024),
    "gated_merge": dict(bm=1024, bn=256),
    "out_proj": dict(bm=1024, bn=512),
    "swiglu_up": dict(bm=2048, bn=256),
    "down_proj": dict(bm=512, bn=512),
}
RMSNORM_ROWS = 256
ATTN_BLOCK = 512
S5_IN_ROWS = 1024
S5_STATE_ROWS = 512
S5_OUT_ROWS = 256


def _params(*semantics):
    return pltpu.CompilerParams(dimension_semantics=semantics,
                                vmem_limit_bytes=VMEM_LIMIT_BYTES)


def _dot(a, b):
    return jnp.dot(a, b, preferred_element_type=F32)


def _dot_nt(a, b_t):
    return lax.dot_general(a, b_t, (((1,), (1,)), ((), ())), preferred_element_type=F32)


def _cast_job(w, steps, step_index):
    rows = w.shape[0] // steps
    assert rows * steps == w.shape[0] and rows % 16 == 0, (w.shape, steps)
    return (pl.BlockSpec((rows, w.shape[1]), lambda *g: (step_index(*g), 0)),
            jax.ShapeDtypeStruct(w.shape, BF16))


def _run_cast_jobs(src_refs, dst_refs):
    for src, dst in zip(src_refs, dst_refs):
        dst[...] = src[...].astype(dst.dtype)


def _rmsnorm_kernel(x_ref, g_ref, o_ref):
    x = x_ref[...]
    ms = jnp.mean(x * x, axis=-1, keepdims=True)
    o_ref[...] = (x * lax.rsqrt(ms + RMS_EPS) * g_ref[...]).astype(o_ref.dtype)


def _rmsnorm(x, g, *, bm=RMSNORM_ROWS):
    m, d = x.shape
    return pl.pallas_call(
        _rmsnorm_kernel,
        grid=(m // bm,),
        in_specs=[pl.BlockSpec((bm, d), lambda i: (i, 0)),
                  pl.BlockSpec((1, d), lambda i: (0, 0))],
        out_specs=pl.BlockSpec((bm, d), lambda i: (i, 0)),
        out_shape=jax.ShapeDtypeStruct((m, d), BF16),
        compiler_params=_params("parallel"),
        name="rmsnorm",
    )(x, g.reshape(1, d).astype(F32))


def _qkv_kernel(a_ref, w_ref, s_ref, o_ref, *, qk_tiles):
    acc = _dot_nt(a_ref[...], w_ref[...])
    is_qk = pl.program_id(1) < qk_tiles
    for h in range(acc.shape[1] // FOX_HEAD_DIM):
        cols = slice(h * FOX_HEAD_DIM, (h + 1) * FOX_HEAD_DIM)
        blk = acc[:, cols]
        ms = jnp.mean(blk * blk, axis=-1, keepdims=True)
        inv = jnp.where(is_qk, lax.rsqrt(ms + RMS_EPS), 1.0)
        o_ref[:, cols] = (blk * inv * s_ref[:, cols]).astype(o_ref.dtype)


def _qkv_proj(a, w_in, scale, *, bm, bn):
    m, k = a.shape
    n = scale.shape[1]
    qk_tiles = 2 * FOX_HEADS * FOX_HEAD_DIM // bn
    return pl.pallas_call(
        functools.partial(_qkv_kernel, qk_tiles=qk_tiles),
        grid=(m // bm, n // bn),
        in_specs=[pl.BlockSpec((bm, k), lambda i, j: (i, 0)),
                  pl.BlockSpec((bn, k), lambda i, j: (j, 0)),
                  pl.BlockSpec((1, bn), lambda i, j: (0, j))],
        out_specs=pl.BlockSpec((bm, bn), lambda i, j: (i, j)),
        out_shape=jax.ShapeDtypeStruct((m, n), BF16),
        compiler_params=_params("parallel", "arbitrary"),
        name="qkv_proj",
    )(a, w_in, scale)


def _s5_fgate_kernel(u_ref, ws_ref, wt_ref, b_ref, wa_ref, wb_ref, s_ref, f_ref, wa_out_ref, wb_out_ref,
                     carry_ref):
    @pl.when(pl.program_id(1) == 0)
    def _():
        carry_ref[...] = jnp.zeros_like(carry_ref)

    _run_cast_jobs((wa_ref, wb_ref), (wa_out_ref, wb_out_ref))

    u = u_ref[...]
    s_ref[...] = _dot_nt(u, ws_ref[...])
    z = lax.dot_general(wt_ref[...], u, (((1,), (1,)), ((), ())),
                        preferred_element_type=F32) + b_ref[...]
    x = jnp.minimum(z, 0.0) - jnp.log1p(jnp.exp(-jnp.abs(z)))
    bs = x.shape[1]
    lane = lax.broadcasted_iota(jnp.int32, x.shape, 1)
    shift = 1
    while shift < bs:
        x = x + jnp.where(lane >= shift, pltpu.roll(x, shift, 1), 0.0)
        shift *= 2
    x = x + carry_ref[:, 0:1]
    blk = f_ref.shape[-1]
    for c in range(bs // blk):
        f_ref[:, c, 0, :] = x[:, c * blk:(c + 1) * blk] * LOG2_E
    carry_ref[...] = jnp.broadcast_to(x[:, bs - 1:bs], carry_ref.shape)


def _s5_fgate_proj(u, w_s5, wt, bias, batch, seq, blk, cast_a, cast_b, *, bs):
    h, d = wt.shape
    n = w_s5.shape[0]
    ns = seq // bs
    step = lambda b, j: b * ns + j
    (spec_a, shape_a), (spec_b, shape_b) = (_cast_job(w, batch * ns, step) for w in (cast_a, cast_b))
    return pl.pallas_call(
        _s5_fgate_kernel,
        grid=(batch, ns),
        in_specs=[pl.BlockSpec((bs, d), lambda b, j: (b * ns + j, 0)),
                  pl.BlockSpec((n, d), lambda b, j: (0, 0)),
                  pl.BlockSpec((h, d), lambda b, j: (0, 0)),
                  pl.BlockSpec((h, 1), lambda b, j: (0, 0)),
                  spec_a, spec_b],
        out_specs=[pl.BlockSpec((bs, n), lambda b, j: (b * ns + j, 0)),
                   pl.BlockSpec((None, h, bs // blk, 1, blk), lambda b, j: (b, 0, j, 0, 0)),
                   spec_a, spec_b],
        out_shape=[jax.ShapeDtypeStruct((batch * seq, n), F32),
                   jax.ShapeDtypeStruct((batch, h, seq // blk, 1, blk), F32),
                   shape_a, shape_b],
        scratch_shapes=[pltpu.VMEM((h, 128), F32)],
        compiler_params=_params("parallel", "arbitrary"),
        name="s5_in_fgate_proj",
    )(u, w_s5, wt, bias, cast_a, cast_b)


ATTN_HEADS_PER_STEP = 4


def _attn_kernel(q_ref, k_ref, v_ref, f_ref, o_ref, *, blk):
    i = pl.program_id(2)
    dh = FOX_HEAD_DIM
    heads = range(ATTN_HEADS_PER_STEP)
    qs = [q_ref[:, h * dh:(h + 1) * dh] for h in heads]

    def block(j, carry, masked):
        start = pl.multiple_of(j * blk, blk)
        out = []
        for h in heads:
            m, l, acc = carry[h]
            ks = k_ref[pl.ds(start, blk), h * dh:(h + 1) * dh]
            vs = v_ref[pl.ds(start, blk), h * dh:(h + 1) * dh]
            s = lax.dot_general(qs[h], ks, (((1,), (1,)), ((), ())), preferred_element_type=F32)
            s = s - f_ref[h, j]
            if masked:
                row = lax.broadcasted_iota(jnp.int32, s.shape, 0)
                col = lax.broadcasted_iota(jnp.int32, s.shape, 1)
                s = jnp.where(col <= row, s, MASK_VALUE)
            m_new = jnp.maximum(m, jnp.max(s, axis=-1, keepdims=True))
            alpha = jnp.exp2(m - m_new)
            p = jnp.exp2(s - m_new)
            l = alpha * l + jnp.sum(p, axis=-1, keepdims=True)
            acc = alpha * acc + _dot(p.astype(BF16), vs)
            out.append((m_new, l, acc))
        return tuple(out)

    init = tuple((jnp.full((blk, 1), MASK_VALUE, F32), jnp.zeros((blk, 1), F32),
                  jnp.zeros((blk, dh), F32)) for _ in heads)
    carry = lax.fori_loop(0, i, lambda j, c: block(j, c, False), init)
    carry = block(i, carry, True)
    for h in heads:
        _, l, acc = carry[h]
        o_ref[:, h * dh:(h + 1) * dh] = (acc / l).astype(o_ref.dtype)


def _attn_shift_kernel(c_ref, q_ref, k_ref, v_ref, f_ref, o_ref, acc_ref, shift_ref, p_ref, *, blk):
    i = pl.program_id(2)
    dh = FOX_HEAD_DIM
    heads = range(ATTN_HEADS_PER_STEP)
    qs = [q_ref[:, h * dh:(h + 1) * dh] for h in heads]
    ones = jnp.ones((blk, dh), BF16)

    def setup():
        for h in heads:
            col = jnp.broadcast_to(f_ref[h, i], (8, blk)).T[:, 0:1] - c_ref[0]
            shift_ref[h] = jnp.broadcast_to(col, (blk, dh))
        acc_ref[...] = jnp.zeros_like(acc_ref)

    def probabilities(h, j, masked):
        start = pl.multiple_of(j * blk, blk)
        ks = k_ref[pl.ds(start, blk), h * dh:(h + 1) * dh]
        s = lax.dot_general(qs[h], ks, (((1,), (1,)), ((), ())), preferred_element_type=F32)
        s = (s + jnp.concatenate([shift_ref[h]] * (blk // dh), axis=1)) - f_ref[h, j]
        if masked:
            row = lax.broadcasted_iota(jnp.int32, s.shape, 0)
            col = lax.broadcasted_iota(jnp.int32, s.shape, 1)
            s = jnp.where(col <= row, s, MASK_VALUE)
        return jnp.exp2(s).astype(BF16)

    def values(h, j):
        start = pl.multiple_of(j * blk, blk)
        return jnp.concatenate([v_ref[pl.ds(start, blk), h * dh:(h + 1) * dh], ones], axis=1)

    setup()

    @pl.when(i == 0)
    def _():
        p_ref[...] = jnp.zeros_like(p_ref)

    @pl.when(i > 0)
    def _():
        for h in heads:
            p_ref[h] = probabilities(h, 0, False)

    @pl.loop(1, i)
    def _(j):
        for h in heads:
            acc_ref[h] += _dot(p_ref[h], values(h, j - 1))
        for h in heads:
            p_ref[h] = probabilities(h, j, False)

    prev = jnp.maximum(i - 1, 0)
    for h in heads:
        acc_ref[h] += _dot(p_ref[h], values(h, prev)) + _dot(probabilities(h, i, True), values(h, i))
    for h in heads:
        acc = acc_ref[h]
        o_ref[:, h * dh:(h + 1) * dh] = (acc[:, :dh] / acc[:, dh:]).astype(o_ref.dtype)


ATTN_SHIFT_LIMIT = 56.0


def _attention(qkv, f_rows, qk_bound, *, blk):
    batch, seq, _ = qkv.shape
    nblk = seq // blk
    hps = ATTN_HEADS_PER_STEP
    wide = hps * FOX_HEAD_DIM
    third = FOX_HEADS // hps
    specs = [pl.BlockSpec((None, blk, wide), lambda b, h, i: (b, i, h)),
             pl.BlockSpec((None, seq, wide), lambda b, h, i: (b, 0, h + third)),
             pl.BlockSpec((None, seq, wide), lambda b, h, i: (b, 0, h + 2 * third)),
             pl.BlockSpec((None, hps, nblk, 1, blk), lambda b, h, i: (b, h, 0, 0, 0))]
    common = dict(
        grid=(batch, FOX_HEADS // hps, nblk),
        out_specs=pl.BlockSpec((None, blk, wide), lambda b, h, i: (b, i, h)),
        out_shape=jax.ShapeDtypeStruct((batch, seq, FOX_HEADS * FOX_HEAD_DIM), BF16),
        compiler_params=_params("parallel", "parallel", "arbitrary"))

    def shifted(qkv, f_rows, c):
        return pl.pallas_call(
            functools.partial(_attn_shift_kernel, blk=blk),
            in_specs=[pl.BlockSpec(memory_space=pltpu.SMEM)] + specs,
            scratch_shapes=[pltpu.VMEM((hps, blk, 2 * FOX_HEAD_DIM), F32),
                            pltpu.VMEM((hps, blk, FOX_HEAD_DIM), F32),
                            pltpu.VMEM((hps, blk, blk), BF16)],
            name="fox_attention_shift", **common)(c.reshape(1), qkv, qkv, qkv, f_rows)

    def online(qkv, f_rows, c):
        return pl.pallas_call(functools.partial(_attn_kernel, blk=blk), in_specs=specs,
                              name="fox_attention_online", **common)(qkv, qkv, qkv, f_rows)

    return lax.cond(qk_bound <= ATTN_SHIFT_LIMIT, shifted, online, qkv, f_rows, qk_bound)


S5_SLAB_GROUPS = 16
S5_SLAB = S5_SLAB_GROUPS * S5_GROUP
S5_SLAB_STATES = S5_SLAB_GROUPS * S5_STATE
S5_SLABS = S5_GROUPS // S5_SLAB_GROUPS


def _split_bf16(a):
    hi = a.astype(BF16)
    return hi, (a - hi.astype(F32)).astype(BF16)


def _dot_split(a, b_hi, b_lo):
    a_hi, a_lo = _split_bf16(a)
    return _dot(a_hi, b_hi) + _dot(a_hi, b_lo) + _dot(a_lo, b_hi)


def _lam_pow(k, lr, li, dt):
    mag = jnp.exp(k * (lr * dt))
    ang = k * (li * dt)
    return mag * jnp.cos(ang), mag * jnp.sin(ang)


def _s5_matrices_kernel(lam_ref, ls_ref, bdb_ref, bdc_ref, bp_ref, cp_ref, gr_ref, l16_ref,
                        bb_ref, chi_ref, clo_ref):
    t = pl.program_id(1)
    ns = S5_SLAB_STATES
    lr, li = lam_ref[0:1, :], lam_ref[1:2, :]
    dt = jnp.exp(ls_ref[...])

    @pl.when(t == 0)
    def _():
        lbr, lbi = _lam_pow(1.0, lr, li, dt)
        den = lr * lr + li * li
        nr = lbr - 1.0
        fr = (nr * lr + lbi * li) / den
        fi = (lbi * lr - nr * li) / den
        b_re, b_im = bdb_ref[0], bdb_ref[1]
        bb_ref[0] = fr * b_re - fi * b_im
        bb_ref[1] = fr * b_im + fi * b_re
        for part in range(2):
            chi_ref[part], clo_ref[part] = _split_bf16(bdc_ref[part])

    bb_re, bb_im = bb_ref[0], bb_ref[1]
    pr, pi = _lam_pow((S5_CHUNK - 1 - t).astype(F32), lr, li, dt)
    bp_re = bb_re * pr - bb_im * pi
    bp_im = bb_re * pi + bb_im * pr
    bp_ref[:, :ns] = bp_re.astype(BF16)
    bp_ref[:, ns:] = bp_im.astype(BF16)

    qr, qi = _lam_pow((t + 1).astype(F32), lr, li, dt)
    qr = jnp.broadcast_to(qr, (8, ns)).T[:, 0:1]
    qi = jnp.broadcast_to(qi, (8, ns)).T[:, 0:1]
    c_re, c_im = bdc_ref[0], bdc_ref[1]
    cp_ref[:ns, :] = (c_re * qr - c_im * qi).astype(BF16)
    cp_ref[ns:, :] = (-(c_re * qi + c_im * qr)).astype(BF16)

    gr_ref[...] = (_dot_split(bp_re, chi_ref[0], clo_ref[0])
                   - _dot_split(bp_im, chi_ref[1], clo_ref[1])).astype(BF16)

    l16r, l16i = _lam_pow(float(S5_CHUNK), lr, li, dt)
    l16_ref[0:1, :] = l16r
    l16_ref[1:2, :] = l16i


def _s5_matrices(lam, ls, bdb, bdc):
    t_len, w, ns = S5_CHUNK, S5_SLAB, S5_SLAB_STATES
    return pl.pallas_call(
        _s5_matrices_kernel,
        grid=(S5_SLABS, t_len),
        in_specs=[pl.BlockSpec((None, 2, ns), lambda s, t: (s, 0, 0)),
                  pl.BlockSpec((None, 1, ns), lambda s, t: (s, 0, 0)),
                  pl.BlockSpec((None, 2, w, ns), lambda s, t: (s, 0, 0, 0)),
                  pl.BlockSpec((None, 2, ns, w), lambda s, t: (s, 0, 0, 0))],
        out_specs=[pl.BlockSpec((None, w, 2 * ns), lambda s, t: (s, t, 0)),
                   pl.BlockSpec((None, None, 2 * ns, w), lambda s, t: (s, t, 0, 0)),
                   pl.BlockSpec((None, w, w), lambda s, t: (s, t, 0)),
                   pl.BlockSpec((None, 2, ns), lambda s, t: (s, 0, 0))],
        out_shape=[jax.ShapeDtypeStruct((S5_SLABS, t_len * w, 2 * ns), BF16),
                   jax.ShapeDtypeStruct((S5_SLABS, t_len, 2 * ns, w), BF16),
                   jax.ShapeDtypeStruct((S5_SLABS, t_len * w, w), BF16),
                   jax.ShapeDtypeStruct((S5_SLABS, 2, ns), F32)],
        scratch_shapes=[pltpu.VMEM((2, w, ns), F32), pltpu.VMEM((2, ns, w), BF16), pltpu.VMEM((2, ns, w), BF16)],
        compiler_params=_params("parallel", "arbitrary"),
        name="s5_matrices",
    )(lam, ls, bdb, bdc)


def _chunk_rows(x_ref):
    return jnp.concatenate([x_ref[:, t, :] for t in range(S5_CHUNK)], axis=1).astype(BF16)


def _s5_states_kernel(x_ref, bp_ref, s_ref):
    s_ref[...] = _dot(_chunk_rows(x_ref), bp_ref[...])


def _s5_states(x3, bp, *, rows):
    nc = x3.shape[0]
    t_len, w, ns = S5_CHUNK, S5_SLAB, S5_SLAB_STATES
    return pl.pallas_call(
        _s5_states_kernel,
        grid=(S5_SLABS, nc // rows),
        in_specs=[pl.BlockSpec((rows, t_len, w), lambda s, r: (r, 0, s)),
                  pl.BlockSpec((None, t_len * w, 2 * ns), lambda s, r: (s, 0, 0),
                               pipeline_mode=pl.Buffered(1))],
        out_specs=pl.BlockSpec((rows, 2 * ns), lambda s, r: (r, s)),
        out_shape=jax.ShapeDtypeStruct((nc, S5_SLABS * 2 * ns), F32),
        compiler_params=_params("parallel", "arbitrary"),
        name="s5_chunk_states",
    )(x3, bp)


def _s5_chunk_scan_kernel(s_ref, l16_ref, xp_ref, *, batch, chunks):
    ns = S5_SLAB_STATES
    ar, ai = l16_ref[0:1, :], l16_ref[1:2, :]

    def step(c, carry):
        out = []
        for b, (xr, xi) in enumerate(carry):
            row = pl.ds(b * chunks + c, 1)
            xp_ref[row, :ns] = xr
            xp_ref[row, ns:] = xi
            sr = s_ref[row, :ns]
            si = s_ref[row, ns:]
            out.append((ar * xr - ai * xi + sr, ar * xi + ai * xr + si))
        return tuple(out)

    zero = jnp.zeros((1, ns), F32)
    lax.fori_loop(0, chunks, step, ((zero, zero),) * batch)


def _s5_chunk_scan(s, l16, *, batch):
    nc = s.shape[0]
    ns = S5_SLAB_STATES
    return pl.pallas_call(
        functools.partial(_s5_chunk_scan_kernel, batch=batch, chunks=nc // batch),
        grid=(S5_SLABS,),
        in_specs=[pl.BlockSpec((nc, 2 * ns), lambda s_: (0, s_)),
                  pl.BlockSpec((None, 2, ns), lambda s_: (s_, 0, 0))],
        out_specs=pl.BlockSpec((nc, 2 * ns), lambda s_: (0, s_)),
        out_shape=jax.ShapeDtypeStruct(s.shape, F32),
        compiler_params=_params("parallel"),
        name="s5_chunk_scan",
    )(s, l16)


def _s5_output_kernel(x_ref, xp_ref, gr_ref, cp_ref, d_ref, y_ref):
    t_len, w = S5_CHUNK, S5_SLAB
    xc = _chunk_rows(x_ref)
    xp = xp_ref[...].astype(BF16)
    d = d_ref[...]
    for t in range(t_len):
        y = _dot(xc[:, :(t + 1) * w], gr_ref[(t_len - 1 - t) * w:, :])
        y = y + _dot(xp, cp_ref[t])
        y_ref[:, t, :] = y + d * x_ref[:, t, :]


def _s5_output(x3, xp, gr, cp, d, *, rows):
    nc = x3.shape[0]
    t_len, w, ns = S5_CHUNK, S5_SLAB, S5_SLAB_STATES
    once = dict(pipeline_mode=pl.Buffered(1))
    return pl.pallas_call(
        _s5_output_kernel,
        grid=(S5_SLABS, nc // rows),
        in_specs=[pl.BlockSpec((rows, t_len, w), lambda s, r: (r, 0, s)),
                  pl.BlockSpec((rows, 2 * ns), lambda s, r: (r, s)),
                  pl.BlockSpec((None, t_len * w, w), lambda s, r: (s, 0, 0), **once),
                  pl.BlockSpec((None, t_len, 2 * ns, w), lambda s, r: (s, 0, 0, 0), **once),
                  pl.BlockSpec((None, 1, w), lambda s, r: (s, 0, 0))],
        out_specs=pl.BlockSpec((rows, t_len, w), lambda s, r: (r, 0, s)),
        out_shape=jax.ShapeDtypeStruct(x3.shape, F32),
        compiler_params=_params("parallel", "arbitrary"),
        name="s5_chunk_output",
    )(x3, xp, gr, cp, d)


def _s5(s5_in, lam_re, lam_im, log_step, b_re, b_im, c_re, c_im, d, *, batch):
    m, width = s5_in.shape
    sl, sg, p, grp = S5_SLABS, S5_SLAB_GROUPS, S5_STATE, S5_GROUP
    ns = S5_SLAB_STATES
    lam = jnp.stack([lam_re, lam_im], axis=0).astype(F32).reshape(2, sl, ns).transpose(1, 0, 2)
    ls = jnp.repeat(log_step.astype(F32), p).reshape(sl, 1, ns)
    eye = jnp.eye(sg, dtype=F32)
    b = jnp.stack([b_re, b_im], axis=0).astype(F32).reshape(2, sl, sg, p, grp)
    bdb = jnp.einsum('rsgpi,gh->srgihp', b, eye).reshape(sl, 2, sg * grp, ns)
    c = jnp.stack([c_re, c_im], axis=0).astype(F32).reshape(2, sl, sg, grp, p)
    bdc = jnp.einsum('rsgop,gh->srhpgo', c, eye).reshape(sl, 2, ns, sg * grp)
    bp, cp, gr, l16 = _s5_matrices(lam, ls, bdb, bdc)
    x3 = s5_in.reshape(m // S5_CHUNK, S5_CHUNK, width)
    states = _s5_states(x3, bp, rows=S5_STATE_ROWS)
    xprev = _s5_chunk_scan(states, l16, batch=batch)
    y3 = _s5_output(x3, xprev, gr, cp, d.astype(F32).reshape(sl, 1, sg * grp), rows=S5_OUT_ROWS)
    return y3.reshape(m, width)


def _glu_kernel(y_ref, w_ref, b_ref, o_ref):
    g = jax.nn.gelu(y_ref[...])
    o_ref[...] = (g * jax.nn.sigmoid(_dot(g.astype(BF16), w_ref[...]) + b_ref[...])).astype(o_ref.dtype)


def _glu(y, w, b, *, bm=512):
    m, n = y.shape
    return pl.pallas_call(
        _glu_kernel,
        grid=(m // bm,),
        in_specs=[pl.BlockSpec((bm, n), lambda i: (i, 0)),
                  pl.BlockSpec((n, n), lambda i: (0, 0)),
                  pl.BlockSpec((1, n), lambda i: (0, 0))],
        out_specs=pl.BlockSpec((bm, n), lambda i: (i, 0)),
        out_shape=jax.ShapeDtypeStruct((m, n), BF16),
        compiler_params=_params("parallel"),
        name="s5_glu",
    )(y, w, b)


def _merge_kernel(u_ref, a_ref, s_ref, wgf_ref, wgs_ref, wpf_ref, wps_ref, bgf_ref, bgs_ref, wc1_ref, wc2_ref,
                  o_ref, wc1_out_ref, wc2_out_ref):
    _run_cast_jobs((wc1_ref, wc2_ref), (wc1_out_ref, wc2_out_ref))
    u = u_ref[...]
    gate_fox = jax.nn.sigmoid(_dot_nt(u, wgf_ref[...]) + bgf_ref[...])
    gate_s5 = jax.nn.sigmoid(_dot_nt(u, wgs_ref[...]) + bgs_ref[...])
    o_ref[...] = (gate_fox * _dot(a_ref[...], wpf_ref[...])
                  + gate_s5 * _dot(s_ref[...], wps_ref[...])).astype(o_ref.dtype)


def _merge(u, attn, ssm, wgf, wgs, wpf, wps, bgf, bgs, cast_1, cast_2, *, bm, bn):
    m, d = u.shape
    n = wpf.shape[1]
    nn = n // bn
    act = lambda a: pl.BlockSpec((bm, a.shape[1]), lambda i, j: (i, 0))
    wgt = lambda w: pl.BlockSpec((w.shape[0], bn), lambda i, j: (0, j))
    wgt_t = lambda w: pl.BlockSpec((bn, w.shape[1]), lambda i, j: (j, 0))
    step = lambda i, j: i * nn + j
    (spec_1, shape_1), (spec_2, shape_2) = (_cast_job(w, (m // bm) * nn, step) for w in (cast_1, cast_2))
    return pl.pallas_call(
        _merge_kernel,
        grid=(m // bm, nn),
        in_specs=[act(u), act(attn), act(ssm), wgt_t(wgf), wgt_t(wgs), wgt(wpf), wgt(wps),
                  pl.BlockSpec((1, bn), lambda i, j: (0, j)),
                  pl.BlockSpec((1, bn), lambda i, j: (0, j)),
                  spec_1, spec_2],
        out_specs=[pl.BlockSpec((bm, bn), lambda i, j: (i, j)), spec_1, spec_2],
        out_shape=[jax.ShapeDtypeStruct((m, n), BF16), shape_1, shape_2],
        compiler_params=_params("parallel", "arbitrary"),
        name="gated_merge",
    )(u, attn, ssm, wgf, wgs, wpf, wps, bgf, bgs, cast_1, cast_2)


NORM_LANES = 128


def _out_proj_kernel(a_ref, w_ref, x_ref, g_ref, h_ref, hg_ref, r_ref, ssq_ref, *, d_model):
    j = pl.program_id(1)

    @pl.when(j == 0)
    def _():
        ssq_ref[...] = jnp.zeros_like(ssq_ref)

    h = x_ref[...] + _dot(a_ref[...], w_ref[...])
    h_ref[...] = h
    hg_ref[...] = (h * g_ref[...]).astype(hg_ref.dtype)
    ssq_ref[...] += jnp.sum(h * h, axis=-1, keepdims=True)

    @pl.when(j == pl.num_programs(1) - 1)
    def _():
        r_ref[...] = jnp.broadcast_to(lax.rsqrt(ssq_ref[...] * (1.0 / d_model) + RMS_EPS), r_ref.shape)


def _out_proj(a, w, x, g, *, bm, bn):
    m, k = a.shape
    n = w.shape[1]
    return pl.pallas_call(
        functools.partial(_out_proj_kernel, d_model=n),
        grid=(m // bm, n // bn),
        in_specs=[pl.BlockSpec((bm, k), lambda i, j: (i, 0)),
                  pl.BlockSpec((k, bn), lambda i, j: (0, j)),
                  pl.BlockSpec((bm, bn), lambda i, j: (i, j)),
                  pl.BlockSpec((1, bn), lambda i, j: (0, j))],
        out_specs=[pl.BlockSpec((bm, bn), lambda i, j: (i, j)),
                   pl.BlockSpec((bm, bn), lambda i, j: (i, j)),
                   pl.BlockSpec((bm, NORM_LANES), lambda i, j: (i, 0))],
        out_shape=[jax.ShapeDtypeStruct((m, n), F32),
                   jax.ShapeDtypeStruct((m, n), BF16),
                   jax.ShapeDtypeStruct((m, NORM_LANES), F32)],
        scratch_shapes=[pltpu.VMEM((bm, 1), F32)],
        compiler_params=_params("parallel", "arbitrary"),
        name="out_proj_residual",
    )(a, w, x, g)


def _swiglu_kernel(a_ref, r_ref, wg_ref, wu_ref, wc_ref, o_ref, wc_out_ref):
    _run_cast_jobs((wc_ref,), (wc_out_ref,))
    a = a_ref[...]
    r = jnp.concatenate([r_ref[...]] * (o_ref.shape[1] // NORM_LANES), axis=1)
    gate = _dot(a, wg_ref[...]) * r
    up = _dot(a, wu_ref[...]) * r
    o_ref[...] = (jax.nn.silu(gate) * up).astype(o_ref.dtype)


def _swiglu_up(a, r, w_gate_up, d_ff, cast_w, *, bm, bn):
    m, k = a.shape
    ni, nj = m // bm, d_ff // bn
    cast_spec, cast_shape = _cast_job(cast_w, ni * nj, lambda i, j: i * nj + j)
    return pl.pallas_call(
        _swiglu_kernel,
        grid=(ni, nj),
        in_specs=[pl.BlockSpec((bm, k), lambda i, j: (i, 0)),
                  pl.BlockSpec((bm, NORM_LANES), lambda i, j: (i, 0)),
                  pl.BlockSpec((k, bn), lambda i, j: (0, j)),
                  pl.BlockSpec((k, bn), lambda i, j: (0, j + nj)),
                  cast_spec],
        out_specs=[pl.BlockSpec((bm, bn), lambda i, j: (i, j)), cast_spec],
        out_shape=[jax.ShapeDtypeStruct((m, d_ff), BF16), cast_shape],
        compiler_params=_params("parallel", "arbitrary"),
        name="swiglu_up",
    )(a, r, w_gate_up, w_gate_up, cast_w)


def _down_kernel(a_ref, w_ref, r_ref, o_ref):
    o_ref[...] = r_ref[...] + _dot(a_ref[...], w_ref[...])


def _down_residual(a, w, res, *, bm, bn):
    m, k = a.shape
    n = w.shape[1]
    return pl.pallas_call(
        _down_kernel,
        grid=(m // bm, n // bn),
        in_specs=[pl.BlockSpec((bm, k), lambda i, j: (i, 0)),
                  pl.BlockSpec((k, bn), lambda i, j: (0, j)),
                  pl.BlockSpec((bm, bn), lambda i, j: (i, j))],
        out_specs=pl.BlockSpec((bm, bn), lambda i, j: (i, j)),
        out_shape=jax.ShapeDtypeStruct((m, n), F32),
        compiler_params=_params("parallel", "arbitrary"),
        name="down_proj_residual",
    )(a, w, res)


def _layer(x, batch, seq, g_mix, w_in, b_fgate, b_gates, q_norm, k_norm,
           lam_re, lam_im, log_step, b_re, b_im, c_re, c_im, s5_d,
           w_glu, b_glu, w_proj_fox, w_proj_s5, w_out, g_ffn, w_gate_up, w_down):
    m, d_model = x.shape
    fox_w = FOX_HEADS * FOX_HEAD_DIM
    s5_w = S5_GROUP * S5_GROUPS
    col_k, col_v = fox_w, 2 * fox_w
    col_f = 3 * fox_w
    col_s5 = col_f + FOX_HEADS
    col_g = col_s5 + s5_w
    d_ff = w_down.shape[0]
    w_in_t = jnp.swapaxes(w_in, 0, 1).astype(BF16)
    w_f_t, w_s5_t = w_in_t[col_f:col_s5], w_in_t[col_s5:col_g]
    w_gate_fox_t, w_gate_s5_t = w_in_t[col_g:col_g + d_model], w_in_t[col_g + d_model:col_g + 2 * d_model]

    u = _rmsnorm(x, g_mix)

    head_scale = jnp.concatenate([
        jnp.tile(q_norm.astype(F32), FOX_HEADS) * (LOG2_E / math.sqrt(FOX_HEAD_DIM)),
        jnp.tile(k_norm.astype(F32), FOX_HEADS),
        jnp.ones((fox_w,), F32)]).reshape(1, 3 * fox_w)
    qkv = _qkv_proj(u, w_in_t, head_scale, **TILES["qkv_proj"])
    s5_in, f_rows, w_proj_fox_b, w_proj_s5_b = _s5_fgate_proj(
        u, w_s5_t, w_f_t, b_fgate.astype(F32).reshape(FOX_HEADS, 1), batch, seq, ATTN_BLOCK, w_proj_fox, w_proj_s5,
        bs=S5_IN_ROWS)

    qk_bound = (1.02 * LOG2_E * math.sqrt(FOX_HEAD_DIM)
                * jnp.max(jnp.abs(q_norm.astype(F32))) * jnp.max(jnp.abs(k_norm.astype(F32))))
    attn = _attention(qkv.reshape(batch, seq, 3 * fox_w), f_rows, qk_bound, blk=ATTN_BLOCK).reshape(m, fox_w)

    y5 = _s5(s5_in, lam_re, lam_im, log_step, b_re, b_im, c_re, c_im, s5_d, batch=batch)
    ssm = _glu(y5, w_glu.astype(BF16), b_glu.astype(F32).reshape(1, s5_w))

    b_gates = b_gates.astype(F32).reshape(1, 2 * d_model)
    merged, w_out_b, w_gate_up_b = _merge(u, attn, ssm, w_gate_fox_t, w_gate_s5_t, w_proj_fox_b, w_proj_s5_b,
                                          b_gates[:, :d_model], b_gates[:, d_model:], w_out, w_gate_up,
                                          **TILES["gated_merge"])
    h, hg, h_inv_rms = _out_proj(merged, w_out_b, x, g_ffn.astype(F32).reshape(1, d_model), **TILES["out_proj"])
    act, w_down_b = _swiglu_up(hg, h_inv_rms, w_gate_up_b, d_ff, w_down, **TILES["swiglu_up"])
    return _down_residual(act, w_down_b, h, **TILES["down_proj"])


def kernel(x, g_mix, w_in, b_fgate, b_gates, q_norm, k_norm, s5_lambda_re, s5_lambda_im, s5_log_step, s5_b_re, s5_b_im, s5_c_re, s5_c_im, s5_d, w_glu, b_glu, w_proj_fox, w_proj_s5, w_out, g_ffn, w_gate_up, w_down):
    batch, seq, d_model = x.shape
    h = x.reshape(batch * seq, d_model)
    for l in range(g_mix.shape[0]):
        h = _layer(h, batch, seq, g_mix[l], w_in[l], b_fgate[l], b_gates[l], q_norm[l], k_norm[l],
                   s5_lambda_re[l], s5_lambda_im[l], s5_log_step[l], s5_b_re[l], s5_b_im[l],
                   s5_c_re[l], s5_c_im[l], s5_d[l], w_glu[l], b_glu[l],
                   w_proj_fox[l], w_proj_s5[l], w_out[l], g_ffn[l], w_gate_up[l], w_down[l])
    return h.reshape(batch, seq, d_model)
```

```python
import functools
import math

import jax
import jax.numpy as jnp
from jax import lax
from jax.experimental import pallas as pl
from jax.experimental.pallas import tpu as pltpu

F32 = jnp.float32
BF16 = jnp.bfloat16

FOX_HEADS = 16
FOX_HEAD_DIM = 128
S5_GROUP = 16
S5_GROUPS = 64
S5_STATE = 64
S5_CHUNK = 16
RMS_EPS = 1e-6
MASK_VALUE = -1e30
LOG2_E = math.log2(math.e)

VMEM_LIMIT_BYTES = 56 * 1024 * 1024

TILES = {
    "qkv_proj": dict(bm=1024, bn=1024),
    "gated_merge": dict(bm=1024, bn=256),
    "out_proj": dict(bm=1024, bn=512),
    "swiglu_up": dict(bm=2048, bn=256),
    "down_proj": dict(bm=512, bn=512),
}
RMSNORM_ROWS = 256
ATTN_BLOCK = 512
S5_IN_ROWS = 1024
S5_STATE_ROWS = 512
S5_OUT_ROWS = 256


def _params(*semantics):
    return pltpu.CompilerParams(dimension_semantics=semantics,
                                vmem_limit_bytes=VMEM_LIMIT_BYTES)


def _dot(a, b):
    return jnp.dot(a, b, preferred_element_type=F32)


def _dot_nt(a, b_t):
    return lax.dot_general(a, b_t, (((1,), (1,)), ((), ())), preferred_element_type=F32)


def _cast_job(w, steps, step_index):
    rows = w.shape[0] // steps
    assert rows * steps == w.shape[0] and rows % 16 == 0, (w.shape, steps)
    return (pl.BlockSpec((rows, w.shape[1]), lambda *g: (step_index(*g), 0)),
            jax.ShapeDtypeStruct(w.shape, BF16))


def _run_cast_jobs(src_refs, dst_refs):
    for src, dst in zip(src_refs, dst_refs):
        dst[...] = src[...].astype(dst.dtype)


def _rmsnorm_kernel(x_ref, g_ref, o_ref):
    x = x_ref[...]
    ms = jnp.mean(x * x, axis=-1, keepdims=True)
    o_ref[...] = (x * lax.rsqrt(ms + RMS_EPS) * g_ref[...]).astype(o_ref.dtype)


def _rmsnorm(x, g, *, bm=RMSNORM_ROWS):
    m, d = x.shape
    return pl.pallas_call(
        _rmsnorm_kernel,
        grid=(m // bm,),
        in_specs=[pl.BlockSpec((bm, d), lambda i: (i, 0)),
                  pl.BlockSpec((1, d), lambda i: (0, 0))],
        out_specs=pl.BlockSpec((bm, d), lambda i: (i, 0)),
        out_shape=jax.ShapeDtypeStruct((m, d), BF16),
        compiler_params=_params("parallel"),
        name="rmsnorm",
    )(x, g.reshape(1, d).astype(F32))


def _qkv_kernel(a_ref, w_ref, s_ref, o_ref, *, qk_tiles):
    acc = _dot_nt(a_ref[...], w_ref[...])
    is_qk = pl.program_id(1) < qk_tiles
    for h in range(acc.shape[1] // FOX_HEAD_DIM):
        cols = slice(h * FOX_HEAD_DIM, (h + 1) * FOX_HEAD_DIM)
        blk = acc[:, cols]
        ms = jnp.mean(blk * blk, axis=-1, keepdims=True)
        inv = jnp.where(is_qk, lax.rsqrt(ms + RMS_EPS), 1.0)
        o_ref[:, cols] = (blk * inv * s_ref[:, cols]).astype(o_ref.dtype)


def _qkv_proj(a, w_in, scale, *, bm, bn):
    m, k = a.shape
    n = scale.shape[1]
    qk_tiles = 2 * FOX_HEADS * FOX_HEAD_DIM // bn
    return pl.pallas_call(
        functools.partial(_qkv_kernel, qk_tiles=qk_tiles),
        grid=(m // bm, n // bn),
        in_specs=[pl.BlockSpec((bm, k), lambda i, j: (i, 0)),
                  pl.BlockSpec((bn, k), lambda i, j: (j, 0)),
                  pl.BlockSpec((1, bn), lambda i, j: (0, j))],
        out_specs=pl.BlockSpec((bm, bn), lambda i, j: (i, j)),
        out_shape=jax.ShapeDtypeStruct((m, n), BF16),
        compiler_params=_params("parallel", "arbitrary"),
        name="qkv_proj",
    )(a, w_in, scale)


def _s5_fgate_kernel(u_ref, ws_ref, wt_ref, b_ref, wa_ref, wb_ref, s_ref, f_ref, wa_out_ref, wb_out_ref,
                     carry_ref):
    @pl.when(pl.program_id(1) == 0)
    def _():
        carry_ref[...] = jnp.zeros_like(carry_ref)

    _run_cast_jobs((wa_ref, wb_ref), (wa_out_ref, wb_out_ref))

    u = u_ref[...]
    s_ref[...] = _dot_nt(u, ws_ref[...])
    z = lax.dot_general(wt_ref[...], u, (((1,), (1,)), ((), ())),
                        preferred_element_type=F32) + b_ref[...]
    x = jnp.minimum(z, 0.0) - jnp.log1p(jnp.exp(-jnp.abs(z)))
    bs = x.shape[1]
    lane = lax.broadcasted_iota(jnp.int32, x.shape, 1)
    shift = 1
    while shift < bs:
        x = x + jnp.where(lane >= shift, pltpu.roll(x, shift, 1), 0.0)
        shift *= 2
    x = x + carry_ref[:, 0:1]
    blk = f_ref.shape[-1]
    for c in range(bs // blk):
        f_ref[:, c, 0, :] = x[:, c * blk:(c + 1) * blk] * LOG2_E
    carry_ref[...] = jnp.broadcast_to(x[:, bs - 1:bs], carry_ref.shape)


def _s5_fgate_proj(u, w_s5, wt, bias, batch, seq, blk, cast_a, cast_b, *, bs):
    h, d = wt.shape
    n = w_s5.shape[0]
    ns = seq // bs
    step = lambda b, j: b * ns + j
    (spec_a, shape_a), (spec_b, shape_b) = (_cast_job(w, batch * ns, step) for w in (cast_a, cast_b))
    return pl.pallas_call(
        _s5_fgate_kernel,
        grid=(batch, ns),
        in_specs=[pl.BlockSpec((bs, d), lambda b, j: (b * ns + j, 0)),
                  pl.BlockSpec((n, d), lambda b, j: (0, 0)),
                  pl.BlockSpec((h, d), lambda b, j: (0, 0)),
                  pl.BlockSpec((h, 1), lambda b, j: (0, 0)),
                  spec_a, spec_b],
        out_specs=[pl.BlockSpec((bs, n), lambda b, j: (b * ns + j, 0)),
                   pl.BlockSpec((None, h, bs // blk, 1, blk), lambda b, j: (b, 0, j, 0, 0)),
                   spec_a, spec_b],
        out_shape=[jax.ShapeDtypeStruct((batch * seq, n), F32),
                   jax.ShapeDtypeStruct((batch, h, seq // blk, 1, blk), F32),
                   shape_a, shape_b],
        scratch_shapes=[pltpu.VMEM((h, 128), F32)],
        compiler_params=_params("parallel", "arbitrary"),
        name="s5_in_fgate_proj",
    )(u, w_s5, wt, bias, cast_a, cast_b)


ATTN_HEADS_PER_STEP = 4


def _attn_kernel(q_ref, k_ref, v_ref, f_ref, o_ref, *, blk):
    i = pl.program_id(2)
    dh = FOX_HEAD_DIM
    heads = range(ATTN_HEADS_PER_STEP)
    qs = [q_ref[:, h * dh:(h + 1) * dh] for h in heads]

    def block(j, carry, masked):
        start = pl.multiple_of(j * blk, blk)
        out = []
        for h in heads:
            m, l, acc = carry[h]
            ks = k_ref[pl.ds(start, blk), h * dh:(h + 1) * dh]
            vs = v_ref[pl.ds(start, blk), h * dh:(h + 1) * dh]
            s = lax.dot_general(qs[h], ks, (((1,), (1,)), ((), ())), preferred_element_type=F32)
            s = s - f_ref[h, j]
            if masked:
                row = lax.broadcasted_iota(jnp.int32, s.shape, 0)
                col = lax.broadcasted_iota(jnp.int32, s.shape, 1)
                s = jnp.where(col <= row, s, MASK_VALUE)
            m_new = jnp.maximum(m, jnp.max(s, axis=-1, keepdims=True))
            alpha = jnp.exp2(m - m_new)
            p = jnp.exp2(s - m_new)
            l = alpha * l + jnp.sum(p, axis=-1, keepdims=True)
            acc = alpha * acc + _dot(p.astype(BF16), vs)
            out.append((m_new, l, acc))
        return tuple(out)

    init = tuple((jnp.full((blk, 1), MASK_VALUE, F32), jnp.zeros((blk, 1), F32),
                  jnp.zeros((blk, dh), F32)) for _ in heads)
    carry = lax.fori_loop(0, i, lambda j, c: block(j, c, False), init)
    carry = block(i, carry, True)
    for h in heads:
        _, l, acc = carry[h]
        o_ref[:, h * dh:(h + 1) * dh] = (acc / l).astype(o_ref.dtype)


def _attn_shift_kernel(c_ref, q_ref, k_ref, v_ref, f_ref, o_ref, acc_ref, shift_ref, p_ref, *, blk):
    i = pl.program_id(2)
    dh = FOX_HEAD_DIM
    heads = range(ATTN_HEADS_PER_STEP)
    qs = [q_ref[:, h * dh:(h + 1) * dh] for h in heads]
    ones = jnp.ones((blk, dh), BF16)

    def setup():
        for h in heads:
            col = jnp.broadcast_to(f_ref[h, i], (8, blk)).T[:, 0:1] - c_ref[0]
            shift_ref[h] = jnp.broadcast_to(col, (blk, dh))
        acc_ref[...] = jnp.zeros_like(acc_ref)

    def probabilities(h, j, masked):
        start = pl.multiple_of(j * blk, blk)
        ks = k_ref[pl.ds(start, blk), h * dh:(h + 1) * dh]
        s = lax.dot_general(qs[h], ks, (((1,), (1,)), ((), ())), preferred_element_type=F32)
        s = (s + jnp.concatenate([shift_ref[h]] * (blk // dh), axis=1)) - f_ref[h, j]
        if masked:
            row = lax.broadcasted_iota(jnp.int32, s.shape, 0)
            col = lax.broadcasted_iota(jnp.int32, s.shape, 1)
            s = jnp.where(col <= row, s, MASK_VALUE)
        return jnp.exp2(s).astype(BF16)

    def values(h, j):
        start = pl.multiple_of(j * blk, blk)
        return jnp.concatenate([v_ref[pl.ds(start, blk), h * dh:(h + 1) * dh], ones], axis=1)

    setup()

    @pl.when(i == 0)
    def _():
        p_ref[...] = jnp.zeros_like(p_ref)

    @pl.when(i > 0)
    def _():
        for h in heads:
            p_ref[h] = probabilities(h, 0, False)

    @pl.loop(1, i)
    def _(j):
        for h in heads:
            acc_ref[h] += _dot(p_ref[h], values(h, j - 1))
        for h in heads:
            p_ref[h] = probabilities(h, j, False)

    prev = jnp.maximum(i - 1, 0)
    for h in heads:
        acc_ref[h] += _dot(p_ref[h], values(h, prev)) + _dot(probabilities(h, i, True), values(h, i))
    for h in heads:
        acc = acc_ref[h]
        o_ref[:, h * dh:(h + 1) * dh] = (acc[:, :dh] / acc[:, dh:]).astype(o_ref.dtype)


ATTN_SHIFT_LIMIT = 56.0


def _attention(qkv, f_rows, qk_bound, *, blk):
    batch, seq, _ = qkv.shape
    nblk = seq // blk
    hps = ATTN_HEADS_PER_STEP
    wide = hps * FOX_HEAD_DIM
    third = FOX_HEADS // hps
    specs = [pl.BlockSpec((None, blk, wide), lambda b, h, i: (b, i, h)),
             pl.BlockSpec((None, seq, wide), lambda b, h, i: (b, 0, h + third)),
             pl.BlockSpec((None, seq, wide), lambda b, h, i: (b, 0, h + 2 * third)),
             pl.BlockSpec((None, hps, nblk, 1, blk), lambda b, h, i: (b, h, 0, 0, 0))]
    common = dict(
        grid=(batch, FOX_HEADS // hps, nblk),
        out_specs=pl.BlockSpec((None, blk, wide), lambda b, h, i: (b, i, h)),
        out_shape=jax.ShapeDtypeStruct((batch, seq, FOX_HEADS * FOX_HEAD_DIM), BF16),
        compiler_params=_params("parallel", "parallel", "arbitrary"))

    def shifted(qkv, f_rows, c):
        return pl.pallas_call(
            functools.partial(_attn_shift_kernel, blk=blk),
            in_specs=[pl.BlockSpec(memory_space=pltpu.SMEM)] + specs,
            scratch_shapes=[pltpu.VMEM((hps, blk, 2 * FOX_HEAD_DIM), F32),
                            pltpu.VMEM((hps, blk, FOX_HEAD_DIM), F32),
                            pltpu.VMEM((hps, blk, blk), BF16)],
            name="fox_attention_shift", **common)(c.reshape(1), qkv, qkv, qkv, f_rows)

    def online(qkv, f_rows, c):
        return pl.pallas_call(functools.partial(_attn_kernel, blk=blk), in_specs=specs,
                              name="fox_attention_online", **common)(qkv, qkv, qkv, f_rows)

    return lax.cond(qk_bound <= ATTN_SHIFT_LIMIT, shifted, online, qkv, f_rows, qk_bound)


S5_SLAB_GROUPS = 16
S5_SLAB = S5_SLAB_GROUPS * S5_GROUP
S5_SLAB_STATES = S5_SLAB_GROUPS * S5_STATE
S5_SLABS = S5_GROUPS // S5_SLAB_GROUPS


def _split_bf16(a):
    hi = a.astype(BF16)
    return hi, (a - hi.astype(F32)).astype(BF16)


def _dot_split(a, b_hi, b_lo):
    a_hi, a_lo = _split_bf16(a)
    return _dot(a_hi, b_hi) + _dot(a_hi, b_lo) + _dot(a_lo, b_hi)


def _lam_pow(k, lr, li, dt):
    mag = jnp.exp(k * (lr * dt))
    ang = k * (li * dt)
    return mag * jnp.cos(ang), mag * jnp.sin(ang)


def _s5_matrices_kernel(lam_ref, ls_ref, bdb_ref, bdc_ref, bp_ref, cp_ref, gr_ref, l16_ref,
                        bb_ref, chi_ref, clo_ref):
    t = pl.program_id(1)
    ns = S5_SLAB_STATES
    lr, li = lam_ref[0:1, :], lam_ref[1:2, :]
    dt = jnp.exp(ls_ref[...])

    @pl.when(t == 0)
    def _():
        lbr, lbi = _lam_pow(1.0, lr, li, dt)
        den = lr * lr + li * li
        nr = lbr - 1.0
        fr = (nr * lr + lbi * li) / den
        fi = (lbi * lr - nr * li) / den
        b_re, b_im = bdb_ref[0], bdb_ref[1]
        bb_ref[0] = fr * b_re - fi * b_im
        bb_ref[1] = fr * b_im + fi * b_re
        for part in range(2):
            chi_ref[part], clo_ref[part] = _split_bf16(bdc_ref[part])

    bb_re, bb_im = bb_ref[0], bb_ref[1]
    pr, pi = _lam_pow((S5_CHUNK - 1 - t).astype(F32), lr, li, dt)
    bp_re = bb_re * pr - bb_im * pi
    bp_im = bb_re * pi + bb_im * pr
    bp_ref[:, :ns] = bp_re.astype(BF16)
    bp_ref[:, ns:] = bp_im.astype(BF16)

    qr, qi = _lam_pow((t + 1).astype(F32), lr, li, dt)
    qr = jnp.broadcast_to(qr, (8, ns)).T[:, 0:1]
    qi = jnp.broadcast_to(qi, (8, ns)).T[:, 0:1]
    c_re, c_im = bdc_ref[0], bdc_ref[1]
    cp_ref[:ns, :] = (c_re * qr - c_im * qi).astype(BF16)
    cp_ref[ns:, :] = (-(c_re * qi + c_im * qr)).astype(BF16)

    gr_ref[...] = (_dot_split(bp_re, chi_ref[0], clo_ref[0])
                   - _dot_split(bp_im, chi_ref[1], clo_ref[1])).astype(BF16)

    l16r, l16i = _lam_pow(float(S5_CHUNK), lr, li, dt)
    l16_ref[0:1, :] = l16r
    l16_ref[1:2, :] = l16i


def _s5_matrices(lam, ls, bdb, bdc):
    t_len, w, ns = S5_CHUNK, S5_SLAB, S5_SLAB_STATES
    return pl.pallas_call(
        _s5_matrices_kernel,
        grid=(S5_SLABS, t_len),
        in_specs=[pl.BlockSpec((None, 2, ns), lambda s, t: (s, 0, 0)),
                  pl.BlockSpec((None, 1, ns), lambda s, t: (s, 0, 0)),
                  pl.BlockSpec((None, 2, w, ns), lambda s, t: (s, 0, 0, 0)),
                  pl.BlockSpec((None, 2, ns, w), lambda s, t: (s, 0, 0, 0))],
        out_specs=[pl.BlockSpec((None, w, 2 * ns), lambda s, t: (s, t, 0)),
                   pl.BlockSpec((None, None, 2 * ns, w), lambda s, t: (s, t, 0, 0)),
                   pl.BlockSpec((None, w, w), lambda s, t: (s, t, 0)),
                   pl.BlockSpec((None, 2, ns), lambda s, t: (s, 0, 0))],
        out_shape=[jax.ShapeDtypeStruct((S5_SLABS, t_len * w, 2 * ns), BF16),
                   jax.ShapeDtypeStruct((S5_SLABS, t_len, 2 * ns, w), BF16),
                   jax.ShapeDtypeStruct((S5_SLABS, t_len * w, w), BF16),
                   jax.ShapeDtypeStruct((S5_SLABS, 2, ns), F32)],
        scratch_shapes=[pltpu.VMEM((2, w, ns), F32), pltpu.VMEM((2, ns, w), BF16), pltpu.VMEM((2, ns, w), BF16)],
        compiler_params=_params("parallel", "arbitrary"),
        name="s5_matrices",
    )(lam, ls, bdb, bdc)


def _chunk_rows(x_ref):
    return jnp.concatenate([x_ref[:, t, :] for t in range(S5_CHUNK)], axis=1).astype(BF16)


def _s5_states_kernel(x_ref, bp_ref, s_ref):
    s_ref[...] = _dot(_chunk_rows(x_ref), bp_ref[...])


def _s5_states(x3, bp, *, rows):
    nc = x3.shape[0]
    t_len, w, ns = S5_CHUNK, S5_SLAB, S5_SLAB_STATES
    return pl.pallas_call(
        _s5_states_kernel,
        grid=(S5_SLABS, nc // rows),
        in_specs=[pl.BlockSpec((rows, t_len, w), lambda s, r: (r, 0, s)),
                  pl.BlockSpec((None, t_len * w, 2 * ns), lambda s, r: (s, 0, 0),
                               pipeline_mode=pl.Buffered(1))],
        out_specs=pl.BlockSpec((rows, 2 * ns), lambda s, r: (r, s)),
        out_shape=jax.ShapeDtypeStruct((nc, S5_SLABS * 2 * ns), F32),
        compiler_params=_params("parallel", "arbitrary"),
        name="s5_chunk_states",
    )(x3, bp)


def _s5_chunk_scan_kernel(s_ref, l16_ref, xp_ref, *, batch, chunks):
    ns = S5_SLAB_STATES
    ar, ai = l16_ref[0:1, :], l16_ref[1:2, :]

    def step(c, carry):
        out = []
        for b, (xr, xi) in enumerate(carry):
            row = pl.ds(b * chunks + c, 1)
            xp_ref[row, :ns] = xr
            xp_ref[row, ns:] = xi
            sr = s_ref[row, :ns]
            si = s_ref[row, ns:]
            out.append((ar * xr - ai * xi + sr, ar * xi + ai * xr + si))
        return tuple(out)

    zero = jnp.zeros((1, ns), F32)
    lax.fori_loop(0, chunks, step, ((zero, zero),) * batch)


def _s5_chunk_scan(s, l16, *, batch):
    nc = s.shape[0]
    ns = S5_SLAB_STATES
    return pl.pallas_call(
        functools.partial(_s5_chunk_scan_kernel, batch=batch, chunks=nc // batch),
        grid=(S5_SLABS,),
        in_specs=[pl.BlockSpec((nc, 2 * ns), lambda s_: (0, s_)),
                  pl.BlockSpec((None, 2, ns), lambda s_: (s_, 0, 0))],
        out_specs=pl.BlockSpec((nc, 2 * ns), lambda s_: (0, s_)),
        out_shape=jax.ShapeDtypeStruct(s.shape, F32),
        compiler_params=_params("parallel"),
        name="s5_chunk_scan",
    )(s, l16)


def _s5_output_kernel(x_ref, xp_ref, gr_ref, cp_ref, d_ref, y_ref):
    t_len, w = S5_CHUNK, S5_SLAB
    xc = _chunk_rows(x_ref)
    xp = xp_ref[...].astype(BF16)
    d = d_ref[...]
    for t in range(t_len):
        y = _dot(xc[:, :(t + 1) * w], gr_ref[(t_len - 1 - t) * w:, :])
        y = y + _dot(xp, cp_ref[t])
        y_ref[:, t, :] = y + d * x_ref[:, t, :]


def _s5_output(x3, xp, gr, cp, d, *, rows):
    nc = x3.shape[0]
    t_len, w, ns = S5_CHUNK, S5_SLAB, S5_SLAB_STATES
    once = dict(pipeline_mode=pl.Buffered(1))
    return pl.pallas_call(
        _s5_output_kernel,
        grid=(S5_SLABS, nc // rows),
        in_specs=[pl.BlockSpec((rows, t_len, w), lambda s, r: (r, 0, s)),
                  pl.BlockSpec((rows, 2 * ns), lambda s, r: (r, s)),
                  pl.BlockSpec((None, t_len * w, w), lambda s, r: (s, 0, 0), **once),
                  pl.BlockSpec((None, t_len, 2 * ns, w), lambda s, r: (s, 0, 0, 0), **once),
                  pl.BlockSpec((None, 1, w), lambda s, r: (s, 0, 0))],
        out_specs=pl.BlockSpec((rows, t_len, w), lambda s, r: (r, 0, s)),
        out_shape=jax.ShapeDtypeStruct(x3.shape, F32),
        compiler_params=_params("parallel", "arbitrary"),
        name="s5_chunk_output",
    )(x3, xp, gr, cp, d)


def _s5(s5_in, lam_re, lam_im, log_step, b_re, b_im, c_re, c_im, d, *, batch):
    m, width = s5_in.shape
    sl, sg, p, grp = S5_SLABS, S5_SLAB_GROUPS, S5_STATE, S5_GROUP
    ns = S5_SLAB_STATES
    lam = jnp.stack([lam_re, lam_im], axis=0).astype(F32).reshape(2, sl, ns).transpose(1, 0, 2)
    ls = jnp.repeat(log_step.astype(F32), p).reshape(sl, 1, ns)
    eye = jnp.eye(sg, dtype=F32)
    b = jnp.stack([b_re, b_im], axis=0).astype(F32).reshape(2, sl, sg, p, grp)
    bdb = jnp.einsum('rsgpi,gh->srgihp', b, eye).reshape(sl, 2, sg * grp, ns)
    c = jnp.stack([c_re, c_im], axis=0).astype(F32).reshape(2, sl, sg, grp, p)
    bdc = jnp.einsum('rsgop,gh->srhpgo', c, eye).reshape(sl, 2, ns, sg * grp)
    bp, cp, gr, l16 = _s5_matrices(lam, ls, bdb, bdc)
    x3 = s5_in.reshape(m // S5_CHUNK, S5_CHUNK, width)
    states = _s5_states(x3, bp, rows=S5_STATE_ROWS)
    xprev = _s5_chunk_scan(states, l16, batch=batch)
    y3 = _s5_output(x3, xprev, gr, cp, d.astype(F32).reshape(sl, 1, sg * grp), rows=S5_OUT_ROWS)
    return y3.reshape(m, width)


def _glu_kernel(y_ref, w_ref, b_ref, o_ref):
    g = jax.nn.gelu(y_ref[...])
    o_ref[...] = (g * jax.nn.sigmoid(_dot(g.astype(BF16), w_ref[...]) + b_ref[...])).astype(o_ref.dtype)


def _glu(y, w, b, *, bm=512):
    m, n = y.shape
    return pl.pallas_call(
        _glu_kernel,
        grid=(m // bm,),
        in_specs=[pl.BlockSpec((bm, n), lambda i: (i, 0)),
                  pl.BlockSpec((n, n), lambda i: (0, 0)),
                  pl.BlockSpec((1, n), lambda i: (0, 0))],
        out_specs=pl.BlockSpec((bm, n), lambda i: (i, 0)),
        out_shape=jax.ShapeDtypeStruct((m, n), BF16),
        compiler_params=_params("parallel"),
        name="s5_glu",
    )(y, w, b)


def _merge_kernel(u_ref, a_ref, s_ref, wgf_ref, wgs_ref, wpf_ref, wps_ref, bgf_ref, bgs_ref, wc1_ref, wc2_ref,
                  o_ref, wc1_out_ref, wc2_out_ref):
    _run_cast_jobs((wc1_ref, wc2_ref), (wc1_out_ref, wc2_out_ref))
    u = u_ref[...]
    gate_fox = jax.nn.sigmoid(_dot_nt(u, wgf_ref[...]) + bgf_ref[...])
    gate_s5 = jax.nn.sigmoid(_dot_nt(u, wgs_ref[...]) + bgs_ref[...])
    o_ref[...] = (gate_fox * _dot(a_ref[...], wpf_ref[...])
                  + gate_s5 * _dot(s_ref[...], wps_ref[...])).astype(o_ref.dtype)


def _merge(u, attn, ssm, w_in_t, gate_fox_row, gate_s5_row, wpf, wps, bgf, bgs, cast_1, cast_2, *, bm, bn):
    m, d = u.shape
    n = wpf.shape[1]
    nn = n // bn
    act = lambda a: pl.BlockSpec((bm, a.shape[1]), lambda i, j: (i, 0))
    wgt = lambda w: pl.BlockSpec((w.shape[0], bn), lambda i, j: (0, j))
    assert gate_fox_row % 16 == 0 and gate_s5_row % 16 == 0 and bn % 16 == 0
    gate = lambda row: pl.BlockSpec((pl.Element(bn), pl.Element(d)),
                                    lambda i, j: (pl.multiple_of(row + j * bn, 16), 0))
    step = lambda i, j: i * nn + j
    (spec_1, shape_1), (spec_2, shape_2) = (_cast_job(w, (m // bm) * nn, step) for w in (cast_1, cast_2))
    return pl.pallas_call(
        _merge_kernel,
        grid=(m // bm, nn),
        in_specs=[act(u), act(attn), act(ssm), gate(gate_fox_row), gate(gate_s5_row), wgt(wpf), wgt(wps),
                  pl.BlockSpec((1, bn), lambda i, j: (0, j)),
                  pl.BlockSpec((1, bn), lambda i, j: (0, j)),
                  spec_1, spec_2],
        out_specs=[pl.BlockSpec((bm, bn), lambda i, j: (i, j)), spec_1, spec_2],
        out_shape=[jax.ShapeDtypeStruct((m, n), BF16), shape_1, shape_2],
        compiler_params=_params("parallel", "arbitrary"),
        name="gated_merge",
    )(u, attn, ssm, w_in_t, w_in_t, wpf, wps, bgf, bgs, cast_1, cast_2)


NORM_LANES = 128


def _out_proj_kernel(a_ref, w_ref, x_ref, g_ref, h_ref, hg_ref, r_ref, ssq_ref, *, d_model):
    j = pl.program_id(1)

    @pl.when(j == 0)
    def _():
        ssq_ref[...] = jnp.zeros_like(ssq_ref)

    h = x_ref[...] + _dot(a_ref[...], w_ref[...])
    h_ref[...] = h
    hg_ref[...] = (h * g_ref[...]).astype(hg_ref.dtype)
    ssq_ref[...] += jnp.sum(h * h, axis=-1, keepdims=True)

    @pl.when(j == pl.num_programs(1) - 1)
    def _():
        r_ref[...] = jnp.broadcast_to(lax.rsqrt(ssq_ref[...] * (1.0 / d_model) + RMS_EPS), r_ref.shape)


def _out_proj(a, w, x, g, *, bm, bn):
    m, k = a.shape
    n = w.shape[1]
    return pl.pallas_call(
        functools.partial(_out_proj_kernel, d_model=n),
        grid=(m // bm, n // bn),
        in_specs=[pl.BlockSpec((bm, k), lambda i, j: (i, 0)),
                  pl.BlockSpec((k, bn), lambda i, j: (0, j)),
                  pl.BlockSpec((bm, bn), lambda i, j: (i, j)),
                  pl.BlockSpec((1, bn), lambda i, j: (0, j))],
        out_specs=[pl.BlockSpec((bm, bn), lambda i, j: (i, j)),
                   pl.BlockSpec((bm, bn), lambda i, j: (i, j)),
                   pl.BlockSpec((bm, NORM_LANES), lambda i, j: (i, 0))],
        out_shape=[jax.ShapeDtypeStruct((m, n), F32),
                   jax.ShapeDtypeStruct((m, n), BF16),
                   jax.ShapeDtypeStruct((m, NORM_LANES), F32)],
        scratch_shapes=[pltpu.VMEM((bm, 1), F32)],
        compiler_params=_params("parallel", "arbitrary"),
        name="out_proj_residual",
    )(a, w, x, g)


def _swiglu_kernel(a_ref, r_ref, wg_ref, wu_ref, wc_ref, o_ref, wc_out_ref):
    _run_cast_jobs((wc_ref,), (wc_out_ref,))
    a = a_ref[...]
    r = jnp.concatenate([r_ref[...]] * (o_ref.shape[1] // NORM_LANES), axis=1)
    gate = _dot(a, wg_ref[...]) * r
    up = _dot(a, wu_ref[...]) * r
    o_ref[...] = (jax.nn.silu(gate) * up).astype(o_ref.dtype)


def _swiglu_up(a, r, w_gate_up, d_ff, cast_w, *, bm, bn):
    m, k = a.shape
    ni, nj = m // bm, d_ff // bn
    cast_spec, cast_shape = _cast_job(cast_w, ni * nj, lambda i, j: i * nj + j)
    return pl.pallas_call(
        _swiglu_kernel,
        grid=(ni, nj),
        in_specs=[pl.BlockSpec((bm, k), lambda i, j: (i, 0)),
                  pl.BlockSpec((bm, NORM_LANES), lambda i, j: (i, 0)),
                  pl.BlockSpec((k, bn), lambda i, j: (0, j)),
                  pl.BlockSpec((k, bn), lambda i, j: (0, j + nj)),
                  cast_spec],
        out_specs=[pl.BlockSpec((bm, bn), lambda i, j: (i, j)), cast_spec],
        out_shape=[jax.ShapeDtypeStruct((m, d_ff), BF16), cast_shape],
        compiler_params=_params("parallel", "arbitrary"),
        name="swiglu_up",
    )(a, r, w_gate_up, w_gate_up, cast_w)


def _down_kernel(a_ref, w_ref, r_ref, o_ref):
    o_ref[...] = r_ref[...] + _dot(a_ref[...], w_ref[...])


def _down_residual(a, w, res, *, bm, bn):
    m, k = a.shape
    n = w.shape[1]
    return pl.pallas_call(
        _down_kernel,
        grid=(m // bm, n // bn),
        in_specs=[pl.BlockSpec((bm, k), lambda i, j: (i, 0)),
                  pl.BlockSpec((k, bn), lambda i, j: (0, j)),
                  pl.BlockSpec((bm, bn), lambda i, j: (i, j))],
        out_specs=pl.BlockSpec((bm, bn), lambda i, j: (i, j)),
        out_shape=jax.ShapeDtypeStruct((m, n), F32),
        compiler_params=_params("parallel", "arbitrary"),
        name="down_proj_residual",
    )(a, w, res)


def _layer(x, batch, seq, g_mix, w_in, b_fgate, b_gates, q_norm, k_norm,
           lam_re, lam_im, log_step, b_re, b_im, c_re, c_im, s5_d,
           w_glu, b_glu, w_proj_fox, w_proj_s5, w_out, g_ffn, w_gate_up, w_down):
    m, d_model = x.shape
    fox_w = FOX_HEADS * FOX_HEAD_DIM
    s5_w = S5_GROUP * S5_GROUPS
    col_k, col_v = fox_w, 2 * fox_w
    col_f = 3 * fox_w
    col_s5 = col_f + FOX_HEADS
    col_g = col_s5 + s5_w
    d_ff = w_down.shape[0]
    w_in_t = jnp.swapaxes(w_in, 0, 1).astype(BF16)
    w_f_t, w_s5_t = w_in_t[col_f:col_s5], w_in_t[col_s5:col_g]

    u = _rmsnorm(x, g_mix)

    head_scale = jnp.concatenate([
        jnp.tile(q_norm.astype(F32), FOX_HEADS) * (LOG2_E / math.sqrt(FOX_HEAD_DIM)),
        jnp.tile(k_norm.astype(F32), FOX_HEADS),
        jnp.ones((fox_w,), F32)]).reshape(1, 3 * fox_w)
    qkv = _qkv_proj(u, w_in_t, head_scale, **TILES["qkv_proj"])
    s5_in, f_rows, w_proj_fox_b, w_proj_s5_b = _s5_fgate_proj(
        u, w_s5_t, w_f_t, b_fgate.astype(F32).reshape(FOX_HEADS, 1), batch, seq, ATTN_BLOCK, w_proj_fox, w_proj_s5,
        bs=S5_IN_ROWS)

    qk_bound = (1.02 * LOG2_E * math.sqrt(FOX_HEAD_DIM)
                * jnp.max(jnp.abs(q_norm.astype(F32))) * jnp.max(jnp.abs(k_norm.astype(F32))))
    attn = _attention(qkv.reshape(batch, seq, 3 * fox_w), f_rows, qk_bound, blk=ATTN_BLOCK).reshape(m, fox_w)

    y5 = _s5(s5_in, lam_re, lam_im, log_step, b_re, b_im, c_re, c_im, s5_d, batch=batch)
    ssm = _glu(y5, w_glu.astype(BF16), b_glu.astype(F32).reshape(1, s5_w))

    b_gates = b_gates.astype(F32).reshape(1, 2 * d_model)
    merged, w_out_b, w_gate_up_b = _merge(u, attn, ssm, w_in_t, col_g, col_g + d_model, w_proj_fox_b, w_proj_s5_b,
                                          b_gates[:, :d_model], b_gates[:, d_model:], w_out, w_gate_up,
                                          **TILES["gated_merge"])
    h, hg, h_inv_rms = _out_proj(merged, w_out_b, x, g_ffn.astype(F32).reshape(1, d_model), **TILES["out_proj"])
    act, w_down_b = _swiglu_up(hg, h_inv_rms, w_gate_up_b, d_ff, w_down, **TILES["swiglu_up"])
    return _down_residual(act, w_down_b, h, **TILES["down_proj"])


def kernel(x, g_mix, w_in, b_fgate, b_gates, q_norm, k_norm, s5_lambda_re, s5_lambda_im, s5_log_step, s5_b_re, s5_b_im, s5_c_re, s5_c_im, s5_d, w_glu, b_glu, w_proj_fox, w_proj_s5, w_out, g_ffn, w_gate_up, w_down):
    batch, seq, d_model = x.shape
    h = x.reshape(batch * seq, d_model)
    for l in range(g_mix.shape[0]):
        h = _layer(h, batch, seq, g_mix[l], w_in[l], b_fgate[l], b_gates[l], q_norm[l], k_norm[l],
                   s5_lambda_re[l], s5_lambda_im[l], s5_log_step[l], s5_b_re[l], s5_b_im[l],
                   s5_c_re[l], s5_c_im[l], s5_d[l], w_glu[l], b_glu[l],
                   w_proj_fox[l], w_proj_s5[l], w_out[l], g_ffn[l], w_gate_up[l], w_down[l])
    return h.reshape(batch, seq, d_model)
```
